```python
import math
import jax, jax.numpy as jnp
from jax import lax
import numpy as np

D_MODEL = 1024
BATCH = 8
SEQ = 4096
DEPTH = 2
DEC_BATCH = 16
DEC_SEQ = 32
PAST_LEN = 1024

CHUNK = 64
HEAD_DIM = 64
A_HEADS = 4
A_WIDTH = A_HEADS * HEAD_DIM
IDX_HEADS = 8
IDX_DIM = 64
TOPK_MAX = 256
Q_BLOCK = 128
N_BUCKETS = 32
MAX_DISTANCE = 128
B_HEADS = 4
B_DK = 32
B_DV = 64
B_WIDTH = B_HEADS * B_DV
GATE_RANK = 16
GATE_TAU = 16.0
C_HEADS = 8
C_DK = 64
C_DV = 64
C_WIDTH = C_HEADS * C_DV
CONV_W = 4
D_FF = 4 * D_MODEL
EPS = 1e-6

SPLIT_SIZES = (A_WIDTH, A_WIDTH, A_WIDTH, IDX_HEADS * IDX_DIM, IDX_DIM, IDX_HEADS,
               B_HEADS * B_DK, B_HEADS * B_DK, B_WIDTH, GATE_RANK, B_WIDTH,
               C_HEADS * C_DK, C_HEADS * C_DK, C_WIDTH, C_HEADS, C_HEADS, C_WIDTH)
SPLIT_POINTS = tuple(int(v) for v in np.cumsum(SPLIT_SIZES)[:-1])
D_IN = sum(SPLIT_SIZES)

kernel_name = 'hybrid_dsa_gla_gdn_stream_step'


def rmsnorm(x, g):
    xf = x.astype(jnp.float32)
    y = xf * lax.rsqrt(jnp.mean(xf * xf, axis=-1, keepdims=True) + EPS)
    return (y * g.astype(jnp.float32)).astype(x.dtype)


def l2norm(x):
    return x * lax.rsqrt(jnp.sum(x * x, axis=-1, keepdims=True) + EPS)


def t5_bucket(rel):
    nb = N_BUCKETS // 2
    ret = jnp.where(rel > 0, nb, 0)
    n = jnp.abs(rel)
    max_exact = nb // 2
    nf = jnp.maximum(n, 1).astype(jnp.float32)
    large = max_exact + (jnp.log(nf / max_exact) / math.log(MAX_DISTANCE / max_exact)
                         * (nb - max_exact)).astype(jnp.int32)
    large = jnp.minimum(large, nb - 1)
    return ret + jnp.where(n < max_exact, n, large)


def sparse_attend(q, qi, wi, qpos, k_all, v_all, ki_all, kpos, topk, rel_bias):
    f32 = jnp.float32
    s = jnp.einsum('bthd,bsd->bths', qi, ki_all, preferred_element_type=f32) * (IDX_DIM ** -0.5)
    score = jnp.einsum('bth,bths->bts', wi.astype(f32), jax.nn.relu(s))
    adm = (kpos[None, :] // CHUNK) <= (qpos[:, None] // CHUNK)
    score = jnp.where(adm[None], score, -jnp.inf)
    _, idx = lax.top_k(score, topk)
    bidx = jnp.arange(q.shape[0])[:, None, None]
    k_sel = k_all[bidx, idx]
    v_sel = v_all[bidx, idx]
    sel_pos = kpos[idx]
    valid = (sel_pos // CHUNK) <= (qpos[None, :, None] // CHUNK)
    bias = rel_bias[t5_bucket(sel_pos - qpos[None, :, None])]
    logits = (jnp.einsum('bthd,btkhd->bthk', q, k_sel, preferred_element_type=f32)
              + jnp.swapaxes(bias, 2, 3).astype(f32))
    logits = jnp.where(valid[:, :, None, :], logits, -jnp.inf)
    p = jax.nn.softmax(logits, axis=-1)
    return jnp.einsum('bthk,btkhd->bthd', p.astype(v_sel.dtype), v_sel)


def gla_chunked(q, k, v, log_a, s0):
    bsz, t, nh, dk = q.shape
    dv = v.shape[-1]
    c = min(CHUNK, t)
    n = t // c
    causal = jnp.tril(jnp.ones((c, c), dtype=bool))

    def to_chunks(a):
        return jnp.swapaxes(a.reshape(bsz, n, c, *a.shape[2:]), 0, 1)

    def step(S, inp):
        qc, kc, vc, gc = inp
        b = jnp.cumsum(gc, axis=1)
        o = jnp.einsum('bchk,bhkv->bchv', qc * jnp.exp(b), S)
        diff = b[:, :, None] - b[:, None, :]
        decay = jnp.exp(jnp.where(causal[None, :, :, None, None], diff, -jnp.inf))
        att = jnp.einsum('bihk,bjhk,bijhk->bhij', qc, kc, decay)
        o = o + jnp.einsum('bhij,bjhv->bihv', att, vc)
        b_last = b[:, -1]
        S = (S * jnp.exp(b_last)[..., None]
             + jnp.einsum('bchk,bchv->bhkv', kc * jnp.exp(b_last[:, None] - b), vc))
        return S, o

    S, o = lax.scan(step, s0, (to_chunks(q), to_chunks(k), to_chunks(v), to_chunks(log_a)))
    return jnp.swapaxes(o, 0, 1).reshape(bsz, t, nh, dv), S


def gdn_chunked(q, k, v, g, beta, s0):
    bsz, t, nh, dk = q.shape
    dv = v.shape[-1]
    c = min(CHUNK, t)
    n = t // c
    incl = jnp.tril(jnp.ones((c, c), dtype=bool))
    strict = jnp.tril(jnp.ones((c, c), dtype=bool), -1)
    eye = jnp.eye(c, dtype=jnp.float32)

    def to_chunks(a):
        return jnp.swapaxes(a.reshape(bsz, n, c, *a.shape[2:]), 0, 1)

    def step(S, inp):
        qc, kc, vc, gc, bc = inp
        gcum = jnp.cumsum(gc, axis=1)
        gh = jnp.swapaxes(gcum, 1, 2)
        lmask = jnp.exp(jnp.where(incl, gh[..., :, None] - gh[..., None, :], -jnp.inf))
        kk = jnp.einsum('bihk,bjhk->bhij', kc, kc)
        bh = jnp.swapaxes(bc, 1, 2)
        m = jnp.where(strict, bh[..., :, None] * kk * lmask, 0.0)
        tm = lax.linalg.triangular_solve(eye + m, jnp.broadcast_to(eye, m.shape),
                                         left_side=True, lower=True, unit_diagonal=True)
        u = jnp.einsum('bhij,bjhv->bihv', tm, vc * bc[..., None])
        w = jnp.einsum('bhij,bjhk->bihk', tm, kc * (bc * jnp.exp(gcum))[..., None])
        v_new = u - jnp.einsum('bchk,bhkv->bchv', w, S)
        qk = jnp.einsum('bihk,bjhk->bhij', qc, kc) * lmask
        o = (jnp.einsum('bchk,bhkv->bchv', qc * jnp.exp(gcum)[..., None], S)
             + jnp.einsum('bhij,bjhv->bihv', qk, v_new))
        g_last = gcum[:, -1]
        S = (S * jnp.exp(g_last)[..., None, None]
             + jnp.einsum('bchk,bchv->bhkv', kc * jnp.exp(g_last[:, None] - gcum)[..., None], v_new))
        return S, o

    S, o = lax.scan(step, s0, (to_chunks(q), to_chunks(k), to_chunks(v), to_chunks(g), to_chunks(beta)))
    return jnp.swapaxes(o, 0, 1).reshape(bsz, t, nh, dv), S


def trunk_layer(x, cache, rel_bias, ln1_g, w_in, a_qnorm_g, a_knorm_g, gla_w_gate, gla_b_gate,
                gla_onorm_g, conv_w, gdn_a_log, gdn_dt_bias, gdn_onorm_g, w_o, ln2_g, w_up, w_down):
    f32 = jnp.float32
    bsz, t, _ = x.shape
    dt = x.dtype
    past = 0 if cache is None else cache[0].shape[1]
    h = rmsnorm(x, ln1_g)
    z = h @ w_in
    (aq, ak, av, aqi, aki, awi, bq, bk, bv, bglr, bog,
     cq, ck, cv, ca, cb, cog) = jnp.split(z, list(SPLIT_POINTS), axis=-1)

    q = rmsnorm(aq.reshape(bsz, t, A_HEADS, HEAD_DIM), a_qnorm_g) * (HEAD_DIM ** -0.5)
    k = rmsnorm(ak.reshape(bsz, t, A_HEADS, HEAD_DIM), a_knorm_g)
    v = av.reshape(bsz, t, A_HEADS, HEAD_DIM)
    qi = aqi.reshape(bsz, t, IDX_HEADS, IDX_DIM)
    wi = awi * (IDX_HEADS ** -0.5)
    if cache is None:
        k_all, v_all, ki_all = k, v, aki
    else:
        k_all = jnp.concatenate([cache[0].astype(dt), k], axis=1)
        v_all = jnp.concatenate([cache[1].astype(dt), v], axis=1)
        ki_all = jnp.concatenate([cache[2].astype(dt), aki], axis=1)
    n_keys = past + t
    kpos = jnp.arange(n_keys)
    qpos = past + jnp.arange(t)
    topk = min(TOPK_MAX, n_keys // 4)
    if t > Q_BLOCK and t % Q_BLOCK == 0:
        nb = t // Q_BLOCK

        def to_blocks(a):
            return jnp.swapaxes(a.reshape(bsz, nb, Q_BLOCK, *a.shape[2:]), 0, 1)

        def attend_block(blk):
            qb, qib, wib, pb = blk
            return sparse_attend(qb, qib, wib, pb, k_all, v_all, ki_all, kpos, topk, rel_bias)

        o_a = lax.map(attend_block, (to_blocks(q), to_blocks(qi), to_blocks(wi), qpos.reshape(nb, Q_BLOCK)))
        o_a = jnp.swapaxes(o_a, 0, 1).reshape(bsz, t, A_WIDTH)
    else:
        o_a = sparse_attend(q, qi, wi, qpos, k_all, v_all, ki_all, kpos, topk, rel_bias).reshape(bsz, t, A_WIDTH)

    gq = bq.reshape(bsz, t, B_HEADS, B_DK).astype(f32) * (B_DK ** -0.5)
    gk = bk.reshape(bsz, t, B_HEADS, B_DK).astype(f32)
    gv = bv.reshape(bsz, t, B_HEADS, B_DV).astype(f32)
    log_a = jax.nn.log_sigmoid((bglr @ gla_w_gate + gla_b_gate).astype(f32)).reshape(bsz, t, B_HEADS, B_DK) / GATE_TAU
    s0_gla = jnp.zeros((bsz, B_HEADS, B_DK, B_DV), f32) if cache is None else cache[3].astype(f32)
    o_b, s_gla = gla_chunked(gq, gk, gv, log_a, s0_gla)
    o_b = rmsnorm(o_b, gla_onorm_g) * jax.nn.silu(bog.astype(f32)).reshape(bsz, t, B_HEADS, B_DV)
    o_b = o_b.reshape(bsz, t, B_WIDTH).astype(dt)

    xc = jnp.concatenate([cq, ck, cv], axis=-1)
    hist = (jnp.zeros((bsz, CONV_W - 1, 3 * C_WIDTH), dt) if cache is None else cache[5].astype(dt))
    xpad = jnp.concatenate([hist, xc], axis=1)
    conv = sum(xpad[:, i:i + t] * conv_w[i] for i in range(CONV_W))
    conv = jax.nn.silu(conv.astype(f32))
    new_conv = xpad[:, -(CONV_W - 1):]
    dq, dk_, dv_ = jnp.split(conv, 3, axis=-1)
    dq = l2norm(dq.reshape(bsz, t, C_HEADS, C_DK)) * (C_DK ** -0.5)
    dk_ = l2norm(dk_.reshape(bsz, t, C_HEADS, C_DK))
    dv_ = dv_.reshape(bsz, t, C_HEADS, C_DV)
    beta = jax.nn.sigmoid(cb.astype(f32))
    g = -jnp.exp(gdn_a_log.astype(f32)) * jax.nn.softplus(ca.astype(f32) + gdn_dt_bias.astype(f32))
    s0_gdn = jnp.zeros((bsz, C_HEADS, C_DK, C_DV), f32) if cache is None else cache[4].astype(f32)
    o_c, s_gdn = gdn_chunked(dq, dk_, dv_, g, beta, s0_gdn)
    o_c = rmsnorm(o_c, gdn_onorm_g) * jax.nn.silu(cog.astype(f32)).reshape(bsz, t, C_HEADS, C_DV)
    o_c = o_c.reshape(bsz, t, C_WIDTH).astype(dt)

    x1 = x + jnp.concatenate([o_a.astype(dt), o_b, o_c], axis=-1) @ w_o
    h2 = rmsnorm(x1, ln2_g)
    y = x1 + jnp.square(jax.nn.relu(h2 @ w_up)) @ w_down
    return y, (k, v, aki, s_gla, s_gdn, new_conv)


def setup_inputs(seed: int = 0) -> dict:
    key = jax.random.key(seed)
    ks = jax.random.split(key, 24)
    f32 = jnp.float32

    def nrm(k, shape, s):
        return jax.random.normal(k, shape, f32) * s

    dt_init = jnp.exp(jax.random.uniform(ks[16], (DEPTH, C_HEADS), f32, math.log(1e-3), math.log(1e-1)))
    return {
        'x_prompt': nrm(ks[0], (BATCH, SEQ, D_MODEL), 1.0),
        'x_sample': nrm(ks[1], (DEC_BATCH, DEC_SEQ, D_MODEL), 1.0),
        'cache_a_k': nrm(ks[2], (DEPTH, DEC_BATCH, PAST_LEN, A_HEADS, HEAD_DIM), 1.0),
        'cache_a_v': nrm(ks[3], (DEPTH, DEC_BATCH, PAST_LEN, A_HEADS, HEAD_DIM), 1.0),
        'cache_a_kidx': nrm(ks[4], (DEPTH, DEC_BATCH, PAST_LEN, IDX_DIM), 1.0),
        'state_gla': nrm(ks[5], (DEPTH, DEC_BATCH, B_HEADS, B_DK, B_DV), 1.0),
        'state_gdn': nrm(ks[6], (DEPTH, DEC_BATCH, C_HEADS, C_DK, C_DV), 0.2),
        'state_conv': nrm(ks[7], (DEPTH, DEC_BATCH, CONV_W - 1, 3 * C_WIDTH), 1.0),
        'rel_bias': nrm(ks[8], (N_BUCKETS, A_HEADS), 0.5),
        'ln1_g': 1.0 + nrm(ks[9], (DEPTH, D_MODEL), 0.02),
        'w_in': nrm(ks[10], (DEPTH, D_MODEL, D_IN), D_MODEL ** -0.5),
        'a_qnorm_g': 1.0 + nrm(ks[11], (DEPTH, HEAD_DIM), 0.02),
        'a_knorm_g': 1.0 + nrm(ks[12], (DEPTH, HEAD_DIM), 0.02),
        'gla_w_gate': nrm(ks[13], (DEPTH, GATE_RANK, B_HEADS * B_DK), GATE_RANK ** -0.5),
        'gla_b_gate': nrm(ks[14], (DEPTH, B_HEADS * B_DK), 0.1),
        'gla_onorm_g': 1.0 + nrm(ks[15], (DEPTH, B_DV), 0.02),
        'conv_w': nrm(ks[17], (DEPTH, CONV_W, 3 * C_WIDTH), CONV_W ** -0.5),
        'gdn_a_log': jnp.log(jax.random.uniform(ks[18], (DEPTH, C_HEADS), f32, 1.0, 16.0)),
        'gdn_dt_bias': jnp.log(jnp.expm1(dt_init)),
        'gdn_onorm_g': 1.0 + nrm(ks[19], (DEPTH, C_DV), 0.02),
        'w_o': nrm(ks[20], (DEPTH, D_MODEL, D_MODEL), D_MODEL ** -0.5),
        'ln2_g': 1.0 + nrm(ks[21], (DEPTH, D_MODEL), 0.02),
        'w_up': nrm(ks[22], (DEPTH, D_MODEL, D_FF), D_MODEL ** -0.5),
        'w_down': nrm(ks[23], (DEPTH, D_FF, D_MODEL), D_FF ** -0.5),
    }


def _stack(states, i):
    return jnp.stack([s[i] for s in states], axis=0)


def reference(x_prompt, x_sample, cache_a_k, cache_a_v, cache_a_kidx, state_gla, state_gdn, state_conv,
              rel_bias, ln1_g, w_in, a_qnorm_g, a_knorm_g, gla_w_gate, gla_b_gate, gla_onorm_g, conv_w,
              gdn_a_log, gdn_dt_bias, gdn_onorm_g, w_o, ln2_g, w_up, w_down):
    yp = x_prompt
    ys = x_sample
    new_p = []
    new_s = []
    for l in range(DEPTH):
        yp, st_p = trunk_layer(yp, None, rel_bias, ln1_g[l], w_in[l], a_qnorm_g[l], a_knorm_g[l],
                               gla_w_gate[l], gla_b_gate[l], gla_onorm_g[l], conv_w[l], gdn_a_log[l],
                               gdn_dt_bias[l], gdn_onorm_g[l], w_o[l], ln2_g[l], w_up[l], w_down[l])
        cache_l = (cache_a_k[l], cache_a_v[l], cache_a_kidx[l], state_gla[l], state_gdn[l], state_conv[l])
        ys, st_s = trunk_layer(ys, cache_l, rel_bias, ln1_g[l], w_in[l], a_qnorm_g[l], a_knorm_g[l],
                               gla_w_gate[l], gla_b_gate[l], gla_onorm_g[l], conv_w[l], gdn_a_log[l],
                               gdn_dt_bias[l], gdn_onorm_g[l], w_o[l], ln2_g[l], w_up[l], w_down[l])
        new_p.append(st_p)
        new_s.append(st_s)
    return (yp, ys,
            _stack(new_p, 0), _stack(new_p, 1), _stack(new_p, 2), _stack(new_p, 3), _stack(new_p, 4), _stack(new_p, 5),
            _stack(new_s, 0), _stack(new_s, 1), _stack(new_s, 2), _stack(new_s, 3), _stack(new_s, 4), _stack(new_s, 5))
```

```python
import functools
import math

import numpy as np
import jax
import jax.numpy as jnp
from jax import lax
from jax.experimental import pallas as pl
from jax.experimental.pallas import tpu as pltpu

D_MODEL = 1024
CHUNK = 64
HEAD_DIM = 64
A_HEADS = 4
A_WIDTH = A_HEADS * HEAD_DIM
IDX_HEADS = 8
IDX_DIM = 64
TOPK_MAX = 256
N_BUCKETS = 32
MAX_DISTANCE = 128
B_HEADS = 4
B_DK = 32
B_DV = 64
B_WIDTH = B_HEADS * B_DV
GATE_RANK = 16
GATE_TAU = 16.0
C_HEADS = 8
C_DK = 64
C_DV = 64
C_WIDTH = C_HEADS * C_DV
CONV_W = 4
D_FF = 4 * D_MODEL
EPS = 1e-6

F32 = jnp.float32
BF16 = jnp.bfloat16
_MXU_DTYPE = BF16
_HI = lax.Precision.HIGHEST
_VMEM_LIMIT = 56 * 1024 * 1024
_LANES = 128
_NEG = -1e30
_INT_MIN = -2 ** 31

_SPLIT_NAMES = ("aq", "ak", "av", "aqi", "aki", "awi", "bq", "bk", "bv", "bglr", "bog",
                "cq", "ck", "cv", "ca", "cb", "cog")
_SPLIT_SIZES = (A_WIDTH, A_WIDTH, A_WIDTH, IDX_HEADS * IDX_DIM, IDX_DIM, IDX_HEADS,
                B_HEADS * B_DK, B_HEADS * B_DK, B_WIDTH, GATE_RANK, B_WIDTH,
                C_HEADS * C_DK, C_HEADS * C_DK, C_WIDTH, C_HEADS, C_HEADS, C_WIDTH)
D_IN = sum(_SPLIT_SIZES)
_SRC = dict(zip(_SPLIT_NAMES, np.concatenate([[0], np.cumsum(_SPLIT_SIZES)[:-1]]).tolist()))
_SIZE = dict(zip(_SPLIT_NAMES, _SPLIT_SIZES))

_M_WI, _M_GLR, _M_CA, _M_CB, _M_CA2 = 64, 72, 88, 96, 104
_NEW_ORDER = ("aq", "ak", "av", "aqi", "aki", "awi", "bglr", "ca", "cb", "ca", "pad16",
              "bq", "bk", "bv", "bog", "cq", "ck", "cv", "cog")


def _build_perm():
    perm, offs, pos = [], {}, 0
    for name in _NEW_ORDER:
        if name == "pad16":
            perm += [D_IN] * 16
            pos += 16
            continue
        offs.setdefault(name, pos)
        perm += list(range(_SRC[name], _SRC[name] + _SIZE[name]))
        pos += _SIZE[name]
    return np.asarray(perm, np.int32), offs, pos


_PERM, _OFF, D_Z = _build_perm()
_C_QA, _C_KA, _C_VA, _C_QI, _C_MISC = _OFF["aq"], _OFF["ak"], _OFF["av"], _OFF["aqi"], _OFF["aki"]
_C_B, _C_C = _OFF["bq"], _OFF["cq"]
_W_B = 2 * B_HEADS * B_DK + 2 * B_WIDTH
_W_C = 4 * C_WIDTH
assert _C_MISC % _LANES == 0 and _C_B == _C_MISC + _LANES and _C_C == _C_B + _W_B and D_Z == _C_C + _W_C


def _mm(a, b):
    return jnp.dot(a.astype(_MXU_DTYPE), b.astype(_MXU_DTYPE), preferred_element_type=F32)


def _mm_nt(a, b):
    return lax.dot_general(a.astype(_MXU_DTYPE), b.astype(_MXU_DTYPE), (((1,), (1,)), ((), ())),
                           preferred_element_type=F32)


def _mmx(a, b):
    return jnp.dot(a, b, preferred_element_type=F32, precision=_HI)


def _mmx_nt(a, b):
    return lax.dot_general(a, b, (((1,), (1,)), ((), ())), preferred_element_type=F32, precision=_HI)


def _mmx_tn(a, b):
    return lax.dot_general(a, b, (((0,), (0,)), ((), ())), preferred_element_type=F32, precision=_HI)


def _bmmx(a, b):
    return jnp.einsum("hij,hjk->hik", a, b, preferred_element_type=F32, precision=_HI)


def _rms(x):
    return x * lax.rsqrt(jnp.mean(x * x, axis=-1, keepdims=True) + EPS)


def _group_norm(x, group, mean):
    outs = []
    for g in range(x.shape[-1] // group):
        xs = x[:, g * group:(g + 1) * group]
        ss = jnp.sum(xs * xs, axis=-1, keepdims=True)
        outs.append(xs * lax.rsqrt((ss / group if mean else ss) + EPS))
    return jnp.concatenate(outs, axis=-1)


def _stack_heads(x, n_heads, group):
    grp = lax.broadcasted_iota(jnp.int32, x.shape, 1) // group
    return jnp.concatenate([jnp.where(grp == h, x, 0.0) for h in range(n_heads)], axis=0)


def _diag_select(r, n_heads, group):
    c = r.shape[0] // n_heads
    grp = lax.broadcasted_iota(jnp.int32, (c, r.shape[1]), 1) // group
    out = r[0:c, :]
    for h in range(1, n_heads):
        out = jnp.where(grp == h, r[h * c:(h + 1) * c, :], out)
    return out


def _tri(c):
    return (lax.broadcasted_iota(jnp.int32, (c, c), 0) >= lax.broadcasted_iota(jnp.int32, (c, c), 1)).astype(F32)


def _const_spec(shape):
    zeros = (0,) * len(shape)
    return pl.BlockSpec(shape, lambda *_: zeros)


def _inproj_body(x_ref, g1_ref, w_ref, gq_ref, gk_ref,
                 qa_ref, ka_ref, va_ref, qi_ref, misc_ref, zb_ref, zc_ref):
    h = (_rms(x_ref[...]) * g1_ref[...]).astype(_MXU_DTYPE)

    def proj(c0, width):
        return jnp.dot(h, w_ref[:, c0:c0 + width], preferred_element_type=F32)

    qa_ref[...] = (_group_norm(proj(_C_QA, A_WIDTH), HEAD_DIM, True) * gq_ref[...]
                   * (HEAD_DIM ** -0.5)).astype(qa_ref.dtype)
    ka_ref[...] = _group_norm(proj(_C_KA, A_WIDTH), HEAD_DIM, True) * gk_ref[...]
    va_ref[...] = proj(_C_VA, A_WIDTH)
    qi_ref[...] = (proj(_C_QI, IDX_HEADS * IDX_DIM) * (IDX_DIM ** -0.5)).astype(qi_ref.dtype)
    misc_ref[...] = proj(_C_MISC, _LANES)
    zb_ref[...] = proj(_C_B, _W_B)
    zc_ref[...] = proj(_C_C, _W_C)


def _inproj(x2d, g1, w_perm, gq, gk):
    n = x2d.shape[0]
    tm = min(256, n)
    assert n % tm == 0
    row = lambda w: pl.BlockSpec((tm, w), lambda i: (i, 0))
    out_shapes = (
        jax.ShapeDtypeStruct((n, A_WIDTH), _MXU_DTYPE), jax.ShapeDtypeStruct((n, A_WIDTH), F32),
        jax.ShapeDtypeStruct((n, A_WIDTH), F32), jax.ShapeDtypeStruct((n, IDX_HEADS * IDX_DIM), _MXU_DTYPE),
        jax.ShapeDtypeStruct((n, _LANES), F32), jax.ShapeDtypeStruct((n, _W_B), F32),
        jax.ShapeDtypeStruct((n, _W_C), F32))
    return pl.pallas_call(
        _inproj_body,
        grid=(n // tm,),
        in_specs=[row(D_MODEL), _const_spec((1, D_MODEL)), _const_spec((D_MODEL, D_Z)),
                  _const_spec((1, A_WIDTH)), _const_spec((1, A_WIDTH))],
        out_specs=[row(A_WIDTH), row(A_WIDTH), row(A_WIDTH), row(IDX_HEADS * IDX_DIM), row(_LANES),
                   row(_W_B), row(_W_C)],
        out_shape=out_shapes,
        compiler_params=pltpu.CompilerParams(dimension_semantics=("parallel",), vmem_limit_bytes=_VMEM_LIMIT),
        name="inproj",
    )(x2d, g1, w_perm, gq, gk)


def _attn_body(q_ref, qi_ref, misc_ref, k_ref, v_ref, ki_ref, tbl_ref, o_ref, key_ref, *,
               tq, kt, n_tiles, past, n_keys, topk, tbl_off, n_tbl, idx_bits):
    qb = pl.program_id(1)
    q0 = past + qb * tq
    n_kt = jnp.minimum(n_tiles, (q0 + tq + kt - 1) // kt)
    row = lax.broadcasted_iota(jnp.int32, (tq, kt), 0)
    col = lax.broadcasted_iota(jnp.int32, (tq, kt), 1)
    q_chunk = (q0 + row) // CHUNK
    qi = qi_ref[...]
    misc = misc_ref[...]
    wi = [misc[:, _M_WI + h:_M_WI + h + 1] * (IDX_HEADS ** -0.5) for h in range(IDX_HEADS)]

    def score_tile(t, carry):
        off = pl.multiple_of(t * kt, kt)
        ki_t = ki_ref[pl.ds(off, kt), :]
        acc = jnp.zeros((tq, kt), F32)
        for h in range(IDX_HEADS):
            s = _mm_nt(qi[:, h * IDX_DIM:(h + 1) * IDX_DIM], ki_t)
            acc = acc + wi[h] * jnp.maximum(s, 0.0)
        bits = pltpu.bitcast(acc, jnp.int32)
        bits = jnp.where(bits == _INT_MIN, 0, bits)
        key = jnp.where(bits >= 0, bits, bits ^ jnp.int32(0x7FFFFFFF))
        kpos = off + col
        adm = ((kpos // CHUNK) <= q_chunk) & (kpos < n_keys)
        key_ref[t] = jnp.where(adm, key, _INT_MIN)
        return carry

    lax.fori_loop(0, n_kt, score_tile, 0)

    def count(pred_fn):
        def body(t, cnt):
            w = jnp.where(pred_fn(t, key_ref[t]), 1.0, 0.0)
            part = w[:, 0:_LANES]
            for c in range(1, kt // _LANES):
                part = part + w[:, c * _LANES:(c + 1) * _LANES]
            return cnt + part
        cnt = lax.fori_loop(0, n_kt, body, jnp.zeros((tq, _LANES), F32))
        return jnp.sum(cnt, axis=-1, keepdims=True)

    def thr_bit(i, thr):
        cand = thr + lax.shift_left(jnp.int32(1), 31 - i)
        cnt = count(lambda t, k: k >= cand)
        return jnp.where(cnt >= topk, cand, thr)

    thr = lax.fori_loop(0, 32, thr_bit, jnp.full((tq, 1), _INT_MIN, jnp.int32))
    thr = jnp.maximum(thr, _INT_MIN + 1)

    cnt_gt = count(lambda t, k: k > thr)
    cnt_ge = count(lambda t, k: k >= thr)
    need = topk - cnt_gt
    has_excess = jnp.max(jnp.where(cnt_ge > topk, 1.0, 0.0)) > 0.0

    def tie_search():
        def bit(i, last):
            cand = last + lax.shift_left(jnp.int32(1), idx_bits - 1 - i)
            below = count(lambda t, k: (k == thr) & ((t * kt + col) < cand))
            return jnp.where(below < need, cand, last)
        return lax.fori_loop(0, idx_bits, bit, jnp.zeros((tq, 1), jnp.int32))

    last = lax.cond(has_excess, tie_search, lambda: jnp.full((tq, 1), 2 ** 30, jnp.int32))

    q = q_ref[...]
    for h in range(A_HEADS):
        hs = slice(h * HEAD_DIM, (h + 1) * HEAD_DIM)
        qh = q[:, hs]

        def attend(t, carry, hs=hs, qh=qh, h=h):
            m, l, acc = carry
            off = pl.multiple_of(t * kt, kt)
            s = _mm_nt(qh, k_ref[pl.ds(off, kt), hs])
            s = s + tbl_ref[jnp.clip(t - qb + tbl_off, 0, n_tbl - 1), h]
            key = key_ref[t]
            sel = (key > thr) | ((key == thr) & ((off + col) <= last))
            s = jnp.where(sel, s, _NEG)
            m_new = jnp.maximum(m, jnp.max(s, axis=-1, keepdims=True))
            alpha = jnp.exp(m - m_new)
            p = jnp.exp(s - m_new)
            l = alpha * l + jnp.sum(p, axis=-1, keepdims=True)
            acc = alpha * acc + _mm(p, v_ref[pl.ds(off, kt), hs])
            return m_new, l, acc

        m0 = jnp.full((tq, 1), _NEG, F32)
        _, l, acc = lax.fori_loop(0, n_kt, attend, (m0, jnp.zeros((tq, 1), F32), jnp.zeros((tq, HEAD_DIM), F32)))
        o_ref[:, hs] = acc / l


def _rel_bucket(rel):
    nb = N_BUCKETS // 2
    ret = jnp.where(rel > 0, nb, 0)
    n = jnp.abs(rel)
    max_exact = nb // 2
    nf = jnp.maximum(n, 1).astype(F32)
    large = max_exact + (jnp.log(nf / max_exact) / math.log(MAX_DISTANCE / max_exact)
                         * (nb - max_exact)).astype(jnp.int32)
    large = jnp.minimum(large, nb - 1)
    return ret + jnp.where(n < max_exact, n, large)


def _bias_tables(rel_bias, tq, kt, nqb, n_tiles, past):
    i = jnp.arange(tq)[:, None]
    j = jnp.arange(kt)[None, :]
    if nqb == 1:
        rels = [t * kt + j - (past + i) for t in range(n_tiles)]
        off = 0
    else:
        assert past == 0 and tq == kt and kt >= MAX_DISTANCE
        rels = [j - i - 2 * kt, j - i - kt, j - i]
        off = 2
    tbl = jnp.stack([rel_bias[_rel_bucket(r)] for r in rels], axis=0)
    return jnp.transpose(tbl, (0, 3, 1, 2)).astype(F32), off


def _attn(q, qi, misc, k_all, v_all, ki_all, rel_bias, past):
    bsz, t, _ = q.shape
    n_keys = k_all.shape[1]
    topk = min(TOPK_MAX, n_keys // 4)
    kt = 256
    tq = min(kt, t)
    assert t % tq == 0
    nqb = t // tq
    n_tiles = -(-n_keys // kt)
    pad = n_tiles * kt - n_keys
    assert n_tiles * kt >= topk
    if pad:
        k_all, v_all, ki_all = (jnp.pad(a, ((0, 0), (0, pad), (0, 0))) for a in (k_all, v_all, ki_all))
    tbl, tbl_off = _bias_tables(rel_bias, tq, kt, nqb, n_tiles, past)
    n_tbl = tbl.shape[0]
    body = functools.partial(_attn_body, tq=tq, kt=kt, n_tiles=n_tiles, past=past, n_keys=n_keys, topk=topk,
                             tbl_off=tbl_off, n_tbl=n_tbl, idx_bits=int(n_tiles * kt).bit_length())
    qblk = lambda w: pl.BlockSpec((None, tq, w), lambda b, i: (b, i, 0))
    keys = lambda w: pl.BlockSpec((None, n_tiles * kt, w), lambda b, i: (b, 0, 0))
    return pl.pallas_call(
        body,
        grid=(bsz, nqb),
        in_specs=[qblk(A_WIDTH), qblk(IDX_HEADS * IDX_DIM), qblk(_LANES), keys(A_WIDTH), keys(A_WIDTH),
                  keys(IDX_DIM), _const_spec((n_tbl, A_HEADS, tq, kt))],
        out_specs=qblk(A_WIDTH),
        out_shape=jax.ShapeDtypeStruct((bsz, t, A_WIDTH), F32),
        scratch_shapes=[pltpu.VMEM((n_tiles, tq, kt), jnp.int32)],
        compiler_params=pltpu.CompilerParams(dimension_semantics=("parallel", "parallel"),
                                             vmem_limit_bytes=_VMEM_LIMIT),
        name="attn",
    )(q, qi, misc, k_all, v_all, ki_all, tbl)


def _gla_body(zb_ref, misc_ref, wg_ref, bg_ref, og_ref, s0_ref, o_ref, s_out_ref, s_ref, *, c):
    dkw = B_HEADS * B_DK

    @pl.when(pl.program_id(1) == 0)
    def _():
        s_ref[...] = s0_ref[...]

    zb = zb_ref[...]
    q = zb[:, 0:dkw] * (B_DK ** -0.5)
    k = zb[:, dkw:2 * dkw]
    v = zb[:, 2 * dkw:2 * dkw + B_WIDTH]
    gate = zb[:, 2 * dkw + B_WIDTH:]
    log_a = jax.nn.log_sigmoid(_mmx(misc_ref[...], wg_ref[...]) + bg_ref[...]) / GATE_TAU
    b = _mmx(_tri(c), log_a)
    b_last = b[c - 1:c, :]
    b_mid = b[c // 2 - 1:c // 2, :]
    q_bd = _stack_heads(q * jnp.exp(b - b_mid), B_HEADS, B_DK)
    att = _mmx_nt(q_bd, k * jnp.exp(b_mid - b))
    ri = lax.broadcasted_iota(jnp.int32, att.shape, 0) % c
    cj = lax.broadcasted_iota(jnp.int32, att.shape, 1)
    att = jnp.where(ri >= cj, att, 0.0)
    s = s_ref[...]
    o = _mmx_nt(q * jnp.exp(b), s) + _diag_select(_mmx(att, v), B_HEADS, B_DV)
    upd = _mmx_tn(v, k * jnp.exp(b_last - b))
    bd = (lax.broadcasted_iota(jnp.int32, upd.shape, 0) // B_DV) == (lax.broadcasted_iota(jnp.int32, upd.shape, 1) // B_DK)
    s_new = s * jnp.exp(b_last) + jnp.where(bd, upd, 0.0)
    s_ref[...] = s_new
    s_out_ref[...] = s_new
    o_ref[...] = _group_norm(o, B_DV, True) * og_ref[...] * jax.nn.silu(gate)


def _gla(zb, misc, wg_emb, bg, og, s0_t, c):
    bsz, t, _ = zb.shape
    dkw = B_HEADS * B_DK
    blk = lambda w: pl.BlockSpec((None, c, w), lambda b, i: (b, i, 0))
    st = pl.BlockSpec((None, B_WIDTH, dkw), lambda b, i: (b, 0, 0))
    return pl.pallas_call(
        functools.partial(_gla_body, c=c),
        grid=(bsz, t // c),
        in_specs=[blk(_W_B), blk(_LANES), _const_spec((_LANES, dkw)), _const_spec((1, dkw)),
                  _const_spec((1, B_WIDTH)), st],
        out_specs=[blk(B_WIDTH), st],
        out_shape=(jax.ShapeDtypeStruct((bsz, t, B_WIDTH), F32), jax.ShapeDtypeStruct((bsz, B_WIDTH, dkw), F32)),
        scratch_shapes=[pltpu.VMEM((B_WIDTH, dkw), F32)],
        compiler_params=pltpu.CompilerParams(dimension_semantics=("parallel", "arbitrary"),
                                             vmem_limit_bytes=_VMEM_LIMIT),
        name="gla",
    )(zb, misc, wg_emb, bg, og, s0_t)


def _gdn_body(zc_ref, misc_ref, cw_ref, alog_ref, dtb_ref, og_ref, s0_ref, hist_ref,
              o_ref, s_out_ref, xpad_ref, s_ref, *, c):
    w3 = 3 * C_WIDTH

    @pl.when(pl.program_id(1) == 0)
    def _():
        s_ref[...] = s0_ref[...]
        xpad_ref[0:8, :] = hist_ref[...]

    xpad_ref[8:8 + c, :] = zc_ref[:, 0:w3]
    conv = xpad_ref[5:5 + c, :] * cw_ref[0:1, :]
    for i in range(1, CONV_W):
        conv = conv + xpad_ref[5 + i:5 + i + c, :] * cw_ref[i:i + 1, :]
    xpad_ref[0:8, :] = xpad_ref[c:c + 8, :]
    act = jax.nn.silu(conv)
    q = _group_norm(act[:, 0:C_WIDTH], C_DK, False) * (C_DK ** -0.5)
    k = _group_norm(act[:, C_WIDTH:2 * C_WIDTH], C_DK, False)
    v = act[:, 2 * C_WIDTH:w3]

    misc = misc_ref[...]
    lane = lax.broadcasted_iota(jnp.int32, (c, _LANES), 1)
    g_full = -jnp.exp(alog_ref[...]) * jax.nn.softplus(misc + dtb_ref[...])
    beta_full = jax.nn.sigmoid(misc)
    gcum = _mmx(_tri(c), g_full)

    el = lax.broadcasted_iota(jnp.int32, (_LANES, C_WIDTH), 0)
    eh = lax.broadcasted_iota(jnp.int32, (_LANES, C_WIDTH), 1) // C_DV
    gexp = _mmx(gcum, (el == _M_CA + eh).astype(F32))
    bexp = _mmx(beta_full, (el == _M_CB + eh).astype(F32))
    g_last = gexp[c - 1:c, :]
    eg = jnp.exp(gexp)

    x_bd = jnp.concatenate(
        [jnp.where(lane == _M_CA + h, gcum, 0.0) + jnp.where(lane == _M_CA2 + h, 1.0, 0.0) for h in range(C_HEADS)],
        axis=0)
    y = (jnp.where((lane >= _M_CA) & (lane < _M_CA + C_HEADS), 1.0, 0.0)
         + jnp.where((lane >= _M_CA2) & (lane < _M_CA2 + C_HEADS), -gcum, 0.0))
    diff = _mmx_nt(x_bd, y)
    ri = lax.broadcasted_iota(jnp.int32, diff.shape, 0) % c
    cj = lax.broadcasted_iota(jnp.int32, diff.shape, 1)
    lmask = jnp.where(ri >= cj, jnp.exp(jnp.minimum(diff, 0.0)), 0.0)

    kb = k * bexp
    m = jnp.where(ri > cj, _mmx_nt(_stack_heads(kb, C_HEADS, C_DK), k) * lmask, 0.0)
    qk = _mmx_nt(_stack_heads(q, C_HEADS, C_DK), k) * lmask

    m3 = m.reshape(C_HEADS, c, c)
    eye = (lax.broadcasted_iota(jnp.int32, (c, c), 0) == lax.broadcasted_iota(jnp.int32, (c, c), 1)).astype(F32)
    inv = eye[None] - m3
    pw = m3
    for _ in range(int(math.log2(c)) - 1):
        pw = _bmmx(pw, pw)
        inv = inv + _bmmx(inv, pw)
    r = _mmx(inv.reshape(C_HEADS * c, c), jnp.concatenate([v * bexp, kb * eg], axis=1))
    u = _diag_select(r[:, 0:C_WIDTH], C_HEADS, C_DV)
    w = _diag_select(r[:, C_WIDTH:], C_HEADS, C_DK)

    s = s_ref[...]
    rs = _mmx(jnp.concatenate([w, q * eg], axis=0), s)
    v_new = u - rs[0:c, :]
    o = rs[c:, :] + _diag_select(_mmx(qk, v_new), C_HEADS, C_DV)
    upd = _mmx_tn(k * jnp.exp(g_last - gexp), v_new)
    bd = (lax.broadcasted_iota(jnp.int32, upd.shape, 0) // C_DK) == (lax.broadcasted_iota(jnp.int32, upd.shape, 1) // C_DV)
    s_new = s * jnp.exp(g_last) + jnp.where(bd, upd, 0.0)
    s_ref[...] = s_new
    s_out_ref[...] = s_new
    o_ref[...] = _group_norm(o, C_DV, True) * og_ref[...] * jax.nn.silu(zc_ref[:, w3:])


def _gdn(zc, misc, conv_w, alog_emb, dtb_emb, og, s0_bd, hist8, c):
    bsz, t, _ = zc.shape
    w3 = 3 * C_WIDTH
    blk = lambda w: pl.BlockSpec((None, c, w), lambda b, i: (b, i, 0))
    st = pl.BlockSpec((None, C_WIDTH, C_WIDTH), lambda b, i: (b, 0, 0))
    return pl.pallas_call(
        functools.partial(_gdn_body, c=c),
        grid=(bsz, t // c),
        in_specs=[blk(_W_C), blk(_LANES), _const_spec((CONV_W, w3)), _const_spec((1, _LANES)),
                  _const_spec((1, _LANES)), _const_spec((1, C_WIDTH)), st,
                  pl.BlockSpec((None, 8, w3), lambda b, i: (b, 0, 0))],
        out_specs=[blk(C_WIDTH), st],
        out_shape=(jax.ShapeDtypeStruct((bsz, t, C_WIDTH), F32), jax.ShapeDtypeStruct((bsz, C_WIDTH, C_WIDTH), F32)),
        scratch_shapes=[pltpu.VMEM((8 + c, w3), F32), pltpu.VMEM((C_WIDTH, C_WIDTH), F32)],
        compiler_params=pltpu.CompilerParams(dimension_semantics=("parallel", "arbitrary"),
                                             vmem_limit_bytes=_VMEM_LIMIT),
        name="gdn",
    )(zc, misc, conv_w, alog_emb, dtb_emb, og, s0_bd, hist8)


def _mlp_body(x_ref, oa_ref, ob_ref, oc_ref, wo_ref, g2_ref, wup_ref, wdn_ref, y_ref, *, ff_tile):
    mixed = jnp.concatenate([oa_ref[...], ob_ref[...], oc_ref[...]], axis=-1)
    y_ref[...] = x_ref[...] + _mm(mixed, wo_ref[...])
    h2 = (_rms(y_ref[...]) * g2_ref[...]).astype(_MXU_DTYPE)
    for j in range(D_FF // ff_tile):
        up = jnp.dot(h2, wup_ref[:, j * ff_tile:(j + 1) * ff_tile], preferred_element_type=F32)
        y_ref[...] += _mm(jnp.square(jnp.maximum(up, 0.0)), wdn_ref[j * ff_tile:(j + 1) * ff_tile, :])


def _mlp(x2d, oa, ob, oc, wo, g2, wup, wdn):
    n = x2d.shape[0]
    tm = min(256, n)
    assert n % tm == 0
    row = lambda w: pl.BlockSpec((tm, w), lambda i: (i, 0))
    return pl.pallas_call(
        functools.partial(_mlp_body, ff_tile=1024),
        grid=(n // tm,),
        in_specs=[row(D_MODEL), row(A_WIDTH), row(B_WIDTH), row(C_WIDTH), _const_spec((D_MODEL, D_MODEL)),
                  _const_spec((1, D_MODEL)), _const_spec((D_MODEL, D_FF)), _const_spec((D_FF, D_MODEL))],
        out_specs=row(D_MODEL),
        out_shape=jax.ShapeDtypeStruct((n, D_MODEL), F32),
        compiler_params=pltpu.CompilerParams(dimension_semantics=("parallel",), vmem_limit_bytes=_VMEM_LIMIT),
        name="mlp",
    )(x2d, oa, ob, oc, wo, g2, wup, wdn)


def _prep_layer_params(rel_bias, ln1_g, w_in, a_qnorm_g, a_knorm_g, gla_w_gate, gla_b_gate, gla_onorm_g,
                       conv_w, gdn_a_log, gdn_dt_bias, gdn_onorm_g, w_o, ln2_g, w_up, w_down):
    w_ext = jnp.concatenate([w_in, jnp.zeros((D_MODEL, 1), w_in.dtype)], axis=1)
    lane_emb = lambda vec, starts: sum(
        jnp.zeros((1, _LANES), F32).at[0, s:s + vec.shape[0]].set(vec.astype(F32)) for s in starts)
    return dict(
        rel_bias=rel_bias.astype(F32),
        g1=ln1_g.reshape(1, D_MODEL).astype(F32),
        w_perm=jnp.take(w_ext, jnp.asarray(_PERM), axis=1).astype(_MXU_DTYPE),
        gq=jnp.tile(a_qnorm_g.astype(F32), A_HEADS).reshape(1, A_WIDTH),
        gk=jnp.tile(a_knorm_g.astype(F32), A_HEADS).reshape(1, A_WIDTH),
        wg_emb=jnp.zeros((_LANES, B_HEADS * B_DK), F32).at[_M_GLR:_M_GLR + GATE_RANK].set(gla_w_gate.astype(F32)),
        bg=gla_b_gate.reshape(1, -1).astype(F32),
        og_b=jnp.tile(gla_onorm_g.astype(F32), B_HEADS).reshape(1, B_WIDTH),
        conv_w=conv_w.astype(F32),
        alog_emb=lane_emb(gdn_a_log, (_M_CA, _M_CA2)),
        dtb_emb=lane_emb(gdn_dt_bias, (_M_CA, _M_CA2)),
        og_c=jnp.tile(gdn_onorm_g.astype(F32), C_HEADS).reshape(1, C_WIDTH),
        wo=w_o.astype(_MXU_DTYPE), g2=ln2_g.reshape(1, D_MODEL).astype(F32),
        wup=w_up.astype(_MXU_DTYPE), wdn=w_down.astype(_MXU_DTYPE),
    )


def _layer(x, cache, p):
    bsz, t, _ = x.shape
    n = bsz * t
    assert t >= CONV_W - 1
    c = min(CHUNK, t)
    x2d = x.reshape(n, D_MODEL)
    qa, ka, va, qi, misc, zb, zc = _inproj(x2d, p["g1"], p["w_perm"], p["gq"], p["gk"])
    r3 = lambda a: a.reshape(bsz, t, a.shape[-1])
    aki = misc[:, 0:IDX_DIM]

    if cache is None:
        past = 0
        k_all, v_all, ki_all = r3(ka), r3(va), r3(aki)
        s0_gla = jnp.zeros((bsz, B_WIDTH, B_HEADS * B_DK), F32)
        s0_gdn = jnp.zeros((bsz, C_WIDTH, C_WIDTH), F32)
        hist8 = jnp.zeros((bsz, 8, 3 * C_WIDTH), F32)
    else:
        ck, cv, cki, sg, sd, cbuf = cache
        past = ck.shape[1]
        k_all = jnp.concatenate([ck.reshape(bsz, past, A_WIDTH).astype(F32), r3(ka)], axis=1)
        v_all = jnp.concatenate([cv.reshape(bsz, past, A_WIDTH).astype(F32), r3(va)], axis=1)
        ki_all = jnp.concatenate([cki.astype(F32), r3(aki)], axis=1)
        s0_gla = jnp.einsum("bhkv,hg->bhvgk", sg.astype(F32), jnp.eye(B_HEADS, dtype=F32)).reshape(
            bsz, B_WIDTH, B_HEADS * B_DK)
        s0_gdn = jnp.einsum("bhkv,hg->bhkgv", sd.astype(F32), jnp.eye(C_HEADS, dtype=F32)).reshape(
            bsz, C_WIDTH, C_WIDTH)
        hist8 = jnp.pad(cbuf.astype(F32), ((0, 0), (8 - (CONV_W - 1), 0), (0, 0)))

    o_a = _attn(r3(qa), r3(qi), r3(misc), k_all.astype(_MXU_DTYPE), v_all.astype(_MXU_DTYPE),
                ki_all.astype(_MXU_DTYPE), p["rel_bias"], past)
    o_b, s_gla_t = _gla(r3(zb), r3(misc), p["wg_emb"], p["bg"], p["og_b"], s0_gla, c)
    o_c, s_gdn_bd = _gdn(r3(zc), r3(misc), p["conv_w"], p["alog_emb"], p["dtb_emb"], p["og_c"], s0_gdn, hist8, c)
    y = _mlp(x2d, o_a.reshape(n, A_WIDTH), o_b.reshape(n, B_WIDTH), o_c.reshape(n, C_WIDTH),
             p["wo"], p["g2"], p["wup"], p["wdn"]).reshape(bsz, t, D_MODEL)

    s_gla = jnp.stack([s_gla_t[:, h * B_DV:(h + 1) * B_DV, h * B_DK:(h + 1) * B_DK] for h in range(B_HEADS)], axis=1)
    s_gla = jnp.swapaxes(s_gla, 2, 3)
    s_gdn = jnp.stack([s_gdn_bd[:, h * C_DK:(h + 1) * C_DK, h * C_DV:(h + 1) * C_DV] for h in range(C_HEADS)], axis=1)
    new_conv = r3(zc)[:, t - (CONV_W - 1):, 0:3 * C_WIDTH]
    state = (ka.reshape(bsz, t, A_HEADS, HEAD_DIM), va.reshape(bsz, t, A_HEADS, HEAD_DIM), r3(aki),
             s_gla, s_gdn, new_conv)
    return y, state


def kernel(x_prompt, x_sample, cache_a_k, cache_a_v, cache_a_kidx, state_gla, state_gdn, state_conv, rel_bias, ln1_g, w_in, a_qnorm_g, a_knorm_g, gla_w_gate, gla_b_gate, gla_onorm_g, conv_w, gdn_a_log, gdn_dt_bias, gdn_onorm_g, w_o, ln2_g, w_up, w_down):
    depth = ln1_g.shape[0]
    yp, ys = x_prompt, x_sample
    new_p, new_s = [], []
    for l in range(depth):
        p = _prep_layer_params(rel_bias, ln1_g[l], w_in[l], a_qnorm_g[l], a_knorm_g[l], gla_w_gate[l],
                               gla_b_gate[l], gla_onorm_g[l], conv_w[l], gdn_a_log[l], gdn_dt_bias[l],
                               gdn_onorm_g[l], w_o[l], ln2_g[l], w_up[l], w_down[l])
        yp, st_p = _layer(yp, None, p)
        cache_l = (cache_a_k[l], cache_a_v[l], cache_a_kidx[l], state_gla[l], state_gdn[l], state_conv[l])
        ys, st_s = _layer(ys, cache_l, p)
        new_p.append(st_p)
        new_s.append(st_s)
    stack = lambda states, i: jnp.stack([s[i] for s in states], axis=0)
    return (yp, ys) + tuple(stack(new_p, i) for i in range(6)) + tuple(stack(new_s, i) for i in range(6))
```

```python
import functools
import math

import numpy as np
import jax
import jax.numpy as jnp
from jax import lax
from jax.experimental import pallas as pl
from jax.experimental.pallas import tpu as pltpu

D_MODEL = 1024
CHUNK = 64
HEAD_DIM = 64
A_HEADS = 4
A_WIDTH = A_HEADS * HEAD_DIM
IDX_HEADS = 8
IDX_DIM = 64
TOPK_MAX = 256
N_BUCKETS = 32
MAX_DISTANCE = 128
B_HEADS = 4
B_DK = 32
B_DV = 64
B_WIDTH = B_HEADS * B_DV
GATE_RANK = 16
GATE_TAU = 16.0
C_HEADS = 8
C_DK = 64
C_DV = 64
C_WIDTH = C_HEADS * C_DV
CONV_W = 4
D_FF = 4 * D_MODEL
EPS = 1e-6

F32 = jnp.float32
BF16 = jnp.bfloat16
_MXU_DTYPE = BF16
_HI = lax.Precision.HIGHEST
_VMEM_LIMIT = 56 * 1024 * 1024
_LANES = 128
_NEG = -1e30
_INT_MIN = -2 ** 31

_SPLIT_NAMES = ("aq", "ak", "av", "aqi", "aki", "awi", "bq", "bk", "bv", "bglr", "bog",
                "cq", "ck", "cv", "ca", "cb", "cog")
_SPLIT_SIZES = (A_WIDTH, A_WIDTH, A_WIDTH, IDX_HEADS * IDX_DIM, IDX_DIM, IDX_HEADS,
                B_HEADS * B_DK, B_HEADS * B_DK, B_WIDTH, GATE_RANK, B_WIDTH,
                C_HEADS * C_DK, C_HEADS * C_DK, C_WIDTH, C_HEADS, C_HEADS, C_WIDTH)
D_IN = sum(_SPLIT_SIZES)
_SRC = dict(zip(_SPLIT_NAMES, np.concatenate([[0], np.cumsum(_SPLIT_SIZES)[:-1]]).tolist()))
_SIZE = dict(zip(_SPLIT_NAMES, _SPLIT_SIZES))

_M_WI, _M_GLR, _M_CA, _M_CB = 64, 72, 88, 96
_NEW_ORDER = ("aq", "ak", "av", "aqi", "aki", "awi", "bglr", "ca", "cb", "pad24",
              "bq", "bk", "bv", "bog", "cq", "ck", "cv", "cog")
_PAD = {"pad24": 24}


def _new_offsets():
    offs, pos = {}, 0
    for name in _NEW_ORDER:
        offs[name] = pos
        pos += _PAD.get(name, 0) or _SIZE[name]
    return offs, pos


_OFF, D_Z = _new_offsets()
_C_QA, _C_KA, _C_VA, _C_QI, _C_MISC = _OFF["aq"], _OFF["ak"], _OFF["av"], _OFF["aqi"], _OFF["aki"]
_C_B, _C_C = _OFF["bq"], _OFF["cq"]
_W_B = 2 * B_HEADS * B_DK + 2 * B_WIDTH
_W_C = 4 * C_WIDTH
assert _C_MISC % _LANES == 0 and _C_B == _C_MISC + _LANES and _C_C == _C_B + _W_B and D_Z == _C_C + _W_C
assert (_OFF["awi"], _OFF["bglr"], _OFF["ca"], _OFF["cb"]) == tuple(_C_MISC + m for m in (_M_WI, _M_GLR, _M_CA, _M_CB))


def _mm(a, b):
    return jnp.dot(a.astype(_MXU_DTYPE), b.astype(_MXU_DTYPE), preferred_element_type=F32)


def _mm_nt(a, b):
    return lax.dot_general(a.astype(_MXU_DTYPE), b.astype(_MXU_DTYPE), (((1,), (1,)), ((), ())),
                           preferred_element_type=F32)


def _mm_tn(a, b):
    return lax.dot_general(a.astype(_MXU_DTYPE), b.astype(_MXU_DTYPE), (((0,), (0,)), ((), ())),
                           preferred_element_type=F32)


def _mmx(a, b):
    return jnp.dot(a, b, preferred_element_type=F32, precision=_HI)


def _bmm3(a, b):
    def split(x):
        hi = x.astype(BF16)
        return hi, (x - hi.astype(F32)).astype(BF16)
    bmm = lambda u, v: jnp.einsum("hij,hjk->hik", u, v, preferred_element_type=F32)
    (ah, al), (bh, bl) = split(a), split(b)
    return bmm(ah, bh) + (bmm(ah, bl) + bmm(al, bh))


def _cumsum_rows(x):
    r = lax.broadcasted_iota(jnp.int32, x.shape, 0)
    s = 1
    while s < x.shape[0]:
        x = x + jnp.where(r >= s, pltpu.roll(x, s, 0), 0.0)
        s *= 2
    return x


def _expand_heads(x, lane0, n_heads, width):
    c = x.shape[0]
    return jnp.concatenate([jnp.broadcast_to(x[:, lane0 + h:lane0 + h + 1], (c, width)) for h in range(n_heads)],
                           axis=-1)


def _rms(x):
    return x * lax.rsqrt(jnp.mean(x * x, axis=-1, keepdims=True) + EPS)


def _group_norm(x, group, mean):
    outs = []
    for g in range(x.shape[-1] // group):
        xs = x[:, g * group:(g + 1) * group]
        ss = jnp.sum(xs * xs, axis=-1, keepdims=True)
        outs.append(xs * lax.rsqrt((ss / group if mean else ss) + EPS))
    return jnp.concatenate(outs, axis=-1)


def _stack_heads(x, n_heads, group):
    grp = lax.broadcasted_iota(jnp.int32, x.shape, 1) // group
    return jnp.concatenate([jnp.where(grp == h, x, 0.0) for h in range(n_heads)], axis=0)


def _diag_select(r, n_heads, group):
    c = r.shape[0] // n_heads
    grp = lax.broadcasted_iota(jnp.int32, (c, r.shape[1]), 1) // group
    out = r[0:c, :]
    for h in range(1, n_heads):
        out = jnp.where(grp == h, r[h * c:(h + 1) * c, :], out)
    return out


def _fold_rows(w):
    parts = [w[i * 8:(i + 1) * 8, :] for i in range(w.shape[0] // 8)]
    while len(parts) > 1:
        parts = [parts[i] + parts[i + 1] for i in range(0, len(parts), 2)]
    return parts[0]


def _const_spec(shape):
    zeros = (0,) * len(shape)
    return pl.BlockSpec(shape, lambda *_: zeros)


def _inproj_body(x_ref, g1_ref, w_ref, gq_ref, gk_ref,
                 qa_ref, ka_ref, va_ref, qi_ref, misc_ref, zb_ref, zc_ref):
    h = (_rms(x_ref[...]) * g1_ref[...]).astype(_MXU_DTYPE)

    def proj(c0, width):
        return jnp.dot(h, w_ref[:, c0:c0 + width], preferred_element_type=F32)

    qa_ref[...] = (_group_norm(proj(_C_QA, A_WIDTH), HEAD_DIM, True) * gq_ref[...]
                   * (HEAD_DIM ** -0.5)).astype(qa_ref.dtype)
    ka_ref[...] = _group_norm(proj(_C_KA, A_WIDTH), HEAD_DIM, True) * gk_ref[...]
    va_ref[...] = proj(_C_VA, A_WIDTH)
    qi_ref[...] = (proj(_C_QI, IDX_HEADS * IDX_DIM) * (IDX_DIM ** -0.5)).astype(qi_ref.dtype)
    misc_ref[...] = proj(_C_MISC, _LANES)
    zb_ref[...] = proj(_C_B, _W_B)
    zc_ref[...] = proj(_C_C, _W_C)


def _inproj(x2d, g1, w_perm, gq, gk):
    n = x2d.shape[0]
    tm = min(256, n)
    assert n % tm == 0
    row = lambda w: pl.BlockSpec((tm, w), lambda i: (i, 0))
    out_shapes = (
        jax.ShapeDtypeStruct((n, A_WIDTH), _MXU_DTYPE), jax.ShapeDtypeStruct((n, A_WIDTH), F32),
        jax.ShapeDtypeStruct((n, A_WIDTH), F32), jax.ShapeDtypeStruct((n, IDX_HEADS * IDX_DIM), _MXU_DTYPE),
        jax.ShapeDtypeStruct((n, _LANES), F32), jax.ShapeDtypeStruct((n, _W_B), F32),
        jax.ShapeDtypeStruct((n, _W_C), F32))
    return pl.pallas_call(
        _inproj_body,
        grid=(n // tm,),
        in_specs=[row(D_MODEL), _const_spec((1, D_MODEL)), _const_spec((D_MODEL, D_Z)),
                  _const_spec((1, A_WIDTH)), _const_spec((1, A_WIDTH))],
        out_specs=[row(A_WIDTH), row(A_WIDTH), row(A_WIDTH), row(IDX_HEADS * IDX_DIM), row(_LANES),
                   row(_W_B), row(_W_C)],
        out_shape=out_shapes,
        compiler_params=pltpu.CompilerParams(dimension_semantics=("parallel",), vmem_limit_bytes=_VMEM_LIMIT),
        name="inproj",
    )(x2d, g1, w_perm, gq, gk)


def _attn_body(q_ref, qi_ref, wi_ref, k_ref, vt_ref, ki_ref, tbl_ref, o_ref, key_ref, acc_ref, *,
               tq, kt, n_tiles, past, n_keys, topk, tbl_off, n_tbl, idx_bits):
    qb = pl.program_id(1)
    q0 = past + qb * tq
    n_kt = jnp.minimum(n_tiles, (q0 + tq + kt - 1) // kt)
    krow = lax.broadcasted_iota(jnp.int32, (kt, tq), 0)
    q_chunk = (q0 + lax.broadcasted_iota(jnp.int32, (kt, tq), 1)) // CHUNK
    wi = wi_ref[...] * (IDX_HEADS ** -0.5)

    def score_tile(t, carry):
        off = pl.multiple_of(t * kt, kt)
        ki_t = ki_ref[pl.ds(off, kt), :]
        acc = jnp.zeros((kt, tq), F32)
        for h in range(IDX_HEADS):
            acc = acc + wi[h:h + 1, :] * jnp.maximum(_mm_nt(ki_t, qi_ref[h]), 0.0)
        bits = pltpu.bitcast(acc, jnp.int32)
        bits = jnp.where(bits == _INT_MIN, 0, bits)
        key = jnp.where(bits >= 0, bits, bits ^ jnp.int32(0x7FFFFFFF))
        kpos = off + krow
        adm = ((kpos // CHUNK) <= q_chunk) & (kpos < n_keys)
        key_ref[t] = jnp.where(adm, key, _INT_MIN)
        return carry

    lax.fori_loop(0, n_kt, score_tile, 0)

    def count(pred_fn):
        def body(t, cnt):
            return cnt + _fold_rows(jnp.where(pred_fn(t, key_ref[t]), 1.0, 0.0))
        cnt = lax.fori_loop(0, n_kt, body, jnp.zeros((8, tq), F32))
        return jnp.sum(cnt, axis=0, keepdims=True)

    def thr_bit(i, thr):
        cand = thr + lax.shift_left(jnp.int32(1), 31 - i)
        cnt = count(lambda t, k: k >= cand)
        return jnp.where(cnt >= topk, cand, thr)

    thr = lax.fori_loop(0, 32, thr_bit, jnp.full((1, tq), _INT_MIN, jnp.int32))
    thr = jnp.maximum(thr, _INT_MIN + 1)

    cnt_gt = count(lambda t, k: k > thr)
    cnt_ge = count(lambda t, k: k >= thr)
    need = topk - cnt_gt
    has_excess = jnp.max(jnp.where(cnt_ge > topk, 1.0, 0.0)) > 0.0

    def tie_search():
        def bit(i, last):
            cand = last + lax.shift_left(jnp.int32(1), idx_bits - 1 - i)
            below = count(lambda t, k: (k == thr) & ((t * kt + krow) < cand))
            return jnp.where(below < need, cand, last)
        return lax.fori_loop(0, idx_bits, bit, jnp.zeros((1, tq), jnp.int32))

    last = lax.cond(has_excess, tie_search, lambda: jnp.full((1, tq), 2 ** 30, jnp.int32))

    acc_ref[...] = jnp.zeros(acc_ref.shape, F32)

    def attend(t, carry):
        ms, ls = carry
        off = pl.multiple_of(t * kt, kt)
        key = key_ref[t]
        sel = (key > thr) | ((key == thr) & ((off + krow) <= last))
        ti = jnp.clip(t - qb + tbl_off, 0, n_tbl - 1)
        new_ms, new_ls = [], []
        for h in range(A_HEADS):
            s = _mm_nt(k_ref[h, pl.ds(off, kt), :], q_ref[h])
            s = jnp.where(sel, s + tbl_ref[ti, h], _NEG)
            m_new = jnp.maximum(ms[h], jnp.max(s, axis=0, keepdims=True))
            alpha = jnp.exp(ms[h] - m_new)
            p = jnp.exp(s - m_new)
            new_ls.append(alpha * ls[h] + jnp.sum(p, axis=0, keepdims=True))
            new_ms.append(m_new)
            acc_ref[h] = alpha * acc_ref[h] + _mm(vt_ref[h, t], p)
        return tuple(new_ms), tuple(new_ls)

    m0 = tuple(jnp.full((1, tq), _NEG, F32) for _ in range(A_HEADS))
    l0 = tuple(jnp.zeros((1, tq), F32) for _ in range(A_HEADS))
    _, ls = lax.fori_loop(0, n_kt, attend, (m0, l0))
    for h in range(A_HEADS):
        o_ref[h] = acc_ref[h] / ls[h]


def _rel_bucket(rel):
    nb = N_BUCKETS // 2
    ret = jnp.where(rel > 0, nb, 0)
    n = jnp.abs(rel)
    max_exact = nb // 2
    nf = jnp.maximum(n, 1).astype(F32)
    large = max_exact + (jnp.log(nf / max_exact) / math.log(MAX_DISTANCE / max_exact)
                         * (nb - max_exact)).astype(jnp.int32)
    large = jnp.minimum(large, nb - 1)
    return ret + jnp.where(n < max_exact, n, large)


def _attn_tiling(t, n_keys):
    kt = 256
    tq = min(kt, t)
    assert t % tq == 0
    return tq, kt, t // tq, -(-n_keys // kt)


def _bias_tables(rel_bias, t, n_keys, past):
    tq, kt, nqb, n_tiles = _attn_tiling(t, n_keys)
    j = jnp.arange(kt)[:, None]
    i = jnp.arange(tq)[None, :]
    if nqb == 1:
        rels = [tile * kt + j - (past + i) for tile in range(n_tiles)]
        off = 0
    else:
        assert past == 0 and tq == kt and kt >= MAX_DISTANCE
        rels = [j - i - 2 * kt, j - i - kt, j - i]
        off = 2
    onehot = jax.nn.one_hot(_rel_bucket(jnp.stack(rels, axis=0)), N_BUCKETS, dtype=F32)
    tbl = jnp.einsum("nktb,bh->nhkt", onehot, rel_bias.astype(F32), precision=_HI)
    return tbl, off


def _attn(q, qi, wi, k_all, v_all, ki_all, tbl, tbl_off, past):
    bsz, t, _ = q.shape
    n_keys = k_all.shape[1]
    topk = min(TOPK_MAX, n_keys // 4)
    tq, kt, nqb, n_tiles = _attn_tiling(t, n_keys)
    s_pad = n_tiles * kt
    assert s_pad >= topk
    if s_pad != n_keys:
        k_all, v_all, ki_all = (jnp.pad(a, ((0, 0), (0, s_pad - n_keys), (0, 0))) for a in (k_all, v_all, ki_all))
    q_hm = q.reshape(bsz, t, A_HEADS, HEAD_DIM).transpose(0, 2, 1, 3)
    qi_hm = qi.reshape(bsz, t, IDX_HEADS, IDX_DIM).transpose(0, 2, 1, 3)
    wi_t = wi.transpose(0, 2, 1)
    k_hm = k_all.reshape(bsz, s_pad, A_HEADS, HEAD_DIM).transpose(0, 2, 1, 3)
    vt_hm = v_all.reshape(bsz, n_tiles, kt, A_HEADS, HEAD_DIM).transpose(0, 3, 1, 4, 2)
    n_tbl = tbl.shape[0]
    body = functools.partial(_attn_body, tq=tq, kt=kt, n_tiles=n_tiles, past=past, n_keys=n_keys, topk=topk,
                             tbl_off=tbl_off, n_tbl=n_tbl, idx_bits=int(s_pad).bit_length())
    o_t = pl.pallas_call(
        body,
        grid=(bsz, nqb),
        in_specs=[pl.BlockSpec((None, A_HEADS, tq, HEAD_DIM), lambda b, i: (b, 0, i, 0)),
                  pl.BlockSpec((None, IDX_HEADS, tq, IDX_DIM), lambda b, i: (b, 0, i, 0)),
                  pl.BlockSpec((None, IDX_HEADS, tq), lambda b, i: (b, 0, i)),
                  pl.BlockSpec((None, A_HEADS, s_pad, HEAD_DIM), lambda b, i: (b, 0, 0, 0)),
                  pl.BlockSpec((None, A_HEADS, n_tiles, HEAD_DIM, kt), lambda b, i: (b, 0, 0, 0, 0)),
                  pl.BlockSpec((None, s_pad, IDX_DIM), lambda b, i: (b, 0, 0)),
                  _const_spec((n_tbl, A_HEADS, kt, tq))],
        out_specs=pl.BlockSpec((None, A_HEADS, HEAD_DIM, tq), lambda b, i: (b, 0, 0, i)),
        out_shape=jax.ShapeDtypeStruct((bsz, A_HEADS, HEAD_DIM, t), F32),
        scratch_shapes=[pltpu.VMEM((n_tiles, kt, tq), jnp.int32), pltpu.VMEM((A_HEADS, HEAD_DIM, tq), F32)],
        compiler_params=pltpu.CompilerParams(dimension_semantics=("parallel", "parallel"),
                                             vmem_limit_bytes=_VMEM_LIMIT),
        name="attn",
    )(q_hm, qi_hm, wi_t, k_hm, vt_hm, ki_all, tbl)
    return o_t.transpose(0, 3, 1, 2).reshape(bsz, t, A_WIDTH)


def _gla_body(zb_ref, misc_ref, wg_ref, bg_ref, og_ref, s0_ref, o_ref, s_out_ref, s_ref, *, c):
    dkw = B_HEADS * B_DK

    @pl.when(pl.program_id(1) == 0)
    def _():
        s_ref[...] = s0_ref[...]

    zb = zb_ref[...]
    q = zb[:, 0:dkw] * (B_DK ** -0.5)
    k = zb[:, dkw:2 * dkw]
    v = zb[:, 2 * dkw:2 * dkw + B_WIDTH]
    gate = zb[:, 2 * dkw + B_WIDTH:]
    log_a = jax.nn.log_sigmoid(_mmx(misc_ref[...], wg_ref[...]) + bg_ref[...]) / GATE_TAU
    b = _cumsum_rows(log_a)
    b_last = b[c - 1:c, :]
    b_mid = b[c // 2 - 1:c // 2, :]
    q_bd = _stack_heads(q * jnp.exp(b - b_mid), B_HEADS, B_DK)
    att = _mm_nt(q_bd, k * jnp.exp(b_mid - b))
    ri = lax.broadcasted_iota(jnp.int32, att.shape, 0) % c
    cj = lax.broadcasted_iota(jnp.int32, att.shape, 1)
    att = jnp.where(ri >= cj, att, 0.0)
    s = s_ref[...]
    o = _mm_nt(q * jnp.exp(b), s) + _diag_select(_mm(att, v), B_HEADS, B_DV)
    upd = _mm_tn(v, k * jnp.exp(b_last - b))
    bd = (lax.broadcasted_iota(jnp.int32, upd.shape, 0) // B_DV) == (lax.broadcasted_iota(jnp.int32, upd.shape, 1) // B_DK)
    s_new = s * jnp.exp(b_last) + jnp.where(bd, upd, 0.0)
    s_ref[...] = s_new
    s_out_ref[...] = s_new
    o_ref[...] = _group_norm(o, B_DV, True) * og_ref[...] * jax.nn.silu(gate)


def _gla(zb, misc, wg_emb, bg, og, s0_t, c):
    bsz, t, _ = zb.shape
    dkw = B_HEADS * B_DK
    blk = lambda w: pl.BlockSpec((None, c, w), lambda b, i: (b, i, 0))
    st = pl.BlockSpec((None, B_WIDTH, dkw), lambda b, i: (b, 0, 0))
    return pl.pallas_call(
        functools.partial(_gla_body, c=c),
        grid=(bsz, t // c),
        in_specs=[blk(_W_B), blk(_LANES), _const_spec((_LANES, dkw)), _const_spec((1, dkw)),
                  _const_spec((1, B_WIDTH)), st],
        out_specs=[blk(B_WIDTH), st],
        out_shape=(jax.ShapeDtypeStruct((bsz, t, B_WIDTH), F32), jax.ShapeDtypeStruct((bsz, B_WIDTH, dkw), F32)),
        scratch_shapes=[pltpu.VMEM((B_WIDTH, dkw), F32)],
        compiler_params=pltpu.CompilerParams(dimension_semantics=("parallel", "arbitrary"),
                                             vmem_limit_bytes=_VMEM_LIMIT),
        name="gla",
    )(zb, misc, wg_emb, bg, og, s0_t)


def _gdn_body(zc_ref, misc_ref, cw_ref, alog_ref, dtb_ref, og_ref, s0_ref, hist_ref,
              o_ref, s_out_ref, xpad_ref, s_ref, *, c):
    w3 = 3 * C_WIDTH

    @pl.when(pl.program_id(1) == 0)
    def _():
        s_ref[...] = s0_ref[...]
        xpad_ref[0:8, :] = hist_ref[...]

    xpad_ref[8:8 + c, :] = zc_ref[:, 0:w3]
    conv = xpad_ref[5:5 + c, :] * cw_ref[0:1, :]
    for i in range(1, CONV_W):
        conv = conv + xpad_ref[5 + i:5 + i + c, :] * cw_ref[i:i + 1, :]
    xpad_ref[0:8, :] = xpad_ref[c:c + 8, :]
    act = jax.nn.silu(conv)
    q = _group_norm(act[:, 0:C_WIDTH], C_DK, False) * (C_DK ** -0.5)
    k = _group_norm(act[:, C_WIDTH:2 * C_WIDTH], C_DK, False)
    v = act[:, 2 * C_WIDTH:w3]

    misc = misc_ref[...]
    g_full = -jnp.exp(alog_ref[...]) * jax.nn.softplus(misc + dtb_ref[...])
    beta_full = jax.nn.sigmoid(misc)
    gcum = _cumsum_rows(g_full)
    gexp = _expand_heads(gcum, _M_CA, C_HEADS, C_DV)
    bexp = _expand_heads(beta_full, _M_CB, C_HEADS, C_DV)
    g_last = gexp[c - 1:c, :]
    eg = jnp.exp(gexp)

    gcum_t = gcum.T
    diff = jnp.concatenate(
        [jnp.broadcast_to(gcum[:, _M_CA + h:_M_CA + h + 1], (c, c))
         - jnp.broadcast_to(gcum_t[_M_CA + h:_M_CA + h + 1, :], (c, c)) for h in range(C_HEADS)], axis=0)
    ri = lax.broadcasted_iota(jnp.int32, diff.shape, 0) % c
    cj = lax.broadcasted_iota(jnp.int32, diff.shape, 1)
    lmask = jnp.where(ri >= cj, jnp.exp(jnp.minimum(diff, 0.0)), 0.0)

    kb = k * bexp
    m = jnp.where(ri > cj, _mm_nt(_stack_heads(kb, C_HEADS, C_DK), k) * lmask, 0.0)
    qk = _mm_nt(_stack_heads(q, C_HEADS, C_DK), k) * lmask

    m3 = m.reshape(C_HEADS, c, c)
    eye = (lax.broadcasted_iota(jnp.int32, (c, c), 0) == lax.broadcasted_iota(jnp.int32, (c, c), 1)).astype(F32)
    inv = eye[None] - m3
    pw = m3
    for _ in range(int(math.log2(c)) - 1):
        pw = _bmm3(pw, pw)
        inv = inv + _bmm3(inv, pw)
    r = _mm(inv.reshape(C_HEADS * c, c), jnp.concatenate([v * bexp, kb * eg], axis=1))
    u = _diag_select(r[:, 0:C_WIDTH], C_HEADS, C_DV)
    w = _diag_select(r[:, C_WIDTH:], C_HEADS, C_DK)

    s = s_ref[...]
    rs = _mm(jnp.concatenate([w, q * eg], axis=0), s)
    v_new = u - rs[0:c, :]
    o = rs[c:, :] + _diag_select(_mm(qk, v_new), C_HEADS, C_DV)
    upd = _mm_tn(k * jnp.exp(g_last - gexp), v_new)
    bd = (lax.broadcasted_iota(jnp.int32, upd.shape, 0) // C_DK) == (lax.broadcasted_iota(jnp.int32, upd.shape, 1) // C_DV)
    s_new = s * jnp.exp(g_last) + jnp.where(bd, upd, 0.0)
    s_ref[...] = s_new
    s_out_ref[...] = s_new
    o_ref[...] = _group_norm(o, C_DV, True) * og_ref[...] * jax.nn.silu(zc_ref[:, w3:])


def _gdn(zc, misc, conv_w, alog_emb, dtb_emb, og, s0_bd, hist8, c):
    bsz, t, _ = zc.shape
    w3 = 3 * C_WIDTH
    blk = lambda w: pl.BlockSpec((None, c, w), lambda b, i: (b, i, 0))
    st = pl.BlockSpec((None, C_WIDTH, C_WIDTH), lambda b, i: (b, 0, 0))
    return pl.pallas_call(
        functools.partial(_gdn_body, c=c),
        grid=(bsz, t // c),
        in_specs=[blk(_W_C), blk(_LANES), _const_spec((CONV_W, w3)), _const_spec((1, _LANES)),
                  _const_spec((1, _LANES)), _const_spec((1, C_WIDTH)), st,
                  pl.BlockSpec((None, 8, w3), lambda b, i: (b, 0, 0))],
        out_specs=[blk(C_WIDTH), st],
        out_shape=(jax.ShapeDtypeStruct((bsz, t, C_WIDTH), F32), jax.ShapeDtypeStruct((bsz, C_WIDTH, C_WIDTH), F32)),
        scratch_shapes=[pltpu.VMEM((8 + c, w3), F32), pltpu.VMEM((C_WIDTH, C_WIDTH), F32)],
        compiler_params=pltpu.CompilerParams(dimension_semantics=("parallel", "arbitrary"),
                                             vmem_limit_bytes=_VMEM_LIMIT),
        name="gdn",
    )(zc, misc, conv_w, alog_emb, dtb_emb, og, s0_bd, hist8)


def _mlp_body(x_ref, oa_ref, ob_ref, oc_ref, wo_ref, g2_ref, wup_ref, wdn_ref, y_ref, *, ff_tile):
    mixed = jnp.concatenate([oa_ref[...], ob_ref[...], oc_ref[...]], axis=-1)
    y_ref[...] = x_ref[...] + _mm(mixed, wo_ref[...])
    h2 = (_rms(y_ref[...]) * g2_ref[...]).astype(_MXU_DTYPE)
    for j in range(D_FF // ff_tile):
        up = jnp.dot(h2, wup_ref[:, j * ff_tile:(j + 1) * ff_tile], preferred_element_type=F32)
        y_ref[...] += _mm(jnp.square(jnp.maximum(up, 0.0)), wdn_ref[j * ff_tile:(j + 1) * ff_tile, :])


def _mlp(x2d, oa, ob, oc, wo, g2, wup, wdn):
    n = x2d.shape[0]
    tm = min(256, n)
    assert n % tm == 0
    row = lambda w: pl.BlockSpec((tm, w), lambda i: (i, 0))
    return pl.pallas_call(
        functools.partial(_mlp_body, ff_tile=1024),
        grid=(n // tm,),
        in_specs=[row(D_MODEL), row(A_WIDTH), row(B_WIDTH), row(C_WIDTH), _const_spec((D_MODEL, D_MODEL)),
                  _const_spec((1, D_MODEL)), _const_spec((D_MODEL, D_FF)), _const_spec((D_FF, D_MODEL))],
        out_specs=row(D_MODEL),
        out_shape=jax.ShapeDtypeStruct((n, D_MODEL), F32),
        compiler_params=pltpu.CompilerParams(dimension_semantics=("parallel",), vmem_limit_bytes=_VMEM_LIMIT),
        name="mlp",
    )(x2d, oa, ob, oc, wo, g2, wup, wdn)


def _prep_layer_params(ln1_g, w_in, a_qnorm_g, a_knorm_g, gla_w_gate, gla_b_gate, gla_onorm_g,
                       conv_w, gdn_a_log, gdn_dt_bias, gdn_onorm_g, w_o, ln2_g, w_up, w_down):
    cols = [jnp.zeros((D_MODEL, _PAD[name]), w_in.dtype) if name in _PAD
            else w_in[:, _SRC[name]:_SRC[name] + _SIZE[name]] for name in _NEW_ORDER]
    lane_emb = lambda vec, start: jnp.zeros((1, _LANES), F32).at[0, start:start + vec.shape[0]].set(vec.astype(F32))
    return dict(
        g1=ln1_g.reshape(1, D_MODEL).astype(F32),
        w_perm=jnp.concatenate(cols, axis=1).astype(_MXU_DTYPE),
        gq=jnp.tile(a_qnorm_g.astype(F32), A_HEADS).reshape(1, A_WIDTH),
        gk=jnp.tile(a_knorm_g.astype(F32), A_HEADS).reshape(1, A_WIDTH),
        wg_emb=jnp.zeros((_LANES, B_HEADS * B_DK), F32).at[_M_GLR:_M_GLR + GATE_RANK].set(gla_w_gate.astype(F32)),
        bg=gla_b_gate.reshape(1, -1).astype(F32),
        og_b=jnp.tile(gla_onorm_g.astype(F32), B_HEADS).reshape(1, B_WIDTH),
        conv_w=conv_w.astype(F32),
        alog_emb=lane_emb(gdn_a_log, _M_CA),
        dtb_emb=lane_emb(gdn_dt_bias, _M_CA),
        og_c=jnp.tile(gdn_onorm_g.astype(F32), C_HEADS).reshape(1, C_WIDTH),
        wo=w_o.astype(_MXU_DTYPE), g2=ln2_g.reshape(1, D_MODEL).astype(F32),
        wup=w_up.astype(_MXU_DTYPE), wdn=w_down.astype(_MXU_DTYPE),
    )


def _layer(x, cache, p, tbl, tbl_off):
    bsz, t, _ = x.shape
    n = bsz * t
    assert t >= CONV_W - 1
    c = min(CHUNK, t)
    x2d = x.reshape(n, D_MODEL)
    qa, ka, va, qi, misc, zb, zc = _inproj(x2d, p["g1"], p["w_perm"], p["gq"], p["gk"])
    r3 = lambda a: a.reshape(bsz, t, a.shape[-1])
    aki = misc[:, 0:IDX_DIM]

    if cache is None:
        past = 0
        k_all, v_all, ki_all = r3(ka), r3(va), r3(aki)
        s0_gla = jnp.zeros((bsz, B_WIDTH, B_HEADS * B_DK), F32)
        s0_gdn = jnp.zeros((bsz, C_WIDTH, C_WIDTH), F32)
        hist8 = jnp.zeros((bsz, 8, 3 * C_WIDTH), F32)
    else:
        ck, cv, cki, sg, sd, cbuf = cache
        past = ck.shape[1]
        k_all = jnp.concatenate([ck.reshape(bsz, past, A_WIDTH).astype(F32), r3(ka)], axis=1)
        v_all = jnp.concatenate([cv.reshape(bsz, past, A_WIDTH).astype(F32), r3(va)], axis=1)
        ki_all = jnp.concatenate([cki.astype(F32), r3(aki)], axis=1)
        s0_gla = jnp.einsum("bhkv,hg->bhvgk", sg.astype(F32), jnp.eye(B_HEADS, dtype=F32)).reshape(
            bsz, B_WIDTH, B_HEADS * B_DK)
        s0_gdn = jnp.einsum("bhkv,hg->bhkgv", sd.astype(F32), jnp.eye(C_HEADS, dtype=F32)).reshape(
            bsz, C_WIDTH, C_WIDTH)
        hist8 = jnp.pad(cbuf.astype(F32), ((0, 0), (8 - (CONV_W - 1), 0), (0, 0)))

    o_a = _attn(r3(qa), r3(qi), r3(misc[:, _M_WI:_M_WI + IDX_HEADS]), k_all.astype(_MXU_DTYPE),
                v_all.astype(_MXU_DTYPE), ki_all.astype(_MXU_DTYPE), tbl, tbl_off, past)
    o_b, s_gla_t = _gla(r3(zb), r3(misc), p["wg_emb"], p["bg"], p["og_b"], s0_gla, c)
    o_c, s_gdn_bd = _gdn(r3(zc), r3(misc), p["conv_w"], p["alog_emb"], p["dtb_emb"], p["og_c"], s0_gdn, hist8, c)
    y = _mlp(x2d, o_a.reshape(n, A_WIDTH), o_b.reshape(n, B_WIDTH), o_c.reshape(n, C_WIDTH),
             p["wo"], p["g2"], p["wup"], p["wdn"]).reshape(bsz, t, D_MODEL)

    s_gla = jnp.stack([s_gla_t[:, h * B_DV:(h + 1) * B_DV, h * B_DK:(h + 1) * B_DK] for h in range(B_HEADS)], axis=1)
    s_gla = jnp.swapaxes(s_gla, 2, 3)
    s_gdn = jnp.stack([s_gdn_bd[:, h * C_DK:(h + 1) * C_DK, h * C_DV:(h + 1) * C_DV] for h in range(C_HEADS)], axis=1)
    new_conv = r3(zc)[:, t - (CONV_W - 1):, 0:3 * C_WIDTH]
    state = (ka.reshape(bsz, t, A_HEADS, HEAD_DIM), va.reshape(bsz, t, A_HEADS, HEAD_DIM), r3(aki),
             s_gla, s_gdn, new_conv)
    return y, state


def kernel(x_prompt, x_sample, cache_a_k, cache_a_v, cache_a_kidx, state_gla, state_gdn, state_conv, rel_bias, ln1_g, w_in, a_qnorm_g, a_knorm_g, gla_w_gate, gla_b_gate, gla_onorm_g, conv_w, gdn_a_log, gdn_dt_bias, gdn_onorm_g, w_o, ln2_g, w_up, w_down):
    depth = ln1_g.shape[0]
    yp, ys = x_prompt, x_sample
    past = cache_a_k.shape[2]
    tbl_p = _bias_tables(rel_bias, x_prompt.shape[1], x_prompt.shape[1], 0)
    tbl_s = _bias_tables(rel_bias, x_sample.shape[1], past + x_sample.shape[1], past)
    new_p, new_s = [], []
    for l in range(depth):
        p = _prep_layer_params(ln1_g[l], w_in[l], a_qnorm_g[l], a_knorm_g[l], gla_w_gate[l],
                               gla_b_gate[l], gla_onorm_g[l], conv_w[l], gdn_a_log[l], gdn_dt_bias[l],
                               gdn_onorm_g[l], w_o[l], ln2_g[l], w_up[l], w_down[l])
        yp, st_p = _layer(yp, None, p, *tbl_p)
        cache_l = (cache_a_k[l], cache_a_v[l], cache_a_kidx[l], state_gla[l], state_gdn[l], state_conv[l])
        ys, st_s = _layer(ys, cache_l, p, *tbl_s)
        new_p.append(st_p)
        new_s.append(st_s)
    stack = lambda states, i: jnp.stack([s[i] for s in states], axis=0)
    return (yp, ys) + tuple(stack(new_p, i) for i in range(6)) + tuple(stack(new_s, i) for i in range(6))
```

```python
import functools
import math

import numpy as np
import jax
import jax.numpy as jnp
from jax import lax
from jax.experimental import pallas as pl
from jax.experimental.pallas import tpu as pltpu

D_MODEL = 1024
CHUNK = 64
HEAD_DIM = 64
A_HEADS = 4
A_WIDTH = A_HEADS * HEAD_DIM
IDX_HEADS = 8
IDX_DIM = 64
TOPK_MAX = 256
N_BUCKETS = 32
MAX_DISTANCE = 128
B_HEADS = 4
B_DK = 32
B_DV = 64
B_WIDTH = B_HEADS * B_DV
GATE_RANK = 16
GATE_TAU = 16.0
C_HEADS = 8
C_DK = 64
C_DV = 64
C_WIDTH = C_HEADS * C_DV
CONV_W = 4
D_FF = 4 * D_MODEL
EPS = 1e-6

F32 = jnp.float32
BF16 = jnp.bfloat16
_MXU_DTYPE = BF16
_HI = lax.Precision.HIGHEST
_VMEM_LIMIT = 56 * 1024 * 1024
_LANES = 128
_NEG = -1e30
_INT_MIN = -2 ** 31

_SPLIT_NAMES = ("aq", "ak", "av", "aqi", "aki", "awi", "bq", "bk", "bv", "bglr", "bog",
                "cq", "ck", "cv", "ca", "cb", "cog")
_SPLIT_SIZES = (A_WIDTH, A_WIDTH, A_WIDTH, IDX_HEADS * IDX_DIM, IDX_DIM, IDX_HEADS,
                B_HEADS * B_DK, B_HEADS * B_DK, B_WIDTH, GATE_RANK, B_WIDTH,
                C_HEADS * C_DK, C_HEADS * C_DK, C_WIDTH, C_HEADS, C_HEADS, C_WIDTH)
D_IN = sum(_SPLIT_SIZES)
_SRC = dict(zip(_SPLIT_NAMES, np.concatenate([[0], np.cumsum(_SPLIT_SIZES)[:-1]]).tolist()))
_SIZE = dict(zip(_SPLIT_NAMES, _SPLIT_SIZES))

_M_WI, _M_GLR, _M_CA, _M_CB = 64, 72, 88, 96
_NEW_ORDER = ("aq", "ak", "av", "aqi", "aki", "awi", "bglr", "ca", "cb", "pad24",
              "bq", "bk", "bv", "bog", "cq", "ck", "cv", "cog")
_PAD = {"pad24": 24}


def _new_offsets():
    offs, pos = {}, 0
    for name in _NEW_ORDER:
        offs[name] = pos
        pos += _PAD.get(name, 0) or _SIZE[name]
    return offs, pos


_OFF, D_Z = _new_offsets()
_C_QA, _C_KA, _C_VA, _C_QI, _C_MISC = _OFF["aq"], _OFF["ak"], _OFF["av"], _OFF["aqi"], _OFF["aki"]
_C_B, _C_C = _OFF["bq"], _OFF["cq"]
_W_B = 2 * B_HEADS * B_DK + 2 * B_WIDTH
_W_C = 4 * C_WIDTH
assert _C_MISC % _LANES == 0 and _C_B == _C_MISC + _LANES and _C_C == _C_B + _W_B and D_Z == _C_C + _W_C
assert (_OFF["awi"], _OFF["bglr"], _OFF["ca"], _OFF["cb"]) == tuple(_C_MISC + m for m in (_M_WI, _M_GLR, _M_CA, _M_CB))


def _mm(a, b):
    return jnp.dot(a.astype(_MXU_DTYPE), b.astype(_MXU_DTYPE), preferred_element_type=F32)


def _mm_nt(a, b):
    return lax.dot_general(a.astype(_MXU_DTYPE), b.astype(_MXU_DTYPE), (((1,), (1,)), ((), ())),
                           preferred_element_type=F32)


def _mm_tn(a, b):
    return lax.dot_general(a.astype(_MXU_DTYPE), b.astype(_MXU_DTYPE), (((0,), (0,)), ((), ())),
                           preferred_element_type=F32)


def _mmx(a, b):
    return jnp.dot(a, b, preferred_element_type=F32, precision=_HI)


def _bmm3(a, b):
    def split(x):
        hi = x.astype(BF16)
        return hi, (x - hi.astype(F32)).astype(BF16)
    bmm = lambda u, v: jnp.einsum("hij,hjk->hik", u, v, preferred_element_type=F32)
    (ah, al), (bh, bl) = split(a), split(b)
    return bmm(ah, bh) + (bmm(ah, bl) + bmm(al, bh))


def _cumsum_rows(x):
    r = lax.broadcasted_iota(jnp.int32, x.shape, 0)
    s = 1
    while s < x.shape[0]:
        x = x + jnp.where(r >= s, pltpu.roll(x, s, 0), 0.0)
        s *= 2
    return x


def _expand_heads(x, lane0, n_heads, width):
    c = x.shape[0]
    return jnp.concatenate([jnp.broadcast_to(x[:, lane0 + h:lane0 + h + 1], (c, width)) for h in range(n_heads)],
                           axis=-1)


def _rms(x):
    return x * lax.rsqrt(jnp.mean(x * x, axis=-1, keepdims=True) + EPS)


def _group_norm(x, group, mean):
    outs = []
    for g in range(x.shape[-1] // group):
        xs = x[:, g * group:(g + 1) * group]
        ss = jnp.sum(xs * xs, axis=-1, keepdims=True)
        outs.append(xs * lax.rsqrt((ss / group if mean else ss) + EPS))
    return jnp.concatenate(outs, axis=-1)


def _stack_heads(x, n_heads, group):
    grp = lax.broadcasted_iota(jnp.int32, x.shape, 1) // group
    return jnp.concatenate([jnp.where(grp == h, x, 0.0) for h in range(n_heads)], axis=0)


def _diag_select(r, n_heads, group):
    c = r.shape[0] // n_heads
    grp = lax.broadcasted_iota(jnp.int32, (c, r.shape[1]), 1) // group
    out = r[0:c, :]
    for h in range(1, n_heads):
        out = jnp.where(grp == h, r[h * c:(h + 1) * c, :], out)
    return out


def _fold_rows(w, rows=8):
    parts = [w[i * rows:(i + 1) * rows, :] for i in range(w.shape[0] // rows)]
    while len(parts) > 1:
        parts = [parts[i] + parts[i + 1] for i in range(0, len(parts), 2)]
    return parts[0]


def _const_spec(shape):
    zeros = (0,) * len(shape)
    return pl.BlockSpec(shape, lambda *_: zeros)


def _inproj_body(x_ref, g1_ref, w_ref, gq_ref, gk_ref,
                 qa_ref, ka_ref, va_ref, qi_ref, misc_ref, zb_ref, zc_ref):
    h = (_rms(x_ref[...]) * g1_ref[...]).astype(_MXU_DTYPE)

    def proj(c0, width):
        return jnp.dot(h, w_ref[:, c0:c0 + width], preferred_element_type=F32)

    qa_ref[...] = (_group_norm(proj(_C_QA, A_WIDTH), HEAD_DIM, True) * gq_ref[...]
                   * (HEAD_DIM ** -0.5)).astype(qa_ref.dtype)
    ka_ref[...] = _group_norm(proj(_C_KA, A_WIDTH), HEAD_DIM, True) * gk_ref[...]
    va_ref[...] = proj(_C_VA, A_WIDTH)
    qi_ref[...] = (proj(_C_QI, IDX_HEADS * IDX_DIM) * (IDX_DIM ** -0.5)).astype(qi_ref.dtype)
    misc_ref[...] = proj(_C_MISC, _LANES)
    zb_ref[...] = proj(_C_B, _W_B)
    zc_ref[...] = proj(_C_C, _W_C)


def _inproj(x2d, g1, w_perm, gq, gk):
    n = x2d.shape[0]
    tm = min(256, n)
    assert n % tm == 0
    row = lambda w: pl.BlockSpec((tm, w), lambda i: (i, 0))
    out_shapes = (
        jax.ShapeDtypeStruct((n, A_WIDTH), _MXU_DTYPE), jax.ShapeDtypeStruct((n, A_WIDTH), F32),
        jax.ShapeDtypeStruct((n, A_WIDTH), F32), jax.ShapeDtypeStruct((n, IDX_HEADS * IDX_DIM), _MXU_DTYPE),
        jax.ShapeDtypeStruct((n, _LANES), F32), jax.ShapeDtypeStruct((n, _W_B), F32),
        jax.ShapeDtypeStruct((n, _W_C), F32))
    return pl.pallas_call(
        _inproj_body,
        grid=(n // tm,),
        in_specs=[row(D_MODEL), _const_spec((1, D_MODEL)), _const_spec((D_MODEL, D_Z)),
                  _const_spec((1, A_WIDTH)), _const_spec((1, A_WIDTH))],
        out_specs=[row(A_WIDTH), row(A_WIDTH), row(A_WIDTH), row(IDX_HEADS * IDX_DIM), row(_LANES),
                   row(_W_B), row(_W_C)],
        out_shape=out_shapes,
        compiler_params=pltpu.CompilerParams(dimension_semantics=("parallel",), vmem_limit_bytes=_VMEM_LIMIT),
        name="inproj",
    )(x2d, g1, w_perm, gq, gk)


def _attn_body(q_ref, qi_ref, wi_ref, k_ref, vt_ref, ki_ref, tbl_ref, o_ref, key_ref, half_ref, acc_ref, s_ref, p_ref, *,
               tq, kt, n_tiles, past, n_keys, topk, tbl_off, n_tbl, idx_bits):
    qb = pl.program_id(1)
    q0 = past + qb * tq
    n_kt = jnp.minimum(n_tiles, (q0 + tq + kt - 1) // kt)
    krow = lax.broadcasted_iota(jnp.int32, (kt, tq), 0)
    q_chunk = (q0 + lax.broadcasted_iota(jnp.int32, (1, tq), 1)) // CHUNK
    adm_end = jnp.minimum((q_chunk + 1) * CHUNK, n_keys)
    wi = wi_ref[...] * (IDX_HEADS ** -0.5)

    def score_tile(t, carry):
        off = pl.multiple_of(t * kt, kt)
        ki_t = ki_ref[pl.ds(off, kt), :]
        acc = jnp.zeros((kt, tq), F32)
        for h in range(IDX_HEADS):
            acc = acc + wi[h:h + 1, :] * jnp.maximum(_mm_nt(ki_t, qi_ref[h]), 0.0)
        bits = pltpu.bitcast(acc, jnp.int32)
        bits = jnp.where(bits == _INT_MIN, 0, bits)
        key = jnp.where(bits >= 0, bits, bits ^ jnp.int32(0x7FFFFFFF))
        key = jnp.where(krow < adm_end - off, key, _INT_MIN)
        key_ref[t] = key
        half_ref[t] = (key >> 16).astype(jnp.int16)
        return carry

    lax.fori_loop(0, n_kt, score_tile, 0)

    def count(pred_fn):
        def body(t, cnt):
            return cnt + _fold_rows(jnp.where(pred_fn(t, key_ref[t]), 1.0, 0.0))
        cnt = lax.fori_loop(0, n_kt, body, jnp.zeros((8, tq), F32))
        return jnp.sum(cnt, axis=0, keepdims=True)

    def count16(pred_fn):
        one, zero = jnp.int16(1), jnp.int16(0)
        def body(t, cnt):
            return cnt + _fold_rows(jnp.where(pred_fn(half_ref[t]), one, zero), 16)
        cnt = lax.fori_loop(0, n_kt, body, jnp.zeros((16, tq), jnp.int16))
        return jnp.sum(cnt.astype(jnp.int32), axis=0, keepdims=True)

    def kth_largest16(kth):
        def bit(i, v):
            cand = v + lax.shift_left(jnp.int32(1), 15 - i)
            cand16 = cand.astype(jnp.int16)
            return jnp.where(count16(lambda k: k >= cand16) >= kth, cand, v)
        return lax.fori_loop(0, 16, bit, jnp.full((1, tq), -2 ** 15, jnp.int32))

    thr_hi = kth_largest16(topk)
    thr_hi16 = thr_hi.astype(jnp.int16)
    above = count16(lambda k: k > thr_hi16)

    def low_halves(t, carry):
        key = key_ref[t]
        half_ref[t] = jnp.where((key >> 16) == thr_hi, (key & 0xFFFF) - 2 ** 15, -2 ** 15).astype(jnp.int16)
        return carry

    lax.fori_loop(0, n_kt, low_halves, 0)
    thr = thr_hi * 2 ** 16 + (kth_largest16(topk - above) + 2 ** 15)
    thr = jnp.maximum(thr, _INT_MIN + 1)

    cnt_gt = count(lambda t, k: k > thr)
    cnt_ge = count(lambda t, k: k >= thr)
    need = topk - cnt_gt
    has_excess = jnp.max(jnp.where(cnt_ge > topk, 1.0, 0.0)) > 0.0

    def tie_search():
        def bit(i, last):
            cand = last + lax.shift_left(jnp.int32(1), idx_bits - 1 - i)
            below = count(lambda t, k: (k == thr) & ((t * kt + krow) < cand))
            return jnp.where(below < need, cand, last)
        return lax.fori_loop(0, idx_bits, bit, jnp.zeros((1, tq), jnp.int32))

    last = lax.cond(has_excess, tie_search, lambda: jnp.full((1, tq), 2 ** 30, jnp.int32))

    acc_ref[...] = jnp.zeros(acc_ref.shape, F32)

    def attend(t, carry):
        ms, ls = carry
        off = pl.multiple_of(t * kt, kt)
        key = key_ref[t]
        sel = (key > thr) | ((key == thr) & ((off + krow) <= last))
        ti = jnp.clip(t - qb + tbl_off, 0, n_tbl - 1)
        for h in range(A_HEADS):
            s_ref[h] = _mm_nt(k_ref[h, pl.ds(off, kt), :], q_ref[h])
        new_ms, new_ls, alphas = [], [], []
        for h in range(A_HEADS):
            s = jnp.where(sel, s_ref[h] + tbl_ref[ti, h], _NEG)
            m_new = jnp.maximum(ms[h], jnp.max(s, axis=0, keepdims=True))
            alpha = jnp.exp(ms[h] - m_new)
            p = jnp.exp(s - m_new)
            new_ls.append(alpha * ls[h] + jnp.sum(p, axis=0, keepdims=True))
            new_ms.append(m_new)
            alphas.append(alpha)
            p_ref[h] = p.astype(p_ref.dtype)
        for h in range(A_HEADS):
            acc_ref[h] = alphas[h] * acc_ref[h] + jnp.dot(vt_ref[h, t], p_ref[h], preferred_element_type=F32)
        return tuple(new_ms), tuple(new_ls)

    m0 = tuple(jnp.full((1, tq), _NEG, F32) for _ in range(A_HEADS))
    l0 = tuple(jnp.zeros((1, tq), F32) for _ in range(A_HEADS))
    _, ls = lax.fori_loop(0, n_kt, attend, (m0, l0))
    for h in range(A_HEADS):
        o_ref[h] = acc_ref[h] / ls[h]


def _rel_bucket(rel):
    nb = N_BUCKETS // 2
    ret = jnp.where(rel > 0, nb, 0)
    n = jnp.abs(rel)
    max_exact = nb // 2
    nf = jnp.maximum(n, 1).astype(F32)
    large = max_exact + (jnp.log(nf / max_exact) / math.log(MAX_DISTANCE / max_exact)
                         * (nb - max_exact)).astype(jnp.int32)
    large = jnp.minimum(large, nb - 1)
    return ret + jnp.where(n < max_exact, n, large)


def _attn_tiling(t, n_keys):
    kt = 256
    tq = min(kt, t)
    assert t % tq == 0
    return tq, kt, t // tq, -(-n_keys // kt)


def _bias_tables(rel_bias, t, n_keys, past):
    tq, kt, nqb, n_tiles = _attn_tiling(t, n_keys)
    j = jnp.arange(kt)[:, None]
    i = jnp.arange(tq)[None, :]
    if nqb == 1:
        rels = [tile * kt + j - (past + i) for tile in range(n_tiles)]
        off = 0
    else:
        assert past == 0 and tq == kt and kt >= MAX_DISTANCE
        rels = [j - i - 2 * kt, j - i - kt, j - i]
        off = 2
    onehot = jax.nn.one_hot(_rel_bucket(jnp.stack(rels, axis=0)), N_BUCKETS, dtype=F32)
    tbl = jnp.einsum("nktb,bh->nhkt", onehot, rel_bias.astype(F32), precision=_HI)
    return tbl, off


def _attn(q, qi, wi, k_all, v_all, ki_all, tbl, tbl_off, past):
    bsz, t, _ = q.shape
    n_keys = k_all.shape[1]
    topk = min(TOPK_MAX, n_keys // 4)
    tq, kt, nqb, n_tiles = _attn_tiling(t, n_keys)
    s_pad = n_tiles * kt
    assert s_pad >= topk
    if s_pad != n_keys:
        k_all, v_all, ki_all = (jnp.pad(a, ((0, 0), (0, s_pad - n_keys), (0, 0))) for a in (k_all, v_all, ki_all))
    q_hm = q.reshape(bsz, t, A_HEADS, HEAD_DIM).transpose(0, 2, 1, 3)
    qi_hm = qi.reshape(bsz, t, IDX_HEADS, IDX_DIM).transpose(0, 2, 1, 3)
    wi_t = wi.transpose(0, 2, 1)
    k_hm = k_all.reshape(bsz, s_pad, A_HEADS, HEAD_DIM).transpose(0, 2, 1, 3)
    vt_hm = v_all.reshape(bsz, n_tiles, kt, A_HEADS, HEAD_DIM).transpose(0, 3, 1, 4, 2)
    n_tbl = tbl.shape[0]
    body = functools.partial(_attn_body, tq=tq, kt=kt, n_tiles=n_tiles, past=past, n_keys=n_keys, topk=topk,
                             tbl_off=tbl_off, n_tbl=n_tbl, idx_bits=int(s_pad).bit_length())
    o_t = pl.pallas_call(
        body,
        grid=(bsz, nqb),
        in_specs=[pl.BlockSpec((None, A_HEADS, tq, HEAD_DIM), lambda b, i: (b, 0, i, 0)),
                  pl.BlockSpec((None, IDX_HEADS, tq, IDX_DIM), lambda b, i: (b, 0, i, 0)),
                  pl.BlockSpec((None, IDX_HEADS, tq), lambda b, i: (b, 0, i)),
                  pl.BlockSpec((None, A_HEADS, s_pad, HEAD_DIM), lambda b, i: (b, 0, 0, 0)),
                  pl.BlockSpec((None, A_HEADS, n_tiles, HEAD_DIM, kt), lambda b, i: (b, 0, 0, 0, 0)),
                  pl.BlockSpec((None, s_pad, IDX_DIM), lambda b, i: (b, 0, 0)),
                  _const_spec((n_tbl, A_HEADS, kt, tq))],
        out_specs=pl.BlockSpec((None, A_HEADS, HEAD_DIM, tq), lambda b, i: (b, 0, 0, i)),
        out_shape=jax.ShapeDtypeStruct((bsz, A_HEADS, HEAD_DIM, t), F32),
        scratch_shapes=[pltpu.VMEM((n_tiles, kt, tq), jnp.int32), pltpu.VMEM((n_tiles, kt, tq), jnp.int16),
                        pltpu.VMEM((A_HEADS, HEAD_DIM, tq), F32),
                        pltpu.VMEM((A_HEADS, kt, tq), F32), pltpu.VMEM((A_HEADS, kt, tq), _MXU_DTYPE)],
        compiler_params=pltpu.CompilerParams(dimension_semantics=("parallel", "parallel"),
                                             vmem_limit_bytes=_VMEM_LIMIT),
        name="attn",
    )(q_hm, qi_hm, wi_t, k_hm, vt_hm, ki_all, tbl)
    return o_t.transpose(0, 3, 1, 2).reshape(bsz, t, A_WIDTH)


def _gla_body(zb_ref, misc_ref, wg_ref, bg_ref, og_ref, s0_ref, o_ref, s_out_ref, s_ref, *, c):
    dkw = B_HEADS * B_DK

    @pl.when(pl.program_id(1) == 0)
    def _():
        s_ref[...] = s0_ref[...]

    zb = zb_ref[...]
    q = zb[:, 0:dkw] * (B_DK ** -0.5)
    k = zb[:, dkw:2 * dkw]
    v = zb[:, 2 * dkw:2 * dkw + B_WIDTH]
    gate = zb[:, 2 * dkw + B_WIDTH:]
    log_a = jax.nn.log_sigmoid(_mmx(misc_ref[...], wg_ref[...]) + bg_ref[...]) / GATE_TAU
    b = _cumsum_rows(log_a)
    b_last = b[c - 1:c, :]
    b_mid = b[c // 2 - 1:c // 2, :]
    q_bd = _stack_heads(q * jnp.exp(b - b_mid), B_HEADS, B_DK)
    att = _mm_nt(q_bd, k * jnp.exp(b_mid - b))
    ri = lax.broadcasted_iota(jnp.int32, att.shape, 0) % c
    cj = lax.broadcasted_iota(jnp.int32, att.shape, 1)
    att = jnp.where(ri >= cj, att, 0.0)
    s = s_ref[...]
    o = _mm_nt(q * jnp.exp(b), s) + _diag_select(_mm(att, v), B_HEADS, B_DV)
    upd = _mm_tn(v, k * jnp.exp(b_last - b))
    bd = (lax.broadcasted_iota(jnp.int32, upd.shape, 0) // B_DV) == (lax.broadcasted_iota(jnp.int32, upd.shape, 1) // B_DK)
    s_new = s * jnp.exp(b_last) + jnp.where(bd, upd, 0.0)
    s_ref[...] = s_new
    s_out_ref[...] = s_new
    o_ref[...] = _group_norm(o, B_DV, True) * og_ref[...] * jax.nn.silu(gate)


def _gla(zb, misc, wg_emb, bg, og, s0_t, c):
    bsz, t, _ = zb.shape
    dkw = B_HEADS * B_DK
    blk = lambda w: pl.BlockSpec((None, c, w), lambda b, i: (b, i, 0))
    st = pl.BlockSpec((None, B_WIDTH, dkw), lambda b, i: (b, 0, 0))
    return pl.pallas_call(
        functools.partial(_gla_body, c=c),
        grid=(bsz, t // c),
        in_specs=[blk(_W_B), blk(_LANES), _const_spec((_LANES, dkw)), _const_spec((1, dkw)),
                  _const_spec((1, B_WIDTH)), st],
        out_specs=[blk(B_WIDTH), st],
        out_shape=(jax.ShapeDtypeStruct((bsz, t, B_WIDTH), F32), jax.ShapeDtypeStruct((bsz, B_WIDTH, dkw), F32)),
        scratch_shapes=[pltpu.VMEM((B_WIDTH, dkw), F32)],
        compiler_params=pltpu.CompilerParams(dimension_semantics=("parallel", "arbitrary"),
                                             vmem_limit_bytes=_VMEM_LIMIT),
        name="gla",
    )(zb, misc, wg_emb, bg, og, s0_t)


def _gdn_body(zc_ref, misc_ref, cw_ref, alog_ref, dtb_ref, og_ref, s0_ref, hist_ref,
              o_ref, s_out_ref, xpad_ref, s_ref, *, c):
    w3 = 3 * C_WIDTH

    @pl.when(pl.program_id(1) == 0)
    def _():
        s_ref[...] = s0_ref[...]
        xpad_ref[0:8, :] = hist_ref[...]

    xpad_ref[8:8 + c, :] = zc_ref[:, 0:w3]
    conv = xpad_ref[5:5 + c, :] * cw_ref[0:1, :]
    for i in range(1, CONV_W):
        conv = conv + xpad_ref[5 + i:5 + i + c, :] * cw_ref[i:i + 1, :]
    xpad_ref[0:8, :] = xpad_ref[c:c + 8, :]
    act = jax.nn.silu(conv)
    q = _group_norm(act[:, 0:C_WIDTH], C_DK, False) * (C_DK ** -0.5)
    k = _group_norm(act[:, C_WIDTH:2 * C_WIDTH], C_DK, False)
    v = act[:, 2 * C_WIDTH:w3]

    misc = misc_ref[...]
    g_full = -jnp.exp(alog_ref[...]) * jax.nn.softplus(misc + dtb_ref[...])
    beta_full = jax.nn.sigmoid(misc)
    gcum = _cumsum_rows(g_full)
    gexp = _expand_heads(gcum, _M_CA, C_HEADS, C_DV)
    bexp = _expand_heads(beta_full, _M_CB, C_HEADS, C_DV)
    g_last = gexp[c - 1:c, :]
    eg = jnp.exp(gexp)

    gcum_t = gcum.T
    diff = jnp.concatenate(
        [jnp.broadcast_to(gcum[:, _M_CA + h:_M_CA + h + 1], (c, c))
         - jnp.broadcast_to(gcum_t[_M_CA + h:_M_CA + h + 1, :], (c, c)) for h in range(C_HEADS)], axis=0)
    ri = lax.broadcasted_iota(jnp.int32, diff.shape, 0) % c
    cj = lax.broadcasted_iota(jnp.int32, diff.shape, 1)
    lmask = jnp.where(ri >= cj, jnp.exp(jnp.minimum(diff, 0.0)), 0.0)

    kb = k * bexp
    m = jnp.where(ri > cj, _mm_nt(_stack_heads(kb, C_HEADS, C_DK), k) * lmask, 0.0)
    qk = _mm_nt(_stack_heads(q, C_HEADS, C_DK), k) * lmask

    m3 = m.reshape(C_HEADS, c, c)
    eye = (lax.broadcasted_iota(jnp.int32, (c, c), 0) == lax.broadcasted_iota(jnp.int32, (c, c), 1)).astype(F32)
    inv = eye[None] - m3
    pw = m3
    for _ in range(int(math.log2(c)) - 1):
        pw = _bmm3(pw, pw)
        inv = inv + _bmm3(inv, pw)
    r = _mm(inv.reshape(C_HEADS * c, c), jnp.concatenate([v * bexp, kb * eg], axis=1))
    u = _diag_select(r[:, 0:C_WIDTH], C_HEADS, C_DV)
    w = _diag_select(r[:, C_WIDTH:], C_HEADS, C_DK)

    s = s_ref[...]
    rs = _mm(jnp.concatenate([w, q * eg], axis=0), s)
    v_new = u - rs[0:c, :]
    o = rs[c:, :] + _diag_select(_mm(qk, v_new), C_HEADS, C_DV)
    upd = _mm_tn(k * jnp.exp(g_last - gexp), v_new)
    bd = (lax.broadcasted_iota(jnp.int32, upd.shape, 0) // C_DK) == (lax.broadcasted_iota(jnp.int32, upd.shape, 1) // C_DV)
    s_new = s * jnp.exp(g_last) + jnp.where(bd, upd, 0.0)
    s_ref[...] = s_new
    s_out_ref[...] = s_new
    o_ref[...] = _group_norm(o, C_DV, True) * og_ref[...] * jax.nn.silu(zc_ref[:, w3:])


def _gdn(zc, misc, conv_w, alog_emb, dtb_emb, og, s0_bd, hist8, c):
    bsz, t, _ = zc.shape
    w3 = 3 * C_WIDTH
    blk = lambda w: pl.BlockSpec((None, c, w), lambda b, i: (b, i, 0))
    st = pl.BlockSpec((None, C_WIDTH, C_WIDTH), lambda b, i: (b, 0, 0))
    return pl.pallas_call(
        functools.partial(_gdn_body, c=c),
        grid=(bsz, t // c),
        in_specs=[blk(_W_C), blk(_LANES), _const_spec((CONV_W, w3)), _const_spec((1, _LANES)),
                  _const_spec((1, _LANES)), _const_spec((1, C_WIDTH)), st,
                  pl.BlockSpec((None, 8, w3), lambda b, i: (b, 0, 0))],
        out_specs=[blk(C_WIDTH), st],
        out_shape=(jax.ShapeDtypeStruct((bsz, t, C_WIDTH), F32), jax.ShapeDtypeStruct((bsz, C_WIDTH, C_WIDTH), F32)),
        scratch_shapes=[pltpu.VMEM((8 + c, w3), F32), pltpu.VMEM((C_WIDTH, C_WIDTH), F32)],
        compiler_params=pltpu.CompilerParams(dimension_semantics=("parallel", "arbitrary"),
                                             vmem_limit_bytes=_VMEM_LIMIT),
        name="gdn",
    )(zc, misc, conv_w, alog_emb, dtb_emb, og, s0_bd, hist8)


def _mlp_body(x_ref, oa_ref, ob_ref, oc_ref, wo_ref, g2_ref, wup_ref, wdn_ref, y_ref, *, ff_tile):
    mixed = jnp.concatenate([oa_ref[...], ob_ref[...], oc_ref[...]], axis=-1)
    y_ref[...] = x_ref[...] + _mm(mixed, wo_ref[...])
    h2 = (_rms(y_ref[...]) * g2_ref[...]).astype(_MXU_DTYPE)
    for j in range(D_FF // ff_tile):
        up = jnp.dot(h2, wup_ref[:, j * ff_tile:(j + 1) * ff_tile], preferred_element_type=F32)
        y_ref[...] += _mm(jnp.square(jnp.maximum(up, 0.0)), wdn_ref[j * ff_tile:(j + 1) * ff_tile, :])


def _mlp(x2d, oa, ob, oc, wo, g2, wup, wdn):
    n = x2d.shape[0]
    tm = min(256, n)
    assert n % tm == 0
    row = lambda w: pl.BlockSpec((tm, w), lambda i: (i, 0))
    return pl.pallas_call(
        functools.partial(_mlp_body, ff_tile=1024),
        grid=(n // tm,),
        in_specs=[row(D_MODEL), row(A_WIDTH), row(B_WIDTH), row(C_WIDTH), _const_spec((D_MODEL, D_MODEL)),
                  _const_spec((1, D_MODEL)), _const_spec((D_MODEL, D_FF)), _const_spec((D_FF, D_MODEL))],
        out_specs=row(D_MODEL),
        out_shape=jax.ShapeDtypeStruct((n, D_MODEL), F32),
        compiler_params=pltpu.CompilerParams(dimension_semantics=("parallel",), vmem_limit_bytes=_VMEM_LIMIT),
        name="mlp",
    )(x2d, oa, ob, oc, wo, g2, wup, wdn)


def _prep_layer_params(ln1_g, w_in, a_qnorm_g, a_knorm_g, gla_w_gate, gla_b_gate, gla_onorm_g,
                       conv_w, gdn_a_log, gdn_dt_bias, gdn_onorm_g, w_o, ln2_g, w_up, w_down):
    cols = [jnp.zeros((D_MODEL, _PAD[name]), w_in.dtype) if name in _PAD
            else w_in[:, _SRC[name]:_SRC[name] + _SIZE[name]] for name in _NEW_ORDER]
    lane_emb = lambda vec, start: jnp.zeros((1, _LANES), F32).at[0, start:start + vec.shape[0]].set(vec.astype(F32))
    return dict(
        g1=ln1_g.reshape(1, D_MODEL).astype(F32),
        w_perm=jnp.concatenate(cols, axis=1).astype(_MXU_DTYPE),
        gq=jnp.tile(a_qnorm_g.astype(F32), A_HEADS).reshape(1, A_WIDTH),
        gk=jnp.tile(a_knorm_g.astype(F32), A_HEADS).reshape(1, A_WIDTH),
        wg_emb=jnp.zeros((_LANES, B_HEADS * B_DK), F32).at[_M_GLR:_M_GLR + GATE_RANK].set(gla_w_gate.astype(F32)),
        bg=gla_b_gate.reshape(1, -1).astype(F32),
        og_b=jnp.tile(gla_onorm_g.astype(F32), B_HEADS).reshape(1, B_WIDTH),
        conv_w=conv_w.astype(F32),
        alog_emb=lane_emb(gdn_a_log, _M_CA),
        dtb_emb=lane_emb(gdn_dt_bias, _M_CA),
        og_c=jnp.tile(gdn_onorm_g.astype(F32), C_HEADS).reshape(1, C_WIDTH),
        wo=w_o.astype(_MXU_DTYPE), g2=ln2_g.reshape(1, D_MODEL).astype(F32),
        wup=w_up.astype(_MXU_DTYPE), wdn=w_down.astype(_MXU_DTYPE),
    )


def _layer(x, cache, p, tbl, tbl_off):
    bsz, t, _ = x.shape
    n = bsz * t
    assert t >= CONV_W - 1
    c = min(CHUNK, t)
    x2d = x.reshape(n, D_MODEL)
    qa, ka, va, qi, misc, zb, zc = _inproj(x2d, p["g1"], p["w_perm"], p["gq"], p["gk"])
    r3 = lambda a: a.reshape(bsz, t, a.shape[-1])
    aki = misc[:, 0:IDX_DIM]

    if cache is None:
        past = 0
        k_all, v_all, ki_all = r3(ka), r3(va), r3(aki)
        s0_gla = jnp.zeros((bsz, B_WIDTH, B_HEADS * B_DK), F32)
        s0_gdn = jnp.zeros((bsz, C_WIDTH, C_WIDTH), F32)
        hist8 = jnp.zeros((bsz, 8, 3 * C_WIDTH), F32)
    else:
        ck, cv, cki, sg, sd, cbuf = cache
        past = ck.shape[1]
        k_all = jnp.concatenate([ck.reshape(bsz, past, A_WIDTH).astype(F32), r3(ka)], axis=1)
        v_all = jnp.concatenate([cv.reshape(bsz, past, A_WIDTH).astype(F32), r3(va)], axis=1)
        ki_all = jnp.concatenate([cki.astype(F32), r3(aki)], axis=1)
        s0_gla = jnp.einsum("bhkv,hg->bhvgk", sg.astype(F32), jnp.eye(B_HEADS, dtype=F32)).reshape(
            bsz, B_WIDTH, B_HEADS * B_DK)
        s0_gdn = jnp.einsum("bhkv,hg->bhkgv", sd.astype(F32), jnp.eye(C_HEADS, dtype=F32)).reshape(
            bsz, C_WIDTH, C_WIDTH)
        hist8 = jnp.pad(cbuf.astype(F32), ((0, 0), (8 - (CONV_W - 1), 0), (0, 0)))

    o_a = _attn(r3(qa), r3(qi), r3(misc[:, _M_WI:_M_WI + IDX_HEADS]), k_all.astype(_MXU_DTYPE),
                v_all.astype(_MXU_DTYPE), ki_all.astype(_MXU_DTYPE), tbl, tbl_off, past)
    o_b, s_gla_t = _gla(r3(zb), r3(misc), p["wg_emb"], p["bg"], p["og_b"], s0_gla, c)
    o_c, s_gdn_bd = _gdn(r3(zc), r3(misc), p["conv_w"], p["alog_emb"], p["dtb_emb"], p["og_c"], s0_gdn, hist8, c)
    y = _mlp(x2d, o_a.reshape(n, A_WIDTH), o_b.reshape(n, B_WIDTH), o_c.reshape(n, C_WIDTH),
             p["wo"], p["g2"], p["wup"], p["wdn"]).reshape(bsz, t, D_MODEL)

    s_gla = jnp.stack([s_gla_t[:, h * B_DV:(h + 1) * B_DV, h * B_DK:(h + 1) * B_DK] for h in range(B_HEADS)], axis=1)
    s_gla = jnp.swapaxes(s_gla, 2, 3)
    s_gdn = jnp.stack([s_gdn_bd[:, h * C_DK:(h + 1) * C_DK, h * C_DV:(h + 1) * C_DV] for h in range(C_HEADS)], axis=1)
    new_conv = r3(zc)[:, t - (CONV_W - 1):, 0:3 * C_WIDTH]
    state = (ka.reshape(bsz, t, A_HEADS, HEAD_DIM), va.reshape(bsz, t, A_HEADS, HEAD_DIM), r3(aki),
             s_gla, s_gdn, new_conv)
    return y, state


def kernel(x_prompt, x_sample, cache_a_k, cache_a_v, cache_a_kidx, state_gla, state_gdn, state_conv, rel_bias, ln1_g, w_in, a_qnorm_g, a_knorm_g, gla_w_gate, gla_b_gate, gla_onorm_g, conv_w, gdn_a_log, gdn_dt_bias, gdn_onorm_g, w_o, ln2_g, w_up, w_down):
    depth = ln1_g.shape[0]
    yp, ys = x_prompt, x_sample
    past = cache_a_k.shape[2]
    tbl_p = _bias_tables(rel_bias, x_prompt.shape[1], x_prompt.shape[1], 0)
    tbl_s = _bias_tables(rel_bias, x_sample.shape[1], past + x_sample.shape[1], past)
    new_p, new_s = [], []
    for l in range(depth):
        p = _prep_layer_params(ln1_g[l], w_in[l], a_qnorm_g[l], a_knorm_g[l], gla_w_gate[l],
                               gla_b_gate[l], gla_onorm_g[l], conv_w[l], gdn_a_log[l], gdn_dt_bias[l],
                               gdn_onorm_g[l], w_o[l], ln2_g[l], w_up[l], w_down[l])
        yp, st_p = _layer(yp, None, p, *tbl_p)
        cache_l = (cache_a_k[l], cache_a_v[l], cache_a_kidx[l], state_gla[l], state_gdn[l], state_conv[l])
        ys, st_s = _layer(ys, cache_l, p, *tbl_s)
        new_p.append(st_p)
        new_s.append(st_s)
    stack = lambda states, i: jnp.stack([s[i] for s in states], axis=0)
    return (yp, ys) + tuple(stack(new_p, i) for i in range(6)) + tuple(stack(new_s, i) for i in range(6))
```

```python
import functools
import math

import numpy as np
import jax
import jax.numpy as jnp
from jax import lax
from jax.experimental import pallas as pl
from jax.experimental.pallas import tpu as pltpu

D_MODEL = 1024
CHUNK = 64
HEAD_DIM = 64
A_HEADS = 4
A_WIDTH = A_HEADS * HEAD_DIM
IDX_HEADS = 8
IDX_DIM = 64
TOPK_MAX = 256
N_BUCKETS = 32
MAX_DISTANCE = 128
B_HEADS = 4
B_DK = 32
B_DV = 64
B_WIDTH = B_HEADS * B_DV
GATE_RANK = 16
GATE_TAU = 16.0
C_HEADS = 8
C_DK = 64
C_DV = 64
C_WIDTH = C_HEADS * C_DV
CONV_W = 4
D_FF = 4 * D_MODEL
EPS = 1e-6

F32 = jnp.float32
BF16 = jnp.bfloat16
_MXU_DTYPE = BF16
_HI = lax.Precision.HIGHEST
_VMEM_LIMIT = 56 * 1024 * 1024
_LANES = 128
_SEQS_PER_STEP = 2
_NEG = -1e30
_INT_MIN = -2 ** 31

_SPLIT_NAMES = ("aq", "ak", "av", "aqi", "aki", "awi", "bq", "bk", "bv", "bglr", "bog",
                "cq", "ck", "cv", "ca", "cb", "cog")
_SPLIT_SIZES = (A_WIDTH, A_WIDTH, A_WIDTH, IDX_HEADS * IDX_DIM, IDX_DIM, IDX_HEADS,
                B_HEADS * B_DK, B_HEADS * B_DK, B_WIDTH, GATE_RANK, B_WIDTH,
                C_HEADS * C_DK, C_HEADS * C_DK, C_WIDTH, C_HEADS, C_HEADS, C_WIDTH)
D_IN = sum(_SPLIT_SIZES)
_SRC = dict(zip(_SPLIT_NAMES, np.concatenate([[0], np.cumsum(_SPLIT_SIZES)[:-1]]).tolist()))
_SIZE = dict(zip(_SPLIT_NAMES, _SPLIT_SIZES))

_M_WI, _M_GLR, _M_CA, _M_CB = 64, 72, 88, 96
_NEW_ORDER = ("aq", "ak", "av", "aqi", "aki", "awi", "bglr", "ca", "cb", "pad24",
              "bq", "bk", "bv", "bog", "cq", "ck", "cv", "cog")
_PAD = {"pad24": 24}


def _new_offsets():
    offs, pos = {}, 0
    for name in _NEW_ORDER:
        offs[name] = pos
        pos += _PAD.get(name, 0) or _SIZE[name]
    return offs, pos


_OFF, D_Z = _new_offsets()
_C_QA, _C_KA, _C_VA, _C_QI, _C_MISC = _OFF["aq"], _OFF["ak"], _OFF["av"], _OFF["aqi"], _OFF["aki"]
_C_B, _C_C = _OFF["bq"], _OFF["cq"]
_W_B = 2 * B_HEADS * B_DK + 2 * B_WIDTH
_W_C = 4 * C_WIDTH
assert _C_MISC % _LANES == 0 and _C_B == _C_MISC + _LANES and _C_C == _C_B + _W_B and D_Z == _C_C + _W_C
assert (_OFF["awi"], _OFF["bglr"], _OFF["ca"], _OFF["cb"]) == tuple(_C_MISC + m for m in (_M_WI, _M_GLR, _M_CA, _M_CB))


def _mm(a, b):
    return jnp.dot(a.astype(_MXU_DTYPE), b.astype(_MXU_DTYPE), preferred_element_type=F32)


def _mm_nt(a, b):
    return lax.dot_general(a.astype(_MXU_DTYPE), b.astype(_MXU_DTYPE), (((1,), (1,)), ((), ())),
                           preferred_element_type=F32)


def _mmx(a, b):
    return jnp.dot(a, b, preferred_element_type=F32, precision=_HI)


def _bmm3(a, b):
    def split(x):
        hi = x.astype(BF16)
        return hi, (x - hi.astype(F32)).astype(BF16)
    bmm = lambda u, v: jnp.einsum("hij,hjk->hik", u, v, preferred_element_type=F32)
    (ah, al), (bh, bl) = split(a), split(b)
    return bmm(ah, bh) + (bmm(ah, bl) + bmm(al, bh))


def _cumsum_rows(x):
    r = lax.broadcasted_iota(jnp.int32, x.shape, 0)
    s = 1
    while s < x.shape[0]:
        x = x + jnp.where(r >= s, pltpu.roll(x, s, 0), 0.0)
        s *= 2
    return x


def _heads3(x, n_heads, width):
    return jnp.stack([x[:, h * width:(h + 1) * width] for h in range(n_heads)], axis=0)


def _lanes2(x3):
    return jnp.concatenate([x3[h] for h in range(x3.shape[0])], axis=-1)


def _cols3(x, lane0, n_heads, width):
    c = x.shape[0]
    return jnp.stack([jnp.broadcast_to(x[:, lane0 + h:lane0 + h + 1], (c, width)) for h in range(n_heads)], axis=0)


def _bmm(a, b):
    return jnp.einsum("hij,hjk->hik", a.astype(_MXU_DTYPE), b.astype(_MXU_DTYPE), preferred_element_type=F32)


def _bmm_nt(a, b):
    return jnp.einsum("hik,hjk->hij", a.astype(_MXU_DTYPE), b.astype(_MXU_DTYPE), preferred_element_type=F32)


def _bmm_tn(a, b):
    return jnp.einsum("hck,hcv->hkv", a.astype(_MXU_DTYPE), b.astype(_MXU_DTYPE), preferred_element_type=F32)


def _rms(x):
    return x * lax.rsqrt(jnp.mean(x * x, axis=-1, keepdims=True) + EPS)


def _group_norm(x, group, mean):
    outs = []
    for g in range(x.shape[-1] // group):
        xs = x[:, g * group:(g + 1) * group]
        ss = jnp.sum(xs * xs, axis=-1, keepdims=True)
        outs.append(xs * lax.rsqrt((ss / group if mean else ss) + EPS))
    return jnp.concatenate(outs, axis=-1)


def _fold_rows(w, rows=8):
    parts = [w[i * rows:(i + 1) * rows, :] for i in range(w.shape[0] // rows)]
    while len(parts) > 1:
        parts = [parts[i] + parts[i + 1] for i in range(0, len(parts), 2)]
    return parts[0]


def _const_spec(shape):
    zeros = (0,) * len(shape)
    return pl.BlockSpec(shape, lambda *_: zeros)


def _inproj_body(x_ref, g1_ref, w_ref, gq_ref, gk_ref,
                 qa_ref, ka_ref, va_ref, qi_ref, misc_ref, zb_ref, zc_ref):
    h = (_rms(x_ref[...]) * g1_ref[...]).astype(_MXU_DTYPE)

    def proj(c0, width):
        return jnp.dot(h, w_ref[:, c0:c0 + width], preferred_element_type=F32)

    qa_ref[...] = (_group_norm(proj(_C_QA, A_WIDTH), HEAD_DIM, True) * gq_ref[...]
                   * (HEAD_DIM ** -0.5)).astype(qa_ref.dtype)
    ka_ref[...] = _group_norm(proj(_C_KA, A_WIDTH), HEAD_DIM, True) * gk_ref[...]
    va_ref[...] = proj(_C_VA, A_WIDTH)
    qi_ref[...] = (proj(_C_QI, IDX_HEADS * IDX_DIM) * (IDX_DIM ** -0.5)).astype(qi_ref.dtype)
    misc_ref[...] = proj(_C_MISC, _LANES)
    zb_ref[...] = proj(_C_B, _W_B)
    zc_ref[...] = proj(_C_C, _W_C)


def _inproj(x2d, g1, w_perm, gq, gk):
    n = x2d.shape[0]
    tm = min(256, n)
    assert n % tm == 0
    row = lambda w: pl.BlockSpec((tm, w), lambda i: (i, 0))
    out_shapes = (
        jax.ShapeDtypeStruct((n, A_WIDTH), _MXU_DTYPE), jax.ShapeDtypeStruct((n, A_WIDTH), F32),
        jax.ShapeDtypeStruct((n, A_WIDTH), F32), jax.ShapeDtypeStruct((n, IDX_HEADS * IDX_DIM), _MXU_DTYPE),
        jax.ShapeDtypeStruct((n, _LANES), F32), jax.ShapeDtypeStruct((n, _W_B), F32),
        jax.ShapeDtypeStruct((n, _W_C), F32))
    return pl.pallas_call(
        _inproj_body,
        grid=(n // tm,),
        in_specs=[row(D_MODEL), _const_spec((1, D_MODEL)), _const_spec((D_MODEL, D_Z)),
                  _const_spec((1, A_WIDTH)), _const_spec((1, A_WIDTH))],
        out_specs=[row(A_WIDTH), row(A_WIDTH), row(A_WIDTH), row(IDX_HEADS * IDX_DIM), row(_LANES),
                   row(_W_B), row(_W_C)],
        out_shape=out_shapes,
        compiler_params=pltpu.CompilerParams(dimension_semantics=("parallel",), vmem_limit_bytes=_VMEM_LIMIT),
        name="inproj",
    )(x2d, g1, w_perm, gq, gk)


def _attn_body(q_ref, qi_ref, wi_ref, k_ref, vt_ref, ki_ref, tbl_ref, o_ref, key_ref, half_ref, acc_ref, s_ref, p_ref, *,
               tq, kt, n_tiles, past, n_keys, topk, tbl_off, n_tbl, idx_bits):
    qb = pl.program_id(1)
    q0 = past + qb * tq
    n_kt = jnp.minimum(n_tiles, (q0 + tq + kt - 1) // kt)
    krow = lax.broadcasted_iota(jnp.int32, (kt, tq), 0)
    q_chunk = (q0 + lax.broadcasted_iota(jnp.int32, (1, tq), 1)) // CHUNK
    adm_end = jnp.minimum((q_chunk + 1) * CHUNK, n_keys)
    wi = wi_ref[...] * (IDX_HEADS ** -0.5)

    def score_tile(t, carry):
        off = pl.multiple_of(t * kt, kt)
        ki_t = ki_ref[pl.ds(off, kt), :]
        acc = jnp.zeros((kt, tq), F32)
        for h in range(IDX_HEADS):
            acc = acc + wi[h:h + 1, :] * jnp.maximum(_mm_nt(ki_t, qi_ref[h]), 0.0)
        bits = pltpu.bitcast(acc, jnp.int32)
        bits = jnp.where(bits == _INT_MIN, 0, bits)
        key = jnp.where(bits >= 0, bits, bits ^ jnp.int32(0x7FFFFFFF))
        key = jnp.where(krow < adm_end - off, key, _INT_MIN)
        key_ref[t] = key
        half_ref[t] = (key >> 16).astype(jnp.int16)
        return carry

    lax.fori_loop(0, n_kt, score_tile, 0)

    def count(pred_fn):
        def body(t, cnt):
            return cnt + _fold_rows(jnp.where(pred_fn(t, key_ref[t]), 1.0, 0.0))
        cnt = lax.fori_loop(0, n_kt, body, jnp.zeros((8, tq), F32))
        return jnp.sum(cnt, axis=0, keepdims=True)

    def count16(pred_fn):
        one, zero = jnp.int16(1), jnp.int16(0)
        def body(t, cnt):
            return cnt + _fold_rows(jnp.where(pred_fn(half_ref[t]), one, zero), 16)
        cnt = lax.fori_loop(0, n_kt, body, jnp.zeros((16, tq), jnp.int16))
        return jnp.sum(cnt.astype(jnp.int32), axis=0, keepdims=True)

    def kth_largest16(kth):
        def bit(i, v):
            cand = v + lax.shift_left(jnp.int32(1), 15 - i)
            cand16 = cand.astype(jnp.int16)
            return jnp.where(count16(lambda k: k >= cand16) >= kth, cand, v)
        return lax.fori_loop(0, 16, bit, jnp.full((1, tq), -2 ** 15, jnp.int32))

    thr_hi = kth_largest16(topk)
    thr_hi16 = thr_hi.astype(jnp.int16)
    above = count16(lambda k: k > thr_hi16)

    def low_halves(t, carry):
        key = key_ref[t]
        half_ref[t] = jnp.where((key >> 16) == thr_hi, (key & 0xFFFF) - 2 ** 15, -2 ** 15).astype(jnp.int16)
        return carry

    lax.fori_loop(0, n_kt, low_halves, 0)
    thr = thr_hi * 2 ** 16 + (kth_largest16(topk - above) + 2 ** 15)
    thr = jnp.maximum(thr, _INT_MIN + 1)

    cnt_gt = count(lambda t, k: k > thr)
    cnt_ge = count(lambda t, k: k >= thr)
    need = topk - cnt_gt
    has_excess = jnp.max(jnp.where(cnt_ge > topk, 1.0, 0.0)) > 0.0

    def tie_search():
        def bit(i, last):
            cand = last + lax.shift_left(jnp.int32(1), idx_bits - 1 - i)
            below = count(lambda t, k: (k == thr) & ((t * kt + krow) < cand))
            return jnp.where(below < need, cand, last)
        return lax.fori_loop(0, idx_bits, bit, jnp.zeros((1, tq), jnp.int32))

    last = lax.cond(has_excess, tie_search, lambda: jnp.full((1, tq), 2 ** 30, jnp.int32))

    acc_ref[...] = jnp.zeros(acc_ref.shape, F32)

    def attend(t, carry):
        ms, ls = carry
        off = pl.multiple_of(t * kt, kt)
        key = key_ref[t]
        sel = (key > thr) | ((key == thr) & ((off + krow) <= last))
        ti = jnp.clip(t - qb + tbl_off, 0, n_tbl - 1)
        for h in range(A_HEADS):
            s_ref[h] = _mm_nt(k_ref[h, pl.ds(off, kt), :], q_ref[h])
        new_ms, new_ls, alphas = [], [], []
        for h in range(A_HEADS):
            s = jnp.where(sel, s_ref[h] + tbl_ref[ti, h], _NEG)
            m_new = jnp.maximum(ms[h], jnp.max(s, axis=0, keepdims=True))
            alpha = jnp.exp(ms[h] - m_new)
            p = jnp.exp(s - m_new)
            new_ls.append(alpha * ls[h] + jnp.sum(p, axis=0, keepdims=True))
            new_ms.append(m_new)
            alphas.append(alpha)
            p_ref[h] = p.astype(p_ref.dtype)
        for h in range(A_HEADS):
            acc_ref[h] = alphas[h] * acc_ref[h] + jnp.dot(vt_ref[h, t], p_ref[h], preferred_element_type=F32)
        return tuple(new_ms), tuple(new_ls)

    m0 = tuple(jnp.full((1, tq), _NEG, F32) for _ in range(A_HEADS))
    l0 = tuple(jnp.zeros((1, tq), F32) for _ in range(A_HEADS))
    _, ls = lax.fori_loop(0, n_kt, attend, (m0, l0))
    for h in range(A_HEADS):
        o_ref[h] = acc_ref[h] / ls[h]


def _rel_bucket(rel):
    nb = N_BUCKETS // 2
    ret = jnp.where(rel > 0, nb, 0)
    n = jnp.abs(rel)
    max_exact = nb // 2
    nf = jnp.maximum(n, 1).astype(F32)
    large = max_exact + (jnp.log(nf / max_exact) / math.log(MAX_DISTANCE / max_exact)
                         * (nb - max_exact)).astype(jnp.int32)
    large = jnp.minimum(large, nb - 1)
    return ret + jnp.where(n < max_exact, n, large)


def _attn_tiling(t, n_keys):
    kt = 256
    tq = min(kt, t)
    assert t % tq == 0
    return tq, kt, t // tq, -(-n_keys // kt)


def _bias_tables(rel_bias, t, n_keys, past):
    tq, kt, nqb, n_tiles = _attn_tiling(t, n_keys)
    j = jnp.arange(kt)[:, None]
    i = jnp.arange(tq)[None, :]
    if nqb == 1:
        rels = [tile * kt + j - (past + i) for tile in range(n_tiles)]
        off = 0
    else:
        assert past == 0 and tq == kt and kt >= MAX_DISTANCE
        rels = [j - i - 2 * kt, j - i - kt, j - i]
        off = 2
    onehot = jax.nn.one_hot(_rel_bucket(jnp.stack(rels, axis=0)), N_BUCKETS, dtype=F32)
    tbl = jnp.einsum("nktb,bh->nhkt", onehot, rel_bias.astype(F32), precision=_HI)
    return tbl, off


def _attn(q, qi, wi, k_all, v_all, ki_all, tbl, tbl_off, past):
    bsz, t, _ = q.shape
    n_keys = k_all.shape[1]
    topk = min(TOPK_MAX, n_keys // 4)
    tq, kt, nqb, n_tiles = _attn_tiling(t, n_keys)
    s_pad = n_tiles * kt
    assert s_pad >= topk
    if s_pad != n_keys:
        k_all, v_all, ki_all = (jnp.pad(a, ((0, 0), (0, s_pad - n_keys), (0, 0))) for a in (k_all, v_all, ki_all))
    q_hm = q.reshape(bsz, t, A_HEADS, HEAD_DIM).transpose(0, 2, 1, 3)
    qi_hm = qi.reshape(bsz, t, IDX_HEADS, IDX_DIM).transpose(0, 2, 1, 3)
    wi_t = wi.transpose(0, 2, 1)
    k_hm = k_all.reshape(bsz, s_pad, A_HEADS, HEAD_DIM).transpose(0, 2, 1, 3)
    vt_hm = v_all.reshape(bsz, n_tiles, kt, A_HEADS, HEAD_DIM).transpose(0, 3, 1, 4, 2)
    n_tbl = tbl.shape[0]
    body = functools.partial(_attn_body, tq=tq, kt=kt, n_tiles=n_tiles, past=past, n_keys=n_keys, topk=topk,
                             tbl_off=tbl_off, n_tbl=n_tbl, idx_bits=int(s_pad).bit_length())
    o_t = pl.pallas_call(
        body,
        grid=(bsz, nqb),
        in_specs=[pl.BlockSpec((None, A_HEADS, tq, HEAD_DIM), lambda b, i: (b, 0, i, 0)),
                  pl.BlockSpec((None, IDX_HEADS, tq, IDX_DIM), lambda b, i: (b, 0, i, 0)),
                  pl.BlockSpec((None, IDX_HEADS, tq), lambda b, i: (b, 0, i)),
                  pl.BlockSpec((None, A_HEADS, s_pad, HEAD_DIM), lambda b, i: (b, 0, 0, 0)),
                  pl.BlockSpec((None, A_HEADS, n_tiles, HEAD_DIM, kt), lambda b, i: (b, 0, 0, 0, 0)),
                  pl.BlockSpec((None, s_pad, IDX_DIM), lambda b, i: (b, 0, 0)),
                  _const_spec((n_tbl, A_HEADS, kt, tq))],
        out_specs=pl.BlockSpec((None, A_HEADS, HEAD_DIM, tq), lambda b, i: (b, 0, 0, i)),
        out_shape=jax.ShapeDtypeStruct((bsz, A_HEADS, HEAD_DIM, t), F32),
        scratch_shapes=[pltpu.VMEM((n_tiles, kt, tq), jnp.int32), pltpu.VMEM((n_tiles, kt, tq), jnp.int16),
                        pltpu.VMEM((A_HEADS, HEAD_DIM, tq), F32),
                        pltpu.VMEM((A_HEADS, kt, tq), F32), pltpu.VMEM((A_HEADS, kt, tq), _MXU_DTYPE)],
        compiler_params=pltpu.CompilerParams(dimension_semantics=("parallel", "parallel"),
                                             vmem_limit_bytes=_VMEM_LIMIT),
        name="attn",
    )(q_hm, qi_hm, wi_t, k_hm, vt_hm, ki_all, tbl)
    return o_t.transpose(0, 3, 1, 2).reshape(bsz, t, A_WIDTH)


def _gla_chunk(zb, misc, wg, bg, og, st):
    c = zb.shape[0]
    dkw = B_HEADS * B_DK
    q = zb[:, 0:dkw] * (B_DK ** -0.5)
    k = zb[:, dkw:2 * dkw]
    v3 = _heads3(zb[:, 2 * dkw:2 * dkw + B_WIDTH], B_HEADS, B_DV)
    gate = zb[:, 2 * dkw + B_WIDTH:]
    log_a = jax.nn.log_sigmoid(_mmx(misc, wg) + bg) / GATE_TAU
    b = _cumsum_rows(log_a)
    b_last = b[c - 1:c, :]
    b_mid = b[c // 2 - 1:c // 2, :]
    att = _bmm_nt(_heads3(q * jnp.exp(b - b_mid), B_HEADS, B_DK), _heads3(k * jnp.exp(b_mid - b), B_HEADS, B_DK))
    ri = lax.broadcasted_iota(jnp.int32, att.shape, 1)
    cj = lax.broadcasted_iota(jnp.int32, att.shape, 2)
    att = jnp.where(ri >= cj, att, 0.0)
    o3 = _bmm_nt(_heads3(q * jnp.exp(b), B_HEADS, B_DK), st) + _bmm(att, v3)
    upd = _bmm_tn(v3, _heads3(k * jnp.exp(b_last - b), B_HEADS, B_DK))
    st_new = st * _heads3(jnp.exp(b_last), B_HEADS, B_DK) + upd
    o = _lanes2(o3 * lax.rsqrt(jnp.mean(o3 * o3, axis=-1, keepdims=True) + EPS))
    return o * og * jax.nn.silu(gate), st_new


def _gla_body(zb_ref, misc_ref, wg_ref, bg_ref, og_ref, s0_ref, o_ref, s_out_ref, s_ref):
    @pl.when(pl.program_id(1) == 0)
    def _():
        s_ref[...] = s0_ref[...]

    for g in range(zb_ref.shape[0]):
        o, st = _gla_chunk(zb_ref[g], misc_ref[g], wg_ref[...], bg_ref[...], og_ref[...], s_ref[g])
        o_ref[g] = o
        s_ref[g] = st
        s_out_ref[g] = st


def _gla(zb, misc, wg_emb, bg, og, s0_t, c):
    bsz, t, _ = zb.shape
    dkw = B_HEADS * B_DK
    g = _SEQS_PER_STEP
    assert bsz % g == 0
    blk = lambda w: pl.BlockSpec((g, c, w), lambda b, i: (b, i, 0))
    st = pl.BlockSpec((g, B_HEADS, B_DV, B_DK), lambda b, i: (b, 0, 0, 0))
    return pl.pallas_call(
        _gla_body,
        grid=(bsz // g, t // c),
        in_specs=[blk(_W_B), blk(_LANES), _const_spec((_LANES, dkw)), _const_spec((1, dkw)),
                  _const_spec((1, B_WIDTH)), st],
        out_specs=[blk(B_WIDTH), st],
        out_shape=(jax.ShapeDtypeStruct((bsz, t, B_WIDTH), F32),
                   jax.ShapeDtypeStruct((bsz, B_HEADS, B_DV, B_DK), F32)),
        scratch_shapes=[pltpu.VMEM((g, B_HEADS, B_DV, B_DK), F32)],
        compiler_params=pltpu.CompilerParams(dimension_semantics=("parallel", "arbitrary"),
                                             vmem_limit_bytes=_VMEM_LIMIT),
        name="gla",
    )(zb, misc, wg_emb, bg, og, s0_t)


def _gdn_chunk(zc_ref, misc, cw_ref, alog, dtb, og, xpad_ref, s3):
    c = misc.shape[0]
    w3 = 3 * C_WIDTH
    xpad_ref[8:8 + c, :] = zc_ref[:, 0:w3]
    conv = xpad_ref[5:5 + c, :] * cw_ref[0:1, :]
    for i in range(1, CONV_W):
        conv = conv + xpad_ref[5 + i:5 + i + c, :] * cw_ref[i:i + 1, :]
    xpad_ref[0:8, :] = xpad_ref[c:c + 8, :]
    act = jax.nn.silu(conv)
    l2 = lambda x3: x3 * lax.rsqrt(jnp.sum(x3 * x3, axis=-1, keepdims=True) + EPS)
    q3 = l2(_heads3(act[:, 0:C_WIDTH], C_HEADS, C_DK)) * (C_DK ** -0.5)
    k3 = l2(_heads3(act[:, C_WIDTH:2 * C_WIDTH], C_HEADS, C_DK))
    v3 = _heads3(act[:, 2 * C_WIDTH:w3], C_HEADS, C_DV)

    g_full = -jnp.exp(alog) * jax.nn.softplus(misc + dtb)
    beta3 = _cols3(jax.nn.sigmoid(misc), _M_CB, C_HEADS, C_DV)
    gcum = _cumsum_rows(g_full)
    gexp3 = _cols3(gcum, _M_CA, C_HEADS, C_DV)
    g_last3 = gexp3[:, c - 1:c, :]
    eg3 = jnp.exp(gexp3)
    gcum_t = gcum.T
    diff3 = _cols3(gcum, _M_CA, C_HEADS, c) - jnp.stack(
        [jnp.broadcast_to(gcum_t[_M_CA + h:_M_CA + h + 1, :], (c, c)) for h in range(C_HEADS)], axis=0)
    ri = lax.broadcasted_iota(jnp.int32, diff3.shape, 1)
    cj = lax.broadcasted_iota(jnp.int32, diff3.shape, 2)
    lmask = jnp.where(ri >= cj, jnp.exp(jnp.minimum(diff3, 0.0)), 0.0)

    kb3 = k3 * beta3
    m3 = jnp.where(ri > cj, _bmm_nt(kb3, k3) * lmask, 0.0)
    qk3 = _bmm_nt(q3, k3) * lmask
    eye = (lax.broadcasted_iota(jnp.int32, (c, c), 0) == lax.broadcasted_iota(jnp.int32, (c, c), 1)).astype(F32)
    inv = eye[None] - m3
    pw = m3
    for _ in range(int(math.log2(c)) - 1):
        pw = _bmm3(pw, pw)
        inv = inv + _bmm3(inv, pw)
    u3 = _bmm(inv, v3 * beta3)
    w3_ = _bmm(inv, kb3 * eg3)
    v_new = u3 - _bmm(w3_, s3)
    o3 = _bmm(q3 * eg3, s3) + _bmm(qk3, v_new)
    s_new = s3 * jnp.exp(g_last3) + _bmm_tn(k3 * jnp.exp(g_last3 - gexp3), v_new)
    o = _lanes2(o3 * lax.rsqrt(jnp.mean(o3 * o3, axis=-1, keepdims=True) + EPS))
    return o * og * jax.nn.silu(zc_ref[:, w3:]), s_new


def _gdn_body(zc_ref, misc_ref, cw_ref, alog_ref, dtb_ref, og_ref, s0_ref, hist_ref,
              o_ref, s_out_ref, xpad_ref, s_ref):
    @pl.when(pl.program_id(1) == 0)
    def _():
        s_ref[...] = s0_ref[...]
        xpad_ref[:, 0:8, :] = hist_ref[...]

    for g in range(zc_ref.shape[0]):
        o, s_new = _gdn_chunk(zc_ref.at[g], misc_ref[g], cw_ref, alog_ref[...], dtb_ref[...], og_ref[...],
                              xpad_ref.at[g], s_ref[g])
        o_ref[g] = o
        s_ref[g] = s_new
        s_out_ref[g] = s_new


def _gdn(zc, misc, conv_w, alog_emb, dtb_emb, og, s0, hist8, c):
    bsz, t, _ = zc.shape
    w3 = 3 * C_WIDTH
    g = _SEQS_PER_STEP
    assert bsz % g == 0
    blk = lambda w: pl.BlockSpec((g, c, w), lambda b, i: (b, i, 0))
    st = pl.BlockSpec((g, C_HEADS, C_DK, C_DV), lambda b, i: (b, 0, 0, 0))
    return pl.pallas_call(
        _gdn_body,
        grid=(bsz // g, t // c),
        in_specs=[blk(_W_C), blk(_LANES), _const_spec((CONV_W, w3)), _const_spec((1, _LANES)),
                  _const_spec((1, _LANES)), _const_spec((1, C_WIDTH)), st,
                  pl.BlockSpec((g, 8, w3), lambda b, i: (b, 0, 0))],
        out_specs=[blk(C_WIDTH), st],
        out_shape=(jax.ShapeDtypeStruct((bsz, t, C_WIDTH), F32),
                   jax.ShapeDtypeStruct((bsz, C_HEADS, C_DK, C_DV), F32)),
        scratch_shapes=[pltpu.VMEM((g, 8 + c, w3), F32), pltpu.VMEM((g, C_HEADS, C_DK, C_DV), F32)],
        compiler_params=pltpu.CompilerParams(dimension_semantics=("parallel", "arbitrary"),
                                             vmem_limit_bytes=_VMEM_LIMIT),
        name="gdn",
    )(zc, misc, conv_w, alog_emb, dtb_emb, og, s0, hist8)


def _mlp_body(x_ref, oa_ref, ob_ref, oc_ref, wo_ref, g2_ref, wup_ref, wdn_ref, y_ref, *, ff_tile):
    mixed = jnp.concatenate([oa_ref[...], ob_ref[...], oc_ref[...]], axis=-1)
    y_ref[...] = x_ref[...] + _mm(mixed, wo_ref[...])
    h2 = (_rms(y_ref[...]) * g2_ref[...]).astype(_MXU_DTYPE)
    for j in range(D_FF // ff_tile):
        up = jnp.dot(h2, wup_ref[:, j * ff_tile:(j + 1) * ff_tile], preferred_element_type=F32)
        y_ref[...] += _mm(jnp.square(jnp.maximum(up, 0.0)), wdn_ref[j * ff_tile:(j + 1) * ff_tile, :])


def _mlp(x2d, oa, ob, oc, wo, g2, wup, wdn):
    n = x2d.shape[0]
    tm = min(256, n)
    assert n % tm == 0
    row = lambda w: pl.BlockSpec((tm, w), lambda i: (i, 0))
    return pl.pallas_call(
        functools.partial(_mlp_body, ff_tile=1024),
        grid=(n // tm,),
        in_specs=[row(D_MODEL), row(A_WIDTH), row(B_WIDTH), row(C_WIDTH), _const_spec((D_MODEL, D_MODEL)),
                  _const_spec((1, D_MODEL)), _const_spec((D_MODEL, D_FF)), _const_spec((D_FF, D_MODEL))],
        out_specs=row(D_MODEL),
        out_shape=jax.ShapeDtypeStruct((n, D_MODEL), F32),
        compiler_params=pltpu.CompilerParams(dimension_semantics=("parallel",), vmem_limit_bytes=_VMEM_LIMIT),
        name="mlp",
    )(x2d, oa, ob, oc, wo, g2, wup, wdn)


def _prep_layer_params(ln1_g, w_in, a_qnorm_g, a_knorm_g, gla_w_gate, gla_b_gate, gla_onorm_g,
                       conv_w, gdn_a_log, gdn_dt_bias, gdn_onorm_g, w_o, ln2_g, w_up, w_down):
    cols = [jnp.zeros((D_MODEL, _PAD[name]), w_in.dtype) if name in _PAD
            else w_in[:, _SRC[name]:_SRC[name] + _SIZE[name]] for name in _NEW_ORDER]
    lane_emb = lambda vec, start: jnp.zeros((1, _LANES), F32).at[0, start:start + vec.shape[0]].set(vec.astype(F32))
    return dict(
        g1=ln1_g.reshape(1, D_MODEL).astype(F32),
        w_perm=jnp.concatenate(cols, axis=1).astype(_MXU_DTYPE),
        gq=jnp.tile(a_qnorm_g.astype(F32), A_HEADS).reshape(1, A_WIDTH),
        gk=jnp.tile(a_knorm_g.astype(F32), A_HEADS).reshape(1, A_WIDTH),
        wg_emb=jnp.zeros((_LANES, B_HEADS * B_DK), F32).at[_M_GLR:_M_GLR + GATE_RANK].set(gla_w_gate.astype(F32)),
        bg=gla_b_gate.reshape(1, -1).astype(F32),
        og_b=jnp.tile(gla_onorm_g.astype(F32), B_HEADS).reshape(1, B_WIDTH),
        conv_w=conv_w.astype(F32),
        alog_emb=lane_emb(gdn_a_log, _M_CA),
        dtb_emb=lane_emb(gdn_dt_bias, _M_CA),
        og_c=jnp.tile(gdn_onorm_g.astype(F32), C_HEADS).reshape(1, C_WIDTH),
        wo=w_o.astype(_MXU_DTYPE), g2=ln2_g.reshape(1, D_MODEL).astype(F32),
        wup=w_up.astype(_MXU_DTYPE), wdn=w_down.astype(_MXU_DTYPE),
    )


def _layer(x, cache, p, tbl, tbl_off):
    bsz, t, _ = x.shape
    n = bsz * t
    assert t >= CONV_W - 1
    c = min(CHUNK, t)
    x2d = x.reshape(n, D_MODEL)
    qa, ka, va, qi, misc, zb, zc = _inproj(x2d, p["g1"], p["w_perm"], p["gq"], p["gk"])
    r3 = lambda a: a.reshape(bsz, t, a.shape[-1])
    aki = misc[:, 0:IDX_DIM]

    if cache is None:
        past = 0
        k_all, v_all, ki_all = r3(ka), r3(va), r3(aki)
        s0_gla = jnp.zeros((bsz, B_HEADS, B_DV, B_DK), F32)
        s0_gdn = jnp.zeros((bsz, C_HEADS, C_DK, C_DV), F32)
        hist8 = jnp.zeros((bsz, 8, 3 * C_WIDTH), F32)
    else:
        ck, cv, cki, sg, sd, cbuf = cache
        past = ck.shape[1]
        k_all = jnp.concatenate([ck.reshape(bsz, past, A_WIDTH).astype(F32), r3(ka)], axis=1)
        v_all = jnp.concatenate([cv.reshape(bsz, past, A_WIDTH).astype(F32), r3(va)], axis=1)
        ki_all = jnp.concatenate([cki.astype(F32), r3(aki)], axis=1)
        s0_gla = jnp.swapaxes(sg.astype(F32), 2, 3)
        s0_gdn = sd.astype(F32)
        hist8 = jnp.pad(cbuf.astype(F32), ((0, 0), (8 - (CONV_W - 1), 0), (0, 0)))

    o_a = _attn(r3(qa), r3(qi), r3(misc[:, _M_WI:_M_WI + IDX_HEADS]), k_all.astype(_MXU_DTYPE),
                v_all.astype(_MXU_DTYPE), ki_all.astype(_MXU_DTYPE), tbl, tbl_off, past)
    o_b, s_gla_t = _gla(r3(zb), r3(misc), p["wg_emb"], p["bg"], p["og_b"], s0_gla, c)
    o_c, s_gdn = _gdn(r3(zc), r3(misc), p["conv_w"], p["alog_emb"], p["dtb_emb"], p["og_c"], s0_gdn, hist8, c)
    y = _mlp(x2d, o_a.reshape(n, A_WIDTH), o_b.reshape(n, B_WIDTH), o_c.reshape(n, C_WIDTH),
             p["wo"], p["g2"], p["wup"], p["wdn"]).reshape(bsz, t, D_MODEL)

    s_gla = jnp.swapaxes(s_gla_t, 2, 3)
    new_conv = r3(zc)[:, t - (CONV_W - 1):, 0:3 * C_WIDTH]
    state = (ka.reshape(bsz, t, A_HEADS, HEAD_DIM), va.reshape(bsz, t, A_HEADS, HEAD_DIM), r3(aki),
             s_gla, s_gdn, new_conv)
    return y, state


def kernel(x_prompt, x_sample, cache_a_k, cache_a_v, cache_a_kidx, state_gla, state_gdn, state_conv, rel_bias, ln1_g, w_in, a_qnorm_g, a_knorm_g, gla_w_gate, gla_b_gate, gla_onorm_g, conv_w, gdn_a_log, gdn_dt_bias, gdn_onorm_g, w_o, ln2_g, w_up, w_down):
    depth = ln1_g.shape[0]
    yp, ys = x_prompt, x_sample
    past = cache_a_k.shape[2]
    tbl_p = _bias_tables(rel_bias, x_prompt.shape[1], x_prompt.shape[1], 0)
    tbl_s = _bias_tables(rel_bias, x_sample.shape[1], past + x_sample.shape[1], past)
    new_p, new_s = [], []
    for l in range(depth):
        p = _prep_layer_params(ln1_g[l], w_in[l], a_qnorm_g[l], a_knorm_g[l], gla_w_gate[l],
                               gla_b_gate[l], gla_onorm_g[l], conv_w[l], gdn_a_log[l], gdn_dt_bias[l],
                               gdn_onorm_g[l], w_o[l], ln2_g[l], w_up[l], w_down[l])
        yp, st_p = _layer(yp, None, p, *tbl_p)
        cache_l = (cache_a_k[l], cache_a_v[l], cache_a_kidx[l], state_gla[l], state_gdn[l], state_conv[l])
        ys, st_s = _layer(ys, cache_l, p, *tbl_s)
        new_p.append(st_p)
        new_s.append(st_s)
    stack = lambda states, i: jnp.stack([s[i] for s in states], axis=0)
    return (yp, ys) + tuple(stack(new_p, i) for i in range(6)) + tuple(stack(new_s, i) for i in range(6))
```

```python
import functools
import math

import numpy as np
import jax
import jax.numpy as jnp
from jax import lax
from jax.experimental import pallas as pl
from jax.experimental.pallas import tpu as pltpu

D_MODEL = 1024
CHUNK = 64
HEAD_DIM = 64
A_HEADS = 4
A_WIDTH = A_HEADS * HEAD_DIM
IDX_HEADS = 8
IDX_DIM = 64
TOPK_MAX = 256
N_BUCKETS = 32
MAX_DISTANCE = 128
B_HEADS = 4
B_DK = 32
B_DV = 64
B_WIDTH = B_HEADS * B_DV
GATE_RANK = 16
GATE_TAU = 16.0
C_HEADS = 8
C_DK = 64
C_DV = 64
C_WIDTH = C_HEADS * C_DV
CONV_W = 4
D_FF = 4 * D_MODEL
EPS = 1e-6

F32 = jnp.float32
BF16 = jnp.bfloat16
_MXU_DTYPE = BF16
_HI = lax.Precision.HIGHEST
_VMEM_LIMIT = 56 * 1024 * 1024
_LANES = 128
_SEQS_PER_STEP = 2
_NEG = -1e30
_LOG2E = math.log2(math.e)
_INT_MIN = -2 ** 31

_SPLIT_NAMES = ("aq", "ak", "av", "aqi", "aki", "awi", "bq", "bk", "bv", "bglr", "bog",
                "cq", "ck", "cv", "ca", "cb", "cog")
_SPLIT_SIZES = (A_WIDTH, A_WIDTH, A_WIDTH, IDX_HEADS * IDX_DIM, IDX_DIM, IDX_HEADS,
                B_HEADS * B_DK, B_HEADS * B_DK, B_WIDTH, GATE_RANK, B_WIDTH,
                C_HEADS * C_DK, C_HEADS * C_DK, C_WIDTH, C_HEADS, C_HEADS, C_WIDTH)
D_IN = sum(_SPLIT_SIZES)
_SRC = dict(zip(_SPLIT_NAMES, np.concatenate([[0], np.cumsum(_SPLIT_SIZES)[:-1]]).tolist()))
_SIZE = dict(zip(_SPLIT_NAMES, _SPLIT_SIZES))

_M_WI, _M_GLR, _M_CA, _M_CB = 64, 72, 88, 96
_NEW_ORDER = ("aq", "ak", "av", "aqi", "aki", "awi", "bglr", "ca", "cb", "pad24",
              "bq", "bk", "bv", "bog", "cq", "ck", "cv", "cog")
_PAD = {"pad24": 24}


def _new_offsets():
    offs, pos = {}, 0
    for name in _NEW_ORDER:
        offs[name] = pos
        pos += _PAD.get(name, 0) or _SIZE[name]
    return offs, pos


_OFF, D_Z = _new_offsets()
_C_QA, _C_KA, _C_VA, _C_QI, _C_MISC = _OFF["aq"], _OFF["ak"], _OFF["av"], _OFF["aqi"], _OFF["aki"]
_C_B, _C_C = _OFF["bq"], _OFF["cq"]
_W_B = 2 * B_HEADS * B_DK + 2 * B_WIDTH
_W_C = 4 * C_WIDTH
assert _C_MISC % _LANES == 0 and _C_B == _C_MISC + _LANES and _C_C == _C_B + _W_B and D_Z == _C_C + _W_C
assert (_OFF["awi"], _OFF["bglr"], _OFF["ca"], _OFF["cb"]) == tuple(_C_MISC + m for m in (_M_WI, _M_GLR, _M_CA, _M_CB))


def _mm(a, b):
    return jnp.dot(a.astype(_MXU_DTYPE), b.astype(_MXU_DTYPE), preferred_element_type=F32)


def _mm_nt(a, b):
    return lax.dot_general(a.astype(_MXU_DTYPE), b.astype(_MXU_DTYPE), (((1,), (1,)), ((), ())),
                           preferred_element_type=F32)


def _mmx(a, b):
    return jnp.dot(a, b, preferred_element_type=F32, precision=_HI)


def _split_bf16(x):
    hi = x.astype(BF16)
    return hi, (x - hi.astype(F32)).astype(BF16)


def _bmm3(a, b):
    (ah, al), (bh, bl) = a, b
    bmm = lambda u, v: jnp.einsum("hij,hjk->hik", u, v, preferred_element_type=F32)
    n = ah.shape[1]
    both = bmm(jnp.concatenate([ah, al], axis=1), bh)
    return both[:, 0:n, :] + (both[:, n:, :] + bmm(ah, bl))


def _cumsum_rows(x):
    r = lax.broadcasted_iota(jnp.int32, x.shape, 0)
    s = 1
    while s < x.shape[0]:
        x = x + jnp.where(r >= s, pltpu.roll(x, s, 0), 0.0)
        s *= 2
    return x


def _heads3(x, n_heads, width):
    return jnp.stack([x[:, h * width:(h + 1) * width] for h in range(n_heads)], axis=0)


def _lanes2(x3):
    return jnp.concatenate([x3[h] for h in range(x3.shape[0])], axis=-1)


def _cols3(x, lane0, n_heads, width):
    c = x.shape[0]
    return jnp.stack([jnp.broadcast_to(x[:, lane0 + h:lane0 + h + 1], (c, width)) for h in range(n_heads)], axis=0)


def _bmm(a, b):
    return jnp.einsum("hij,hjk->hik", a.astype(_MXU_DTYPE), b.astype(_MXU_DTYPE), preferred_element_type=F32)


def _bmm_nt(a, b):
    return jnp.einsum("hik,hjk->hij", a.astype(_MXU_DTYPE), b.astype(_MXU_DTYPE), preferred_element_type=F32)


def _bmm_tn(a, b):
    return jnp.einsum("hck,hcv->hkv", a.astype(_MXU_DTYPE), b.astype(_MXU_DTYPE), preferred_element_type=F32)


def _rms(x):
    return x * lax.rsqrt(jnp.mean(x * x, axis=-1, keepdims=True) + EPS)


def _group_norm(x, group, mean):
    outs = []
    for g in range(x.shape[-1] // group):
        xs = x[:, g * group:(g + 1) * group]
        ss = jnp.sum(xs * xs, axis=-1, keepdims=True)
        outs.append(xs * lax.rsqrt((ss / group if mean else ss) + EPS))
    return jnp.concatenate(outs, axis=-1)


def _fold_rows(w, rows=8):
    parts = [w[i * rows:(i + 1) * rows, :] for i in range(w.shape[0] // rows)]
    while len(parts) > 1:
        parts = [parts[i] + parts[i + 1] for i in range(0, len(parts), 2)]
    return parts[0]


def _const_spec(shape):
    zeros = (0,) * len(shape)
    return pl.BlockSpec(shape, lambda *_: zeros)


def _inproj_body(x_ref, g1_ref, w_ref, gq_ref, gk_ref,
                 qa_ref, ka_ref, va_ref, qi_ref, misc_ref, zb_ref, zc_ref):
    h = (_rms(x_ref[...]) * g1_ref[...]).astype(_MXU_DTYPE)

    def proj(c0, width):
        return jnp.dot(h, w_ref[:, c0:c0 + width], preferred_element_type=F32)

    qa_ref[...] = (_group_norm(proj(_C_QA, A_WIDTH), HEAD_DIM, True) * gq_ref[...]
                   * (HEAD_DIM ** -0.5 * _LOG2E)).astype(qa_ref.dtype)
    ka_ref[...] = _group_norm(proj(_C_KA, A_WIDTH), HEAD_DIM, True) * gk_ref[...]
    va_ref[...] = proj(_C_VA, A_WIDTH)
    qi_ref[...] = (proj(_C_QI, IDX_HEADS * IDX_DIM) * (IDX_DIM ** -0.5)).astype(qi_ref.dtype)
    misc_ref[...] = proj(_C_MISC, _LANES)
    zb_ref[...] = proj(_C_B, _W_B)
    zc_ref[...] = proj(_C_C, _W_C)


def _inproj(x2d, g1, w_perm, gq, gk):
    n = x2d.shape[0]
    tm = min(256, n)
    assert n % tm == 0
    row = lambda w: pl.BlockSpec((tm, w), lambda i: (i, 0))
    out_shapes = (
        jax.ShapeDtypeStruct((n, A_WIDTH), _MXU_DTYPE), jax.ShapeDtypeStruct((n, A_WIDTH), F32),
        jax.ShapeDtypeStruct((n, A_WIDTH), F32), jax.ShapeDtypeStruct((n, IDX_HEADS * IDX_DIM), _MXU_DTYPE),
        jax.ShapeDtypeStruct((n, _LANES), F32), jax.ShapeDtypeStruct((n, _W_B), F32),
        jax.ShapeDtypeStruct((n, _W_C), F32))
    return pl.pallas_call(
        _inproj_body,
        grid=(n // tm,),
        in_specs=[row(D_MODEL), _const_spec((1, D_MODEL)), _const_spec((D_MODEL, D_Z)),
                  _const_spec((1, A_WIDTH)), _const_spec((1, A_WIDTH))],
        out_specs=[row(A_WIDTH), row(A_WIDTH), row(A_WIDTH), row(IDX_HEADS * IDX_DIM), row(_LANES),
                   row(_W_B), row(_W_C)],
        out_shape=out_shapes,
        compiler_params=pltpu.CompilerParams(dimension_semantics=("parallel",), vmem_limit_bytes=_VMEM_LIMIT),
        name="inproj",
    )(x2d, g1, w_perm, gq, gk)


def _attn_body(q_ref, qi_ref, wi_ref, k_ref, vt_ref, ki_ref, tbl_ref, o_ref, key_ref, half_ref, acc_ref, s_ref, p_ref, *,
               tq, kt, n_tiles, past, n_keys, topk, tbl_off, n_tbl, idx_bits):
    qb = pl.program_id(1)
    q0 = past + qb * tq
    n_kt = jnp.minimum(n_tiles, (q0 + tq + kt - 1) // kt)
    krow = lax.broadcasted_iota(jnp.int32, (kt, tq), 0)
    q_chunk = (q0 + lax.broadcasted_iota(jnp.int32, (1, tq), 1)) // CHUNK
    adm_end = jnp.minimum((q_chunk + 1) * CHUNK, n_keys)
    wi = wi_ref[...] * (IDX_HEADS ** -0.5)

    def score_tile(t, carry):
        off = pl.multiple_of(t * kt, kt)
        ki_t = ki_ref[pl.ds(off, kt), :]
        acc = jnp.zeros((kt, tq), F32)
        for h in range(IDX_HEADS):
            acc = acc + wi[h:h + 1, :] * jnp.maximum(_mm(ki_t, qi_ref[h]), 0.0)
        bits = pltpu.bitcast(acc, jnp.int32)
        bits = jnp.where(bits == _INT_MIN, 0, bits)
        key = jnp.where(bits >= 0, bits, bits ^ jnp.int32(0x7FFFFFFF))
        key = jnp.where(krow < adm_end - off, key, _INT_MIN)
        key_ref[t] = key
        half_ref[t] = (key >> 16).astype(jnp.int16)
        return carry

    lax.fori_loop(0, n_kt, score_tile, 0)

    def count(pred_fn):
        def body(t, cnt):
            return cnt + _fold_rows(jnp.where(pred_fn(t, key_ref[t]), 1.0, 0.0))
        cnt = lax.fori_loop(0, n_kt, body, jnp.zeros((8, tq), F32))
        return jnp.sum(cnt, axis=0, keepdims=True)

    def count16(pred_fn):
        one, zero = jnp.int16(1), jnp.int16(0)
        def body(t, cnt):
            return cnt + _fold_rows(jnp.where(pred_fn(half_ref[t]), one, zero), 16)
        cnt = lax.fori_loop(0, n_kt, body, jnp.zeros((16, tq), jnp.int16))
        return jnp.sum(cnt.astype(jnp.int32), axis=0, keepdims=True)

    def kth_largest16(kth):
        def bit(i, v):
            cand = v + lax.shift_left(jnp.int32(1), 15 - i)
            cand16 = cand.astype(jnp.int16)
            return jnp.where(count16(lambda k: k >= cand16) >= kth, cand, v)
        return lax.fori_loop(0, 16, bit, jnp.full((1, tq), -2 ** 15, jnp.int32))

    thr_hi = kth_largest16(topk)
    thr_hi16 = thr_hi.astype(jnp.int16)
    above = count16(lambda k: k > thr_hi16)

    def low_halves(t, carry):
        key = key_ref[t]
        half_ref[t] = jnp.where((key >> 16) == thr_hi, (key & 0xFFFF) - 2 ** 15, -2 ** 15).astype(jnp.int16)
        return carry

    lax.fori_loop(0, n_kt, low_halves, 0)
    thr = thr_hi * 2 ** 16 + (kth_largest16(topk - above) + 2 ** 15)
    thr = jnp.maximum(thr, _INT_MIN + 1)

    cnt_gt = count(lambda t, k: k > thr)
    cnt_ge = count(lambda t, k: k >= thr)
    need = topk - cnt_gt
    has_excess = jnp.max(jnp.where(cnt_ge > topk, 1.0, 0.0)) > 0.0

    def tie_search():
        def bit(i, last):
            cand = last + lax.shift_left(jnp.int32(1), idx_bits - 1 - i)
            below = count(lambda t, k: (k == thr) & ((t * kt + krow) < cand))
            return jnp.where(below < need, cand, last)
        return lax.fori_loop(0, idx_bits, bit, jnp.zeros((1, tq), jnp.int32))

    last = lax.cond(has_excess, tie_search, lambda: jnp.full((1, tq), 2 ** 30, jnp.int32))

    acc_ref[...] = jnp.zeros(acc_ref.shape, F32)

    def attend(t, carry):
        ms, ls = carry
        off = pl.multiple_of(t * kt, kt)
        key = key_ref[t]
        sel = (key > thr) | ((key == thr) & ((off + krow) <= last))
        ti = jnp.clip(t - qb + tbl_off, 0, n_tbl - 1)
        for h in range(A_HEADS):
            s_ref[h] = _mm(k_ref[h, pl.ds(off, kt), :], q_ref[h])
        new_ms, new_ls, alphas = [], [], []
        for h in range(A_HEADS):
            s = jnp.where(sel, s_ref[h] + tbl_ref[ti, h], _NEG)
            m_new = jnp.maximum(ms[h], jnp.max(s, axis=0, keepdims=True))
            alpha = jnp.exp2(ms[h] - m_new)
            p = jnp.exp2(s - m_new)
            new_ls.append(alpha * ls[h] + jnp.sum(p, axis=0, keepdims=True))
            new_ms.append(m_new)
            alphas.append(alpha)
            p_ref[h] = p.astype(p_ref.dtype)
        for h in range(A_HEADS):
            acc_ref[h] = alphas[h] * acc_ref[h] + jnp.dot(vt_ref[h, t], p_ref[h], preferred_element_type=F32)
        return tuple(new_ms), tuple(new_ls)

    m0 = tuple(jnp.full((1, tq), _NEG, F32) for _ in range(A_HEADS))
    l0 = tuple(jnp.zeros((1, tq), F32) for _ in range(A_HEADS))
    _, ls = lax.fori_loop(0, n_kt, attend, (m0, l0))
    for h in range(A_HEADS):
        o_ref[h] = acc_ref[h] / ls[h]


def _rel_bucket(rel):
    nb = N_BUCKETS // 2
    ret = jnp.where(rel > 0, nb, 0)
    n = jnp.abs(rel)
    max_exact = nb // 2
    nf = jnp.maximum(n, 1).astype(F32)
    large = max_exact + (jnp.log(nf / max_exact) / math.log(MAX_DISTANCE / max_exact)
                         * (nb - max_exact)).astype(jnp.int32)
    large = jnp.minimum(large, nb - 1)
    return ret + jnp.where(n < max_exact, n, large)


def _attn_tiling(t, n_keys):
    kt = 256
    tq = min(kt, t)
    assert t % tq == 0
    return tq, kt, t // tq, -(-n_keys // kt)


def _bias_tables(rel_bias, t, n_keys, past):
    tq, kt, nqb, n_tiles = _attn_tiling(t, n_keys)
    j = jnp.arange(kt)[:, None]
    i = jnp.arange(tq)[None, :]
    if nqb == 1:
        rels = [tile * kt + j - (past + i) for tile in range(n_tiles)]
        off = 0
    else:
        assert past == 0 and tq == kt and kt >= MAX_DISTANCE
        rels = [j - i - 2 * kt, j - i - kt, j - i]
        off = 2
    onehot = jax.nn.one_hot(_rel_bucket(jnp.stack(rels, axis=0)), N_BUCKETS, dtype=F32)
    tbl = jnp.einsum("nktb,bh->nhkt", onehot, rel_bias.astype(F32) * _LOG2E, precision=_HI)
    return tbl, off


def _attn(q, qi, wi, k_all, v_all, ki_all, tbl, tbl_off, past):
    bsz, t, _ = q.shape
    n_keys = k_all.shape[1]
    topk = min(TOPK_MAX, n_keys // 4)
    tq, kt, nqb, n_tiles = _attn_tiling(t, n_keys)
    s_pad = n_tiles * kt
    assert s_pad >= topk
    if s_pad != n_keys:
        k_all, v_all, ki_all = (jnp.pad(a, ((0, 0), (0, s_pad - n_keys), (0, 0))) for a in (k_all, v_all, ki_all))
    q_hm = q.reshape(bsz, t, A_HEADS, HEAD_DIM).transpose(0, 2, 3, 1)
    qi_hm = qi.reshape(bsz, t, IDX_HEADS, IDX_DIM).transpose(0, 2, 3, 1)
    wi_t = wi.transpose(0, 2, 1)
    k_hm = k_all.reshape(bsz, s_pad, A_HEADS, HEAD_DIM).transpose(0, 2, 1, 3)
    vt_hm = v_all.reshape(bsz, n_tiles, kt, A_HEADS, HEAD_DIM).transpose(0, 3, 1, 4, 2)
    n_tbl = tbl.shape[0]
    body = functools.partial(_attn_body, tq=tq, kt=kt, n_tiles=n_tiles, past=past, n_keys=n_keys, topk=topk,
                             tbl_off=tbl_off, n_tbl=n_tbl, idx_bits=int(s_pad).bit_length())
    o_t = pl.pallas_call(
        body,
        grid=(bsz, nqb),
        in_specs=[pl.BlockSpec((None, A_HEADS, HEAD_DIM, tq), lambda b, i: (b, 0, 0, i)),
                  pl.BlockSpec((None, IDX_HEADS, IDX_DIM, tq), lambda b, i: (b, 0, 0, i)),
                  pl.BlockSpec((None, IDX_HEADS, tq), lambda b, i: (b, 0, i)),
                  pl.BlockSpec((None, A_HEADS, s_pad, HEAD_DIM), lambda b, i: (b, 0, 0, 0)),
                  pl.BlockSpec((None, A_HEADS, n_tiles, HEAD_DIM, kt), lambda b, i: (b, 0, 0, 0, 0)),
                  pl.BlockSpec((None, s_pad, IDX_DIM), lambda b, i: (b, 0, 0)),
                  _const_spec((n_tbl, A_HEADS, kt, tq))],
        out_specs=pl.BlockSpec((None, A_HEADS, HEAD_DIM, tq), lambda b, i: (b, 0, 0, i)),
        out_shape=jax.ShapeDtypeStruct((bsz, A_HEADS, HEAD_DIM, t), F32),
        scratch_shapes=[pltpu.VMEM((n_tiles, kt, tq), jnp.int32), pltpu.VMEM((n_tiles, kt, tq), jnp.int16),
                        pltpu.VMEM((A_HEADS, HEAD_DIM, tq), F32),
                        pltpu.VMEM((A_HEADS, kt, tq), F32), pltpu.VMEM((A_HEADS, kt, tq), _MXU_DTYPE)],
        compiler_params=pltpu.CompilerParams(dimension_semantics=("parallel", "parallel"),
                                             vmem_limit_bytes=_VMEM_LIMIT),
        name="attn",
    )(q_hm, qi_hm, wi_t, k_hm, vt_hm, ki_all, tbl)
    return o_t.transpose(0, 3, 1, 2).reshape(bsz, t, A_WIDTH)


def _gla_chunk(zb, misc, wg, bg, og, st):
    c = zb.shape[0]
    dkw = B_HEADS * B_DK
    q = zb[:, 0:dkw] * (B_DK ** -0.5)
    k = zb[:, dkw:2 * dkw]
    v3 = _heads3(zb[:, 2 * dkw:2 * dkw + B_WIDTH], B_HEADS, B_DV)
    gate = zb[:, 2 * dkw + B_WIDTH:]
    log_a = jax.nn.log_sigmoid(_mmx(misc, wg) + bg) / GATE_TAU
    b = _cumsum_rows(log_a)
    b_last = b[c - 1:c, :]
    b_mid = b[c // 2 - 1:c // 2, :]
    att = _bmm_nt(_heads3(q * jnp.exp(b - b_mid), B_HEADS, B_DK), _heads3(k * jnp.exp(b_mid - b), B_HEADS, B_DK))
    ri = lax.broadcasted_iota(jnp.int32, att.shape, 1)
    cj = lax.broadcasted_iota(jnp.int32, att.shape, 2)
    att = jnp.where(ri >= cj, att, 0.0)
    o3 = _bmm_nt(_heads3(q * jnp.exp(b), B_HEADS, B_DK), st) + _bmm(att, v3)
    upd = _bmm_tn(v3, _heads3(k * jnp.exp(b_last - b), B_HEADS, B_DK))
    st_new = st * _heads3(jnp.exp(b_last), B_HEADS, B_DK) + upd
    o = _lanes2(o3 * lax.rsqrt(jnp.mean(o3 * o3, axis=-1, keepdims=True) + EPS))
    return o * og * jax.nn.silu(gate), st_new


def _gla_body(zb_ref, misc_ref, wg_ref, bg_ref, og_ref, s0_ref, o_ref, s_out_ref, s_ref):
    @pl.when(pl.program_id(1) == 0)
    def _():
        s_ref[...] = s0_ref[...]

    for g in range(zb_ref.shape[0]):
        o, st = _gla_chunk(zb_ref[g], misc_ref[g], wg_ref[...], bg_ref[...], og_ref[...], s_ref[g])
        o_ref[g] = o
        s_ref[g] = st
        s_out_ref[g] = st


def _gla(zb, misc, wg_emb, bg, og, s0_t, c):
    bsz, t, _ = zb.shape
    dkw = B_HEADS * B_DK
    g = _SEQS_PER_STEP
    assert bsz % g == 0
    blk = lambda w: pl.BlockSpec((g, c, w), lambda b, i: (b, i, 0))
    st = pl.BlockSpec((g, B_HEADS, B_DV, B_DK), lambda b, i: (b, 0, 0, 0))
    return pl.pallas_call(
        _gla_body,
        grid=(bsz // g, t // c),
        in_specs=[blk(_W_B), blk(_LANES), _const_spec((_LANES, dkw)), _const_spec((1, dkw)),
                  _const_spec((1, B_WIDTH)), st],
        out_specs=[blk(B_WIDTH), st],
        out_shape=(jax.ShapeDtypeStruct((bsz, t, B_WIDTH), F32),
                   jax.ShapeDtypeStruct((bsz, B_HEADS, B_DV, B_DK), F32)),
        scratch_shapes=[pltpu.VMEM((g, B_HEADS, B_DV, B_DK), F32)],
        compiler_params=pltpu.CompilerParams(dimension_semantics=("parallel", "arbitrary"),
                                             vmem_limit_bytes=_VMEM_LIMIT),
        name="gla",
    )(zb, misc, wg_emb, bg, og, s0_t)


def _gdn_chunk(zc_ref, misc, cw_ref, alog, dtb, og, xpad_ref, s3):
    c = misc.shape[0]
    w3 = 3 * C_WIDTH
    xpad_ref[8:8 + c, :] = zc_ref[:, 0:w3]
    conv = xpad_ref[5:5 + c, :] * cw_ref[0:1, :]
    for i in range(1, CONV_W):
        conv = conv + xpad_ref[5 + i:5 + i + c, :] * cw_ref[i:i + 1, :]
    xpad_ref[0:8, :] = xpad_ref[c:c + 8, :]
    act = jax.nn.silu(conv)
    l2 = lambda x3: x3 * lax.rsqrt(jnp.sum(x3 * x3, axis=-1, keepdims=True) + EPS)
    q3 = l2(_heads3(act[:, 0:C_WIDTH], C_HEADS, C_DK)) * (C_DK ** -0.5)
    k3 = l2(_heads3(act[:, C_WIDTH:2 * C_WIDTH], C_HEADS, C_DK))
    v3 = _heads3(act[:, 2 * C_WIDTH:w3], C_HEADS, C_DV)

    g_full = -jnp.exp(alog) * jax.nn.softplus(misc + dtb)
    beta3 = _cols3(jax.nn.sigmoid(misc), _M_CB, C_HEADS, C_DV)
    gcum = _cumsum_rows(g_full)
    gexp3 = _cols3(gcum, _M_CA, C_HEADS, C_DV)
    g_last3 = gexp3[:, c - 1:c, :]
    eg3 = jnp.exp(gexp3)
    gcum_t = gcum.T
    diff3 = _cols3(gcum, _M_CA, C_HEADS, c) - jnp.stack(
        [jnp.broadcast_to(gcum_t[_M_CA + h:_M_CA + h + 1, :], (c, c)) for h in range(C_HEADS)], axis=0)
    ri = lax.broadcasted_iota(jnp.int32, diff3.shape, 1)
    cj = lax.broadcasted_iota(jnp.int32, diff3.shape, 2)
    lmask = jnp.where(ri >= cj, jnp.exp(jnp.minimum(diff3, 0.0)), 0.0)

    kb3 = k3 * beta3
    kq = _bmm_nt(jnp.concatenate([kb3, q3], axis=1), k3)
    m3 = jnp.where(ri > cj, kq[:, 0:c, :] * lmask, 0.0)
    qk3 = kq[:, c:, :] * lmask
    eye = (lax.broadcasted_iota(jnp.int32, (c, c), 0) == lax.broadcasted_iota(jnp.int32, (c, c), 1)).astype(F32)
    inv = eye[None] - m3
    pw = _split_bf16(m3)
    for _ in range(int(math.log2(c)) - 1):
        pw = _split_bf16(_bmm3(pw, pw))
        inv = inv + _bmm3(_split_bf16(inv), pw)
    u3 = _bmm(inv, v3 * beta3)
    w3_ = _bmm(inv, kb3 * eg3)
    wq = _bmm(jnp.concatenate([w3_, q3 * eg3], axis=1), s3)
    v_new = u3 - wq[:, 0:c, :]
    o3 = wq[:, c:, :] + _bmm(qk3, v_new)
    s_new = s3 * jnp.exp(g_last3) + _bmm_tn(k3 * jnp.exp(g_last3 - gexp3), v_new)
    o = _lanes2(o3 * lax.rsqrt(jnp.mean(o3 * o3, axis=-1, keepdims=True) + EPS))
    return o * og * jax.nn.silu(zc_ref[:, w3:]), s_new


def _gdn_body(zc_ref, misc_ref, cw_ref, alog_ref, dtb_ref, og_ref, s0_ref, hist_ref,
              o_ref, s_out_ref, xpad_ref, s_ref):
    @pl.when(pl.program_id(1) == 0)
    def _():
        s_ref[...] = s0_ref[...]
        xpad_ref[:, 0:8, :] = hist_ref[...]

    for g in range(zc_ref.shape[0]):
        o, s_new = _gdn_chunk(zc_ref.at[g], misc_ref[g], cw_ref, alog_ref[...], dtb_ref[...], og_ref[...],
                              xpad_ref.at[g], s_ref[g])
        o_ref[g] = o
        s_ref[g] = s_new
        s_out_ref[g] = s_new


def _gdn(zc, misc, conv_w, alog_emb, dtb_emb, og, s0, hist8, c):
    bsz, t, _ = zc.shape
    w3 = 3 * C_WIDTH
    g = _SEQS_PER_STEP
    assert bsz % g == 0
    blk = lambda w: pl.BlockSpec((g, c, w), lambda b, i: (b, i, 0))
    st = pl.BlockSpec((g, C_HEADS, C_DK, C_DV), lambda b, i: (b, 0, 0, 0))
    return pl.pallas_call(
        _gdn_body,
        grid=(bsz // g, t // c),
        in_specs=[blk(_W_C), blk(_LANES), _const_spec((CONV_W, w3)), _const_spec((1, _LANES)),
                  _const_spec((1, _LANES)), _const_spec((1, C_WIDTH)), st,
                  pl.BlockSpec((g, 8, w3), lambda b, i: (b, 0, 0))],
        out_specs=[blk(C_WIDTH), st],
        out_shape=(jax.ShapeDtypeStruct((bsz, t, C_WIDTH), F32),
                   jax.ShapeDtypeStruct((bsz, C_HEADS, C_DK, C_DV), F32)),
        scratch_shapes=[pltpu.VMEM((g, 8 + c, w3), F32), pltpu.VMEM((g, C_HEADS, C_DK, C_DV), F32)],
        compiler_params=pltpu.CompilerParams(dimension_semantics=("parallel", "arbitrary"),
                                             vmem_limit_bytes=_VMEM_LIMIT),
        name="gdn",
    )(zc, misc, conv_w, alog_emb, dtb_emb, og, s0, hist8)


def _mlp_body(x_ref, oa_ref, ob_ref, oc_ref, wo_ref, g2_ref, wup_ref, wdn_ref, y_ref, *, ff_tile):
    mixed = jnp.concatenate([oa_ref[...], ob_ref[...], oc_ref[...]], axis=-1)
    y_ref[...] = x_ref[...] + _mm(mixed, wo_ref[...])
    h2 = (_rms(y_ref[...]) * g2_ref[...]).astype(_MXU_DTYPE)
    for j in range(D_FF // ff_tile):
        up = jnp.dot(h2, wup_ref[:, j * ff_tile:(j + 1) * ff_tile], preferred_element_type=F32)
        y_ref[...] += _mm(jnp.square(jnp.maximum(up, 0.0)), wdn_ref[j * ff_tile:(j + 1) * ff_tile, :])


def _mlp(x2d, oa, ob, oc, wo, g2, wup, wdn):
    n = x2d.shape[0]
    tm = min(256, n)
    assert n % tm == 0
    row = lambda w: pl.BlockSpec((tm, w), lambda i: (i, 0))
    return pl.pallas_call(
        functools.partial(_mlp_body, ff_tile=1024),
        grid=(n // tm,),
        in_specs=[row(D_MODEL), row(A_WIDTH), row(B_WIDTH), row(C_WIDTH), _const_spec((D_MODEL, D_MODEL)),
                  _const_spec((1, D_MODEL)), _const_spec((D_MODEL, D_FF)), _const_spec((D_FF, D_MODEL))],
        out_specs=row(D_MODEL),
        out_shape=jax.ShapeDtypeStruct((n, D_MODEL), F32),
        compiler_params=pltpu.CompilerParams(dimension_semantics=("parallel",), vmem_limit_bytes=_VMEM_LIMIT),
        name="mlp",
    )(x2d, oa, ob, oc, wo, g2, wup, wdn)


def _prep_layer_params(ln1_g, w_in, a_qnorm_g, a_knorm_g, gla_w_gate, gla_b_gate, gla_onorm_g,
                       conv_w, gdn_a_log, gdn_dt_bias, gdn_onorm_g, w_o, ln2_g, w_up, w_down):
    cols = [jnp.zeros((D_MODEL, _PAD[name]), w_in.dtype) if name in _PAD
            else w_in[:, _SRC[name]:_SRC[name] + _SIZE[name]] for name in _NEW_ORDER]
    lane_emb = lambda vec, start: jnp.zeros((1, _LANES), F32).at[0, start:start + vec.shape[0]].set(vec.astype(F32))
    return dict(
        g1=ln1_g.reshape(1, D_MODEL).astype(F32),
        w_perm=jnp.concatenate(cols, axis=1).astype(_MXU_DTYPE),
        gq=jnp.tile(a_qnorm_g.astype(F32), A_HEADS).reshape(1, A_WIDTH),
        gk=jnp.tile(a_knorm_g.astype(F32), A_HEADS).reshape(1, A_WIDTH),
        wg_emb=jnp.zeros((_LANES, B_HEADS * B_DK), F32).at[_M_GLR:_M_GLR + GATE_RANK].set(gla_w_gate.astype(F32)),
        bg=gla_b_gate.reshape(1, -1).astype(F32),
        og_b=jnp.tile(gla_onorm_g.astype(F32), B_HEADS).reshape(1, B_WIDTH),
        conv_w=conv_w.astype(F32),
        alog_emb=lane_emb(gdn_a_log, _M_CA),
        dtb_emb=lane_emb(gdn_dt_bias, _M_CA),
        og_c=jnp.tile(gdn_onorm_g.astype(F32), C_HEADS).reshape(1, C_WIDTH),
        wo=w_o.astype(_MXU_DTYPE), g2=ln2_g.reshape(1, D_MODEL).astype(F32),
        wup=w_up.astype(_MXU_DTYPE), wdn=w_down.astype(_MXU_DTYPE),
    )


def _layer(x, cache, p, tbl, tbl_off):
    bsz, t, _ = x.shape
    n = bsz * t
    assert t >= CONV_W - 1
    c = min(CHUNK, t)
    x2d = x.reshape(n, D_MODEL)
    qa, ka, va, qi, misc, zb, zc = _inproj(x2d, p["g1"], p["w_perm"], p["gq"], p["gk"])
    r3 = lambda a: a.reshape(bsz, t, a.shape[-1])
    aki = misc[:, 0:IDX_DIM]

    if cache is None:
        past = 0
        k_all, v_all, ki_all = r3(ka), r3(va), r3(aki)
        s0_gla = jnp.zeros((bsz, B_HEADS, B_DV, B_DK), F32)
        s0_gdn = jnp.zeros((bsz, C_HEADS, C_DK, C_DV), F32)
        hist8 = jnp.zeros((bsz, 8, 3 * C_WIDTH), F32)
    else:
        ck, cv, cki, sg, sd, cbuf = cache
        past = ck.shape[1]
        k_all = jnp.concatenate([ck.reshape(bsz, past, A_WIDTH).astype(F32), r3(ka)], axis=1)
        v_all = jnp.concatenate([cv.reshape(bsz, past, A_WIDTH).astype(F32), r3(va)], axis=1)
        ki_all = jnp.concatenate([cki.astype(F32), r3(aki)], axis=1)
        s0_gla = jnp.swapaxes(sg.astype(F32), 2, 3)
        s0_gdn = sd.astype(F32)
        hist8 = jnp.pad(cbuf.astype(F32), ((0, 0), (8 - (CONV_W - 1), 0), (0, 0)))

    o_a = _attn(r3(qa), r3(qi), r3(misc[:, _M_WI:_M_WI + IDX_HEADS]), k_all.astype(_MXU_DTYPE),
                v_all.astype(_MXU_DTYPE), ki_all.astype(_MXU_DTYPE), tbl, tbl_off, past)
    o_b, s_gla_t = _gla(r3(zb), r3(misc), p["wg_emb"], p["bg"], p["og_b"], s0_gla, c)
    o_c, s_gdn = _gdn(r3(zc), r3(misc), p["conv_w"], p["alog_emb"], p["dtb_emb"], p["og_c"], s0_gdn, hist8, c)
    y = _mlp(x2d, o_a.reshape(n, A_WIDTH), o_b.reshape(n, B_WIDTH), o_c.reshape(n, C_WIDTH),
             p["wo"], p["g2"], p["wup"], p["wdn"]).reshape(bsz, t, D_MODEL)

    s_gla = jnp.swapaxes(s_gla_t, 2, 3)
    new_conv = r3(zc)[:, t - (CONV_W - 1):, 0:3 * C_WIDTH]
    state = (ka.reshape(bsz, t, A_HEADS, HEAD_DIM), va.reshape(bsz, t, A_HEADS, HEAD_DIM), r3(aki),
             s_gla, s_gdn, new_conv)
    return y, state


def kernel(x_prompt, x_sample, cache_a_k, cache_a_v, cache_a_kidx, state_gla, state_gdn, state_conv, rel_bias, ln1_g, w_in, a_qnorm_g, a_knorm_g, gla_w_gate, gla_b_gate, gla_onorm_g, conv_w, gdn_a_log, gdn_dt_bias, gdn_onorm_g, w_o, ln2_g, w_up, w_down):
    depth = ln1_g.shape[0]
    yp, ys = x_prompt, x_sample
    past = cache_a_k.shape[2]
    tbl_p = _bias_tables(rel_bias, x_prompt.shape[1], x_prompt.shape[1], 0)
    tbl_s = _bias_tables(rel_bias, x_sample.shape[1], past + x_sample.shape[1], past)
    new_p, new_s = [], []
    for l in range(depth):
        p = _prep_layer_params(ln1_g[l], w_in[l], a_qnorm_g[l], a_knorm_g[l], gla_w_gate[l],
                               gla_b_gate[l], gla_onorm_g[l], conv_w[l], gdn_a_log[l], gdn_dt_bias[l],
                               gdn_onorm_g[l], w_o[l], ln2_g[l], w_up[l], w_down[l])
        yp, st_p = _layer(yp, None, p, *tbl_p)
        cache_l = (cache_a_k[l], cache_a_v[l], cache_a_kidx[l], state_gla[l], state_gdn[l], state_conv[l])
        ys, st_s = _layer(ys, cache_l, p, *tbl_s)
        new_p.append(st_p)
        new_s.append(st_s)
    stack = lambda states, i: jnp.stack([s[i] for s in states], axis=0)
    return (yp, ys) + tuple(stack(new_p, i) for i in range(6)) + tuple(stack(new_s, i) for i in range(6))
```

```python
import functools
import math

import numpy as np
import jax
import jax.numpy as jnp
from jax import lax
from jax.experimental import pallas as pl
from jax.experimental.pallas import tpu as pltpu

D_MODEL = 1024
CHUNK = 64
HEAD_DIM = 64
A_HEADS = 4
A_WIDTH = A_HEADS * HEAD_DIM
IDX_HEADS = 8
IDX_DIM = 64
TOPK_MAX = 256
N_BUCKETS = 32
MAX_DISTANCE = 128
B_HEADS = 4
B_DK = 32
B_DV = 64
B_WIDTH = B_HEADS * B_DV
GATE_RANK = 16
GATE_TAU = 16.0
C_HEADS = 8
C_DK = 64
C_DV = 64
C_WIDTH = C_HEADS * C_DV
CONV_W = 4
D_FF = 4 * D_MODEL
EPS = 1e-6

F32 = jnp.float32
BF16 = jnp.bfloat16
_MXU_DTYPE = BF16
_HI = lax.Precision.HIGHEST
_VMEM_LIMIT = 56 * 1024 * 1024
_LANES = 128
_ROW_TILE = 512
_SEQS_PER_STEP = 2
_NEG = -1e30
_LOG2E = math.log2(math.e)
_INT_MIN = -2 ** 31

_SPLIT_NAMES = ("aq", "ak", "av", "aqi", "aki", "awi", "bq", "bk", "bv", "bglr", "bog",
                "cq", "ck", "cv", "ca", "cb", "cog")
_SPLIT_SIZES = (A_WIDTH, A_WIDTH, A_WIDTH, IDX_HEADS * IDX_DIM, IDX_DIM, IDX_HEADS,
                B_HEADS * B_DK, B_HEADS * B_DK, B_WIDTH, GATE_RANK, B_WIDTH,
                C_HEADS * C_DK, C_HEADS * C_DK, C_WIDTH, C_HEADS, C_HEADS, C_WIDTH)
D_IN = sum(_SPLIT_SIZES)
_SRC = dict(zip(_SPLIT_NAMES, np.concatenate([[0], np.cumsum(_SPLIT_SIZES)[:-1]]).tolist()))
_SIZE = dict(zip(_SPLIT_NAMES, _SPLIT_SIZES))

_M_WI, _M_GLR, _M_CA, _M_CB = 64, 72, 88, 96
_NEW_ORDER = ("aq", "ak", "av", "aqi", "aki", "awi", "bglr", "ca", "cb", "pad24",
              "bq", "bk", "bv", "bog", "cq", "ck", "cv", "cog")
_PAD = {"pad24": 24}


def _new_offsets():
    offs, pos = {}, 0
    for name in _NEW_ORDER:
        offs[name] = pos
        pos += _PAD.get(name, 0) or _SIZE[name]
    return offs, pos


_OFF, D_Z = _new_offsets()
_C_QA, _C_KA, _C_VA, _C_QI, _C_MISC = _OFF["aq"], _OFF["ak"], _OFF["av"], _OFF["aqi"], _OFF["aki"]
_C_B, _C_C = _OFF["bq"], _OFF["cq"]
_W_B = 2 * B_HEADS * B_DK + 2 * B_WIDTH
_W_C = 4 * C_WIDTH
assert _C_MISC % _LANES == 0 and _C_B == _C_MISC + _LANES and _C_C == _C_B + _W_B and D_Z == _C_C + _W_C
assert (_OFF["awi"], _OFF["bglr"], _OFF["ca"], _OFF["cb"]) == tuple(_C_MISC + m for m in (_M_WI, _M_GLR, _M_CA, _M_CB))


def _mm(a, b):
    return jnp.dot(a.astype(_MXU_DTYPE), b.astype(_MXU_DTYPE), preferred_element_type=F32)


def _mm_nt(a, b):
    return lax.dot_general(a.astype(_MXU_DTYPE), b.astype(_MXU_DTYPE), (((1,), (1,)), ((), ())),
                           preferred_element_type=F32)


def _mmx(a, b):
    return jnp.dot(a, b, preferred_element_type=F32, precision=_HI)


def _split_bf16(x):
    hi = x.astype(BF16)
    return hi, (x - hi.astype(F32)).astype(BF16)


def _bmm3(a, b):
    (ah, al), (bh, bl) = a, b
    bmm = lambda u, v: jnp.einsum("hij,hjk->hik", u, v, preferred_element_type=F32)
    n = ah.shape[1]
    both = bmm(jnp.concatenate([ah, al], axis=1), bh)
    return both[:, 0:n, :] + (both[:, n:, :] + bmm(ah, bl))


def _cumsum_rows(x):
    r = lax.broadcasted_iota(jnp.int32, x.shape, 0)
    s = 1
    while s < x.shape[0]:
        x = x + jnp.where(r >= s, pltpu.roll(x, s, 0), 0.0)
        s *= 2
    return x


def _heads3(x, n_heads, width):
    return jnp.stack([x[:, h * width:(h + 1) * width] for h in range(n_heads)], axis=0)


def _lanes2(x3):
    return jnp.concatenate([x3[h] for h in range(x3.shape[0])], axis=-1)


def _cols3(x, lane0, n_heads, width):
    c = x.shape[0]
    return jnp.stack([jnp.broadcast_to(x[:, lane0 + h:lane0 + h + 1], (c, width)) for h in range(n_heads)], axis=0)


def _bmm(a, b):
    return jnp.einsum("hij,hjk->hik", a.astype(_MXU_DTYPE), b.astype(_MXU_DTYPE), preferred_element_type=F32)


def _bmm_nt(a, b):
    return jnp.einsum("hik,hjk->hij", a.astype(_MXU_DTYPE), b.astype(_MXU_DTYPE), preferred_element_type=F32)


def _bmm_tn(a, b):
    return jnp.einsum("hck,hcv->hkv", a.astype(_MXU_DTYPE), b.astype(_MXU_DTYPE), preferred_element_type=F32)


def _rms(x):
    return x * lax.rsqrt(jnp.mean(x * x, axis=-1, keepdims=True) + EPS)


def _group_norm(x, group, mean):
    outs = []
    for g in range(x.shape[-1] // group):
        xs = x[:, g * group:(g + 1) * group]
        ss = jnp.sum(xs * xs, axis=-1, keepdims=True)
        outs.append(xs * lax.rsqrt((ss / group if mean else ss) + EPS))
    return jnp.concatenate(outs, axis=-1)


def _fold_rows(w, rows=8):
    parts = [w[i * rows:(i + 1) * rows, :] for i in range(w.shape[0] // rows)]
    while len(parts) > 1:
        parts = [parts[i] + parts[i + 1] for i in range(0, len(parts), 2)]
    return parts[0]


def _const_spec(shape, single_buffer=False):
    zeros = (0,) * len(shape)
    if single_buffer:
        return pl.BlockSpec(shape, lambda *_: zeros, pipeline_mode=pl.Buffered(1))
    return pl.BlockSpec(shape, lambda *_: zeros)


def _inproj_body(x_ref, g1_ref, w_ref, gq_ref, gk_ref,
                 qa_ref, ka_ref, va_ref, qi_ref, misc_ref, zb_ref, zc_ref):
    h = (_rms(x_ref[...]) * g1_ref[...]).astype(_MXU_DTYPE)

    def proj(c0, width):
        return jnp.dot(h, w_ref[:, c0:c0 + width], preferred_element_type=F32)

    qa_ref[...] = (_group_norm(proj(_C_QA, A_WIDTH), HEAD_DIM, True) * gq_ref[...]
                   * (HEAD_DIM ** -0.5 * _LOG2E)).astype(qa_ref.dtype)
    ka_ref[...] = _group_norm(proj(_C_KA, A_WIDTH), HEAD_DIM, True) * gk_ref[...]
    va_ref[...] = proj(_C_VA, A_WIDTH)
    qi_ref[...] = (proj(_C_QI, IDX_HEADS * IDX_DIM) * (IDX_DIM ** -0.5)).astype(qi_ref.dtype)
    misc_ref[...] = proj(_C_MISC, _LANES)
    zb_ref[...] = proj(_C_B, _W_B)
    zc_ref[...] = proj(_C_C, _W_C)


def _inproj(x2d, g1, w_perm, gq, gk):
    n = x2d.shape[0]
    tm = min(_ROW_TILE, n)
    assert n % tm == 0
    row = lambda w: pl.BlockSpec((tm, w), lambda i: (i, 0))
    out_shapes = (
        jax.ShapeDtypeStruct((n, A_WIDTH), _MXU_DTYPE), jax.ShapeDtypeStruct((n, A_WIDTH), F32),
        jax.ShapeDtypeStruct((n, A_WIDTH), F32), jax.ShapeDtypeStruct((n, IDX_HEADS * IDX_DIM), _MXU_DTYPE),
        jax.ShapeDtypeStruct((n, _LANES), F32), jax.ShapeDtypeStruct((n, _W_B), F32),
        jax.ShapeDtypeStruct((n, _W_C), F32))
    return pl.pallas_call(
        _inproj_body,
        grid=(n // tm,),
        in_specs=[row(D_MODEL), _const_spec((1, D_MODEL)), _const_spec((D_MODEL, D_Z), True),
                  _const_spec((1, A_WIDTH)), _const_spec((1, A_WIDTH))],
        out_specs=[row(A_WIDTH), row(A_WIDTH), row(A_WIDTH), row(IDX_HEADS * IDX_DIM), row(_LANES),
                   row(_W_B), row(_W_C)],
        out_shape=out_shapes,
        compiler_params=pltpu.CompilerParams(dimension_semantics=("parallel",), vmem_limit_bytes=_VMEM_LIMIT),
        name="inproj",
    )(x2d, g1, w_perm, gq, gk)


def _attn_body(q_ref, qi_ref, wi_ref, k_ref, vt_ref, ki_ref, tbl_ref, o_ref, key_ref, half_ref, acc_ref, s_ref, s2_ref, p_ref, p2_ref, *,
               tq, kt, n_tiles, past, n_keys, topk, tbl_off, n_tbl, idx_bits):
    qb = pl.program_id(1)
    q0 = past + qb * tq
    n_kt = jnp.minimum(n_tiles, (q0 + tq + kt - 1) // kt)
    krow = lax.broadcasted_iota(jnp.int32, (kt, tq), 0)
    q_chunk = (q0 + lax.broadcasted_iota(jnp.int32, (1, tq), 1)) // CHUNK
    adm_end = jnp.minimum((q_chunk + 1) * CHUNK, n_keys)
    wi = wi_ref[...] * (IDX_HEADS ** -0.5)

    def score_tile(t, carry):
        off = pl.multiple_of(t * kt, kt)
        ki_t = ki_ref[pl.ds(off, kt), :]
        acc = jnp.zeros((kt, tq), F32)
        for h in range(IDX_HEADS):
            acc = acc + wi[h:h + 1, :] * jnp.maximum(_mm(ki_t, qi_ref[h]), 0.0)
        bits = pltpu.bitcast(acc, jnp.int32)
        bits = jnp.where(bits == _INT_MIN, 0, bits)
        key = jnp.where(bits >= 0, bits, bits ^ jnp.int32(0x7FFFFFFF))
        key = jnp.where(krow < adm_end - off, key, _INT_MIN)
        key_ref[t] = key
        half_ref[t] = (key >> 16).astype(jnp.int16)
        return carry

    lax.fori_loop(0, n_kt, score_tile, 0)

    def count(pred_fn):
        def body(t, cnt):
            return cnt + _fold_rows(jnp.where(pred_fn(t, key_ref[t]), 1.0, 0.0))
        cnt = lax.fori_loop(0, n_kt, body, jnp.zeros((8, tq), F32))
        return jnp.sum(cnt, axis=0, keepdims=True)

    def count16(pred_fn):
        one, zero = jnp.int16(1), jnp.int16(0)
        def body(t, cnt):
            return cnt + _fold_rows(jnp.where(pred_fn(half_ref[t]), one, zero), 16)
        cnt = lax.fori_loop(0, n_kt, body, jnp.zeros((16, tq), jnp.int16))
        return jnp.sum(cnt.astype(jnp.int32), axis=0, keepdims=True)

    def kth_largest16(kth):
        def bit(i, v):
            cand = v + lax.shift_left(jnp.int32(1), 15 - i)
            cand16 = cand.astype(jnp.int16)
            return jnp.where(count16(lambda k: k >= cand16) >= kth, cand, v)
        return lax.fori_loop(0, 16, bit, jnp.full((1, tq), -2 ** 15, jnp.int32))

    thr_hi = kth_largest16(topk)
    thr_hi16 = thr_hi.astype(jnp.int16)
    above = count16(lambda k: k > thr_hi16)

    def low_halves(t, carry):
        key = key_ref[t]
        half_ref[t] = jnp.where((key >> 16) == thr_hi, (key & 0xFFFF) - 2 ** 15, -2 ** 15).astype(jnp.int16)
        return carry

    lax.fori_loop(0, n_kt, low_halves, 0)
    thr = thr_hi * 2 ** 16 + (kth_largest16(topk - above) + 2 ** 15)
    thr = jnp.maximum(thr, _INT_MIN + 1)

    cnt_gt = count(lambda t, k: k > thr)
    cnt_ge = count(lambda t, k: k >= thr)
    need = topk - cnt_gt
    has_excess = jnp.max(jnp.where(cnt_ge > topk, 1.0, 0.0)) > 0.0

    def tie_search():
        def bit(i, last):
            cand = last + lax.shift_left(jnp.int32(1), idx_bits - 1 - i)
            below = count(lambda t, k: (k == thr) & ((t * kt + krow) < cand))
            return jnp.where(below < need, cand, last)
        return lax.fori_loop(0, idx_bits, bit, jnp.zeros((1, tq), jnp.int32))

    last = lax.cond(has_excess, tie_search, lambda: jnp.full((1, tq), 2 ** 30, jnp.int32))

    acc_ref[...] = jnp.zeros(acc_ref.shape, F32)

    def logits(t, dst_ref):
        off = pl.multiple_of(t * kt, kt)
        for h in range(A_HEADS):
            dst_ref[h] = _mm(k_ref[h, pl.ds(off, kt), :], q_ref[h])

    def softmax_pv(t, src_ref, pr_ref, carry):
        ms, ls = carry
        off = pl.multiple_of(t * kt, kt)
        key = key_ref[t]
        sel = (key > thr) | ((key == thr) & ((off + krow) <= last))
        ti = jnp.clip(t - qb + tbl_off, 0, n_tbl - 1)
        new_ms, new_ls, alphas = [], [], []
        for h in range(A_HEADS):
            s = jnp.where(sel, src_ref[h] + tbl_ref[ti, h], _NEG)
            m_new = jnp.maximum(ms[h], jnp.max(s, axis=0, keepdims=True))
            alpha = jnp.exp2(ms[h] - m_new)
            p = jnp.exp2(s - m_new)
            new_ls.append(alpha * ls[h] + jnp.sum(p, axis=0, keepdims=True))
            new_ms.append(m_new)
            alphas.append(alpha)
            pr_ref[h] = p.astype(pr_ref.dtype)
        for h in range(A_HEADS):
            acc_ref[h] = alphas[h] * acc_ref[h] + jnp.dot(vt_ref[h, t], pr_ref[h], preferred_element_type=F32)
        return tuple(new_ms), tuple(new_ls)

    @pl.when(n_kt % 2 == 1)
    def _():
        key_ref[n_kt] = jnp.full((kt, tq), _INT_MIN, jnp.int32)

    n_pair = (n_kt + 1) // 2
    logits(0, s_ref)

    def attend_pair(j, carry):
        t0 = 2 * j
        logits(t0 + 1, s2_ref)
        carry = softmax_pv(t0, s_ref, p_ref, carry)
        logits(jnp.minimum(t0 + 2, 2 * n_pair - 2), s_ref)
        return softmax_pv(t0 + 1, s2_ref, p2_ref, carry)

    m0 = tuple(jnp.full((1, tq), _NEG, F32) for _ in range(A_HEADS))
    l0 = tuple(jnp.zeros((1, tq), F32) for _ in range(A_HEADS))
    _, ls = lax.fori_loop(0, n_pair, attend_pair, (m0, l0))
    for h in range(A_HEADS):
        o_ref[h] = acc_ref[h] / ls[h]


def _rel_bucket(rel):
    nb = N_BUCKETS // 2
    ret = jnp.where(rel > 0, nb, 0)
    n = jnp.abs(rel)
    max_exact = nb // 2
    nf = jnp.maximum(n, 1).astype(F32)
    large = max_exact + (jnp.log(nf / max_exact) / math.log(MAX_DISTANCE / max_exact)
                         * (nb - max_exact)).astype(jnp.int32)
    large = jnp.minimum(large, nb - 1)
    return ret + jnp.where(n < max_exact, n, large)


def _attn_tiling(t, n_keys):
    kt = 256
    tq = min(kt, t)
    assert t % tq == 0
    return tq, kt, t // tq, 2 * -(-n_keys // (2 * kt))


def _bias_tables(rel_bias, t, n_keys, past):
    tq, kt, nqb, n_tiles = _attn_tiling(t, n_keys)
    j = jnp.arange(kt)[:, None]
    i = jnp.arange(tq)[None, :]
    if nqb == 1:
        rels = [tile * kt + j - (past + i) for tile in range(n_tiles)]
        off = 0
    else:
        assert past == 0 and tq == kt and kt >= MAX_DISTANCE
        rels = [j - i - 2 * kt, j - i - kt, j - i]
        off = 2
    onehot = jax.nn.one_hot(_rel_bucket(jnp.stack(rels, axis=0)), N_BUCKETS, dtype=F32)
    tbl = jnp.einsum("nktb,bh->nhkt", onehot, rel_bias.astype(F32) * _LOG2E, precision=_HI)
    return tbl, off


def _attn(q, qi, wi, k_all, v_all, ki_all, tbl, tbl_off, past):
    bsz, t, _ = q.shape
    n_keys = k_all.shape[1]
    topk = min(TOPK_MAX, n_keys // 4)
    tq, kt, nqb, n_tiles = _attn_tiling(t, n_keys)
    s_pad = n_tiles * kt
    assert s_pad >= topk
    if s_pad != n_keys:
        k_all, v_all, ki_all = (jnp.pad(a, ((0, 0), (0, s_pad - n_keys), (0, 0))) for a in (k_all, v_all, ki_all))
    q_hm = q.reshape(bsz, t, A_HEADS, HEAD_DIM).transpose(0, 2, 3, 1)
    qi_hm = qi.reshape(bsz, t, IDX_HEADS, IDX_DIM).transpose(0, 2, 3, 1)
    wi_t = wi.transpose(0, 2, 1)
    k_hm = k_all.reshape(bsz, s_pad, A_HEADS, HEAD_DIM).transpose(0, 2, 1, 3)
    vt_hm = v_all.reshape(bsz, n_tiles, kt, A_HEADS, HEAD_DIM).transpose(0, 3, 1, 4, 2)
    n_tbl = tbl.shape[0]
    body = functools.partial(_attn_body, tq=tq, kt=kt, n_tiles=n_tiles, past=past, n_keys=n_keys, topk=topk,
                             tbl_off=tbl_off, n_tbl=n_tbl, idx_bits=int(s_pad).bit_length())
    o_t = pl.pallas_call(
        body,
        grid=(bsz, nqb),
        in_specs=[pl.BlockSpec((None, A_HEADS, HEAD_DIM, tq), lambda b, i: (b, 0, 0, i)),
                  pl.BlockSpec((None, IDX_HEADS, IDX_DIM, tq), lambda b, i: (b, 0, 0, i)),
                  pl.BlockSpec((None, IDX_HEADS, tq), lambda b, i: (b, 0, i)),
                  pl.BlockSpec((None, A_HEADS, s_pad, HEAD_DIM), lambda b, i: (b, 0, 0, 0)),
                  pl.BlockSpec((None, A_HEADS, n_tiles, HEAD_DIM, kt), lambda b, i: (b, 0, 0, 0, 0)),
                  pl.BlockSpec((None, s_pad, IDX_DIM), lambda b, i: (b, 0, 0)),
                  _const_spec((n_tbl, A_HEADS, kt, tq), True)],
        out_specs=pl.BlockSpec((None, A_HEADS, HEAD_DIM, tq), lambda b, i: (b, 0, 0, i)),
        out_shape=jax.ShapeDtypeStruct((bsz, A_HEADS, HEAD_DIM, t), F32),
        scratch_shapes=[pltpu.VMEM((n_tiles, kt, tq), jnp.int32), pltpu.VMEM((n_tiles, kt, tq), jnp.int16),
                        pltpu.VMEM((A_HEADS, HEAD_DIM, tq), F32),
                        pltpu.VMEM((A_HEADS, kt, tq), F32), pltpu.VMEM((A_HEADS, kt, tq), F32),
                        pltpu.VMEM((A_HEADS, kt, tq), _MXU_DTYPE), pltpu.VMEM((A_HEADS, kt, tq), _MXU_DTYPE)],
        compiler_params=pltpu.CompilerParams(dimension_semantics=("parallel", "parallel"),
                                             vmem_limit_bytes=_VMEM_LIMIT),
        name="attn",
    )(q_hm, qi_hm, wi_t, k_hm, vt_hm, ki_all, tbl)
    return o_t.transpose(0, 3, 1, 2).reshape(bsz, t, A_WIDTH)


def _gla_chunk(zb, misc, wg, bg, og, st):
    c = zb.shape[0]
    dkw = B_HEADS * B_DK
    q = zb[:, 0:dkw] * (B_DK ** -0.5)
    k = zb[:, dkw:2 * dkw]
    v3 = _heads3(zb[:, 2 * dkw:2 * dkw + B_WIDTH], B_HEADS, B_DV)
    gate = zb[:, 2 * dkw + B_WIDTH:]
    log_a = jax.nn.log_sigmoid(_mmx(misc, wg) + bg) / GATE_TAU
    b = _cumsum_rows(log_a)
    b_last = b[c - 1:c, :]
    b_mid = b[c // 2 - 1:c // 2, :]
    att = _bmm_nt(_heads3(q * jnp.exp(b - b_mid), B_HEADS, B_DK), _heads3(k * jnp.exp(b_mid - b), B_HEADS, B_DK))
    ri = lax.broadcasted_iota(jnp.int32, att.shape, 1)
    cj = lax.broadcasted_iota(jnp.int32, att.shape, 2)
    att = jnp.where(ri >= cj, att, 0.0)
    o3 = _bmm_nt(_heads3(q * jnp.exp(b), B_HEADS, B_DK), st) + _bmm(att, v3)
    upd = _bmm_tn(v3, _heads3(k * jnp.exp(b_last - b), B_HEADS, B_DK))
    st_new = st * _heads3(jnp.exp(b_last), B_HEADS, B_DK) + upd
    o = _lanes2(o3 * lax.rsqrt(jnp.mean(o3 * o3, axis=-1, keepdims=True) + EPS))
    return o * og * jax.nn.silu(gate), st_new


def _gla_body(zb_ref, misc_ref, wg_ref, bg_ref, og_ref, s0_ref, o_ref, s_out_ref, s_ref):
    @pl.when(pl.program_id(1) == 0)
    def _():
        s_ref[...] = s0_ref[...]

    for g in range(zb_ref.shape[0]):
        o, st = _gla_chunk(zb_ref[g], misc_ref[g], wg_ref[...], bg_ref[...], og_ref[...], s_ref[g])
        o_ref[g] = o
        s_ref[g] = st
        s_out_ref[g] = st


def _gla(zb, misc, wg_emb, bg, og, s0_t, c):
    bsz, t, _ = zb.shape
    dkw = B_HEADS * B_DK
    g = _SEQS_PER_STEP
    assert bsz % g == 0
    blk = lambda w: pl.BlockSpec((g, c, w), lambda b, i: (b, i, 0))
    st = pl.BlockSpec((g, B_HEADS, B_DV, B_DK), lambda b, i: (b, 0, 0, 0))
    return pl.pallas_call(
        _gla_body,
        grid=(bsz // g, t // c),
        in_specs=[blk(_W_B), blk(_LANES), _const_spec((_LANES, dkw)), _const_spec((1, dkw)),
                  _const_spec((1, B_WIDTH)), st],
        out_specs=[blk(B_WIDTH), st],
        out_shape=(jax.ShapeDtypeStruct((bsz, t, B_WIDTH), F32),
                   jax.ShapeDtypeStruct((bsz, B_HEADS, B_DV, B_DK), F32)),
        scratch_shapes=[pltpu.VMEM((g, B_HEADS, B_DV, B_DK), F32)],
        compiler_params=pltpu.CompilerParams(dimension_semantics=("parallel", "arbitrary"),
                                             vmem_limit_bytes=_VMEM_LIMIT),
        name="gla",
    )(zb, misc, wg_emb, bg, og, s0_t)


def _gdn_chunk(zc_ref, misc, cw_ref, alog, dtb, og, xpad_ref, s3):
    c = misc.shape[0]
    w3 = 3 * C_WIDTH
    xpad_ref[8:8 + c, :] = zc_ref[:, 0:w3]
    conv = xpad_ref[5:5 + c, :] * cw_ref[0:1, :]
    for i in range(1, CONV_W):
        conv = conv + xpad_ref[5 + i:5 + i + c, :] * cw_ref[i:i + 1, :]
    xpad_ref[0:8, :] = xpad_ref[c:c + 8, :]
    act = jax.nn.silu(conv)
    l2 = lambda x3: x3 * lax.rsqrt(jnp.sum(x3 * x3, axis=-1, keepdims=True) + EPS)
    q3 = l2(_heads3(act[:, 0:C_WIDTH], C_HEADS, C_DK)) * (C_DK ** -0.5)
    k3 = l2(_heads3(act[:, C_WIDTH:2 * C_WIDTH], C_HEADS, C_DK))
    v3 = _heads3(act[:, 2 * C_WIDTH:w3], C_HEADS, C_DV)

    g_full = -jnp.exp(alog) * jax.nn.softplus(misc + dtb)
    beta3 = _cols3(jax.nn.sigmoid(misc), _M_CB, C_HEADS, C_DV)
    gcum = _cumsum_rows(g_full)
    gexp3 = _cols3(gcum, _M_CA, C_HEADS, C_DV)
    g_last3 = gexp3[:, c - 1:c, :]
    eg3 = jnp.exp(gexp3)
    gcum_t = gcum.T
    diff3 = _cols3(gcum, _M_CA, C_HEADS, c) - jnp.stack(
        [jnp.broadcast_to(gcum_t[_M_CA + h:_M_CA + h + 1, :], (c, c)) for h in range(C_HEADS)], axis=0)
    ri = lax.broadcasted_iota(jnp.int32, diff3.shape, 1)
    cj = lax.broadcasted_iota(jnp.int32, diff3.shape, 2)
    lmask = jnp.where(ri >= cj, jnp.exp(jnp.minimum(diff3, 0.0)), 0.0)

    kb3 = k3 * beta3
    kq = _bmm_nt(jnp.concatenate([kb3, q3], axis=1), k3)
    m3 = jnp.where(ri > cj, kq[:, 0:c, :] * lmask, 0.0)
    qk3 = kq[:, c:, :] * lmask
    eye = (lax.broadcasted_iota(jnp.int32, (c, c), 0) == lax.broadcasted_iota(jnp.int32, (c, c), 1)).astype(F32)
    inv = eye[None] - m3
    pw = _split_bf16(m3)
    for _ in range(int(math.log2(c)) - 1):
        pw = _split_bf16(_bmm3(pw, pw))
        inv = inv + _bmm3(_split_bf16(inv), pw)
    u3 = _bmm(inv, v3 * beta3)
    w3_ = _bmm(inv, kb3 * eg3)
    wq = _bmm(jnp.concatenate([w3_, q3 * eg3], axis=1), s3)
    v_new = u3 - wq[:, 0:c, :]
    o3 = wq[:, c:, :] + _bmm(qk3, v_new)
    s_new = s3 * jnp.exp(g_last3) + _bmm_tn(k3 * jnp.exp(g_last3 - gexp3), v_new)
    o = _lanes2(o3 * lax.rsqrt(jnp.mean(o3 * o3, axis=-1, keepdims=True) + EPS))
    return o * og * jax.nn.silu(zc_ref[:, w3:]), s_new


def _gdn_body(zc_ref, misc_ref, cw_ref, alog_ref, dtb_ref, og_ref, s0_ref, hist_ref,
              o_ref, s_out_ref, xpad_ref, s_ref):
    @pl.when(pl.program_id(1) == 0)
    def _():
        s_ref[...] = s0_ref[...]
        xpad_ref[:, 0:8, :] = hist_ref[...]

    for g in range(zc_ref.shape[0]):
        o, s_new = _gdn_chunk(zc_ref.at[g], misc_ref[g], cw_ref, alog_ref[...], dtb_ref[...], og_ref[...],
                              xpad_ref.at[g], s_ref[g])
        o_ref[g] = o
        s_ref[g] = s_new
        s_out_ref[g] = s_new


def _gdn(zc, misc, conv_w, alog_emb, dtb_emb, og, s0, hist8, c):
    bsz, t, _ = zc.shape
    w3 = 3 * C_WIDTH
    g = _SEQS_PER_STEP
    assert bsz % g == 0
    blk = lambda w: pl.BlockSpec((g, c, w), lambda b, i: (b, i, 0))
    st = pl.BlockSpec((g, C_HEADS, C_DK, C_DV), lambda b, i: (b, 0, 0, 0))
    return pl.pallas_call(
        _gdn_body,
        grid=(bsz // g, t // c),
        in_specs=[blk(_W_C), blk(_LANES), _const_spec((CONV_W, w3)), _const_spec((1, _LANES)),
                  _const_spec((1, _LANES)), _const_spec((1, C_WIDTH)), st,
                  pl.BlockSpec((g, 8, w3), lambda b, i: (b, 0, 0))],
        out_specs=[blk(C_WIDTH), st],
        out_shape=(jax.ShapeDtypeStruct((bsz, t, C_WIDTH), F32),
                   jax.ShapeDtypeStruct((bsz, C_HEADS, C_DK, C_DV), F32)),
        scratch_shapes=[pltpu.VMEM((g, 8 + c, w3), F32), pltpu.VMEM((g, C_HEADS, C_DK, C_DV), F32)],
        compiler_params=pltpu.CompilerParams(dimension_semantics=("parallel", "arbitrary"),
                                             vmem_limit_bytes=_VMEM_LIMIT),
        name="gdn",
    )(zc, misc, conv_w, alog_emb, dtb_emb, og, s0, hist8)


def _mlp_body(x_ref, oa_ref, ob_ref, oc_ref, wo_ref, g2_ref, wup_ref, wdn_ref, y_ref, *, ff_tile):
    mixed = jnp.concatenate([oa_ref[...], ob_ref[...], oc_ref[...]], axis=-1)
    y_ref[...] = x_ref[...] + _mm(mixed, wo_ref[...])
    h2 = (_rms(y_ref[...]) * g2_ref[...]).astype(_MXU_DTYPE)
    for j in range(D_FF // ff_tile):
        up = jnp.dot(h2, wup_ref[:, j * ff_tile:(j + 1) * ff_tile], preferred_element_type=F32)
        y_ref[...] += _mm(jnp.square(jnp.maximum(up, 0.0)), wdn_ref[j * ff_tile:(j + 1) * ff_tile, :])


def _mlp(x2d, oa, ob, oc, wo, g2, wup, wdn):
    n = x2d.shape[0]
    tm = min(_ROW_TILE, n)
    assert n % tm == 0
    row = lambda w: pl.BlockSpec((tm, w), lambda i: (i, 0))
    return pl.pallas_call(
        functools.partial(_mlp_body, ff_tile=1024),
        grid=(n // tm,),
        in_specs=[row(D_MODEL), row(A_WIDTH), row(B_WIDTH), row(C_WIDTH), _const_spec((D_MODEL, D_MODEL), True),
                  _const_spec((1, D_MODEL)), _const_spec((D_MODEL, D_FF), True), _const_spec((D_FF, D_MODEL), True)],
        out_specs=row(D_MODEL),
        out_shape=jax.ShapeDtypeStruct((n, D_MODEL), F32),
        compiler_params=pltpu.CompilerParams(dimension_semantics=("parallel",), vmem_limit_bytes=_VMEM_LIMIT),
        name="mlp",
    )(x2d, oa, ob, oc, wo, g2, wup, wdn)


def _prep_layer_params(ln1_g, w_in, a_qnorm_g, a_knorm_g, gla_w_gate, gla_b_gate, gla_onorm_g,
                       conv_w, gdn_a_log, gdn_dt_bias, gdn_onorm_g, w_o, ln2_g, w_up, w_down):
    cols = [jnp.zeros((D_MODEL, _PAD[name]), w_in.dtype) if name in _PAD
            else w_in[:, _SRC[name]:_SRC[name] + _SIZE[name]] for name in _NEW_ORDER]
    lane_emb = lambda vec, start: jnp.zeros((1, _LANES), F32).at[0, start:start + vec.shape[0]].set(vec.astype(F32))
    return dict(
        g1=ln1_g.reshape(1, D_MODEL).astype(F32),
        w_perm=jnp.concatenate(cols, axis=1).astype(_MXU_DTYPE),
        gq=jnp.tile(a_qnorm_g.astype(F32), A_HEADS).reshape(1, A_WIDTH),
        gk=jnp.tile(a_knorm_g.astype(F32), A_HEADS).reshape(1, A_WIDTH),
        wg_emb=jnp.zeros((_LANES, B_HEADS * B_DK), F32).at[_M_GLR:_M_GLR + GATE_RANK].set(gla_w_gate.astype(F32)),
        bg=gla_b_gate.reshape(1, -1).astype(F32),
        og_b=jnp.tile(gla_onorm_g.astype(F32), B_HEADS).reshape(1, B_WIDTH),
        conv_w=conv_w.astype(F32),
        alog_emb=lane_emb(gdn_a_log, _M_CA),
        dtb_emb=lane_emb(gdn_dt_bias, _M_CA),
        og_c=jnp.tile(gdn_onorm_g.astype(F32), C_HEADS).reshape(1, C_WIDTH),
        wo=w_o.astype(_MXU_DTYPE), g2=ln2_g.reshape(1, D_MODEL).astype(F32),
        wup=w_up.astype(_MXU_DTYPE), wdn=w_down.astype(_MXU_DTYPE),
    )


def _layer(x, cache, p, tbl, tbl_off):
    bsz, t, _ = x.shape
    n = bsz * t
    assert t >= CONV_W - 1
    c = min(CHUNK, t)
    x2d = x.reshape(n, D_MODEL)
    qa, ka, va, qi, misc, zb, zc = _inproj(x2d, p["g1"], p["w_perm"], p["gq"], p["gk"])
    r3 = lambda a: a.reshape(bsz, t, a.shape[-1])
    aki = misc[:, 0:IDX_DIM]

    if cache is None:
        past = 0
        k_all, v_all, ki_all = r3(ka), r3(va), r3(aki)
        s0_gla = jnp.zeros((bsz, B_HEADS, B_DV, B_DK), F32)
        s0_gdn = jnp.zeros((bsz, C_HEADS, C_DK, C_DV), F32)
        hist8 = jnp.zeros((bsz, 8, 3 * C_WIDTH), F32)
    else:
        ck, cv, cki, sg, sd, cbuf = cache
        past = ck.shape[1]
        k_all = jnp.concatenate([ck.reshape(bsz, past, A_WIDTH).astype(F32), r3(ka)], axis=1)
        v_all = jnp.concatenate([cv.reshape(bsz, past, A_WIDTH).astype(F32), r3(va)], axis=1)
        ki_all = jnp.concatenate([cki.astype(F32), r3(aki)], axis=1)
        s0_gla = jnp.swapaxes(sg.astype(F32), 2, 3)
        s0_gdn = sd.astype(F32)
        hist8 = jnp.pad(cbuf.astype(F32), ((0, 0), (8 - (CONV_W - 1), 0), (0, 0)))

    o_a = _attn(r3(qa), r3(qi), r3(misc[:, _M_WI:_M_WI + IDX_HEADS]), k_all.astype(_MXU_DTYPE),
                v_all.astype(_MXU_DTYPE), ki_all.astype(_MXU_DTYPE), tbl, tbl_off, past)
    o_b, s_gla_t = _gla(r3(zb), r3(misc), p["wg_emb"], p["bg"], p["og_b"], s0_gla, c)
    o_c, s_gdn = _gdn(r3(zc), r3(misc), p["conv_w"], p["alog_emb"], p["dtb_emb"], p["og_c"], s0_gdn, hist8, c)
    y = _mlp(x2d, o_a.reshape(n, A_WIDTH), o_b.reshape(n, B_WIDTH), o_c.reshape(n, C_WIDTH),
             p["wo"], p["g2"], p["wup"], p["wdn"]).reshape(bsz, t, D_MODEL)

    s_gla = jnp.swapaxes(s_gla_t, 2, 3)
    new_conv = r3(zc)[:, t - (CONV_W - 1):, 0:3 * C_WIDTH]
    state = (ka.reshape(bsz, t, A_HEADS, HEAD_DIM), va.reshape(bsz, t, A_HEADS, HEAD_DIM), r3(aki),
             s_gla, s_gdn, new_conv)
    return y, state


def kernel(x_prompt, x_sample, cache_a_k, cache_a_v, cache_a_kidx, state_gla, state_gdn, state_conv, rel_bias, ln1_g, w_in, a_qnorm_g, a_knorm_g, gla_w_gate, gla_b_gate, gla_onorm_g, conv_w, gdn_a_log, gdn_dt_bias, gdn_onorm_g, w_o, ln2_g, w_up, w_down):
    depth = ln1_g.shape[0]
    yp, ys = x_prompt, x_sample
    past = cache_a_k.shape[2]
    tbl_p = _bias_tables(rel_bias, x_prompt.shape[1], x_prompt.shape[1], 0)
    tbl_s = _bias_tables(rel_bias, x_sample.shape[1], past + x_sample.shape[1], past)
    new_p, new_s = [], []
    for l in range(depth):
        p = _prep_layer_params(ln1_g[l], w_in[l], a_qnorm_g[l], a_knorm_g[l], gla_w_gate[l],
                               gla_b_gate[l], gla_onorm_g[l], conv_w[l], gdn_a_log[l], gdn_dt_bias[l],
                               gdn_onorm_g[l], w_o[l], ln2_g[l], w_up[l], w_down[l])
        yp, st_p = _layer(yp, None, p, *tbl_p)
        cache_l = (cache_a_k[l], cache_a_v[l], cache_a_kidx[l], state_gla[l], state_gdn[l], state_conv[l])
        ys, st_s = _layer(ys, cache_l, p, *tbl_s)
        new_p.append(st_p)
        new_s.append(st_s)
    stack = lambda states, i: jnp.stack([s[i] for s in states], axis=0)
    return (yp, ys) + tuple(stack(new_p, i) for i in range(6)) + tuple(stack(new_s, i) for i in range(6))
```

```python
import functools
import math

import numpy as np
import jax
import jax.numpy as jnp
from jax import lax
from jax.experimental import pallas as pl
from jax.experimental.pallas import tpu as pltpu

D_MODEL = 1024
CHUNK = 64
HEAD_DIM = 64
A_HEADS = 4
A_WIDTH = A_HEADS * HEAD_DIM
IDX_HEADS = 8
IDX_DIM = 64
TOPK_MAX = 256
N_BUCKETS = 32
MAX_DISTANCE = 128
B_HEADS = 4
B_DK = 32
B_DV = 64
B_WIDTH = B_HEADS * B_DV
GATE_RANK = 16
GATE_TAU = 16.0
C_HEADS = 8
C_DK = 64
C_DV = 64
C_WIDTH = C_HEADS * C_DV
CONV_W = 4
D_FF = 4 * D_MODEL
EPS = 1e-6

F32 = jnp.float32
BF16 = jnp.bfloat16
_MXU_DTYPE = BF16
_HI = lax.Precision.HIGHEST
_VMEM_LIMIT = 56 * 1024 * 1024
_LANES = 128
_ROW_TILE = 512
_SEQS_PER_STEP = 2
_NEG = -1e30
_LOG2E = math.log2(math.e)
_INT_MIN = -2 ** 31

_SPLIT_NAMES = ("aq", "ak", "av", "aqi", "aki", "awi", "bq", "bk", "bv", "bglr", "bog",
                "cq", "ck", "cv", "ca", "cb", "cog")
_SPLIT_SIZES = (A_WIDTH, A_WIDTH, A_WIDTH, IDX_HEADS * IDX_DIM, IDX_DIM, IDX_HEADS,
                B_HEADS * B_DK, B_HEADS * B_DK, B_WIDTH, GATE_RANK, B_WIDTH,
                C_HEADS * C_DK, C_HEADS * C_DK, C_WIDTH, C_HEADS, C_HEADS, C_WIDTH)
D_IN = sum(_SPLIT_SIZES)
_SRC = dict(zip(_SPLIT_NAMES, np.concatenate([[0], np.cumsum(_SPLIT_SIZES)[:-1]]).tolist()))
_SIZE = dict(zip(_SPLIT_NAMES, _SPLIT_SIZES))

_M_WI, _M_GLR, _M_CA, _M_CB = 64, 72, 88, 96
_NEW_ORDER = ("aq", "ak", "av", "aqi", "aki", "awi", "bglr", "ca", "cb", "pad24",
              "bq", "bk", "bv", "bog", "cq", "ck", "cv", "cog")
_PAD = {"pad24": 24}


def _new_offsets():
    offs, pos = {}, 0
    for name in _NEW_ORDER:
        offs[name] = pos
        pos += _PAD.get(name, 0) or _SIZE[name]
    return offs, pos


_OFF, D_Z = _new_offsets()
_C_QA, _C_KA, _C_VA, _C_QI, _C_MISC = _OFF["aq"], _OFF["ak"], _OFF["av"], _OFF["aqi"], _OFF["aki"]
_C_B, _C_C = _OFF["bq"], _OFF["cq"]
_W_B = 2 * B_HEADS * B_DK + 2 * B_WIDTH
_W_C = 4 * C_WIDTH
assert _C_MISC % _LANES == 0 and _C_B == _C_MISC + _LANES and _C_C == _C_B + _W_B and D_Z == _C_C + _W_C
assert (_OFF["awi"], _OFF["bglr"], _OFF["ca"], _OFF["cb"]) == tuple(_C_MISC + m for m in (_M_WI, _M_GLR, _M_CA, _M_CB))


def _mm(a, b):
    return jnp.dot(a.astype(_MXU_DTYPE), b.astype(_MXU_DTYPE), preferred_element_type=F32)


def _mm_nt(a, b):
    return lax.dot_general(a.astype(_MXU_DTYPE), b.astype(_MXU_DTYPE), (((1,), (1,)), ((), ())),
                           preferred_element_type=F32)


def _mmx(a, b):
    return jnp.dot(a, b, preferred_element_type=F32, precision=_HI)


def _split_bf16(x):
    hi = x.astype(BF16)
    return hi, (x - hi.astype(F32)).astype(BF16)


def _bmm3(a, b):
    (ah, al), (bh, bl) = a, b
    bmm = lambda u, v: jnp.einsum("hij,hjk->hik", u, v, preferred_element_type=F32)
    n = ah.shape[1]
    both = bmm(jnp.concatenate([ah, al], axis=1), bh)
    return both[:, 0:n, :] + (both[:, n:, :] + bmm(ah, bl))


def _cumsum_rows(x):
    r = lax.broadcasted_iota(jnp.int32, x.shape, 0)
    s = 1
    while s < x.shape[0]:
        x = x + jnp.where(r >= s, pltpu.roll(x, s, 0), 0.0)
        s *= 2
    return x


def _heads3(x, n_heads, width):
    return jnp.stack([x[:, h * width:(h + 1) * width] for h in range(n_heads)], axis=0)


def _lanes2(x3):
    return jnp.concatenate([x3[h] for h in range(x3.shape[0])], axis=-1)


def _cols3(x, lane0, n_heads, width):
    c = x.shape[0]
    return jnp.stack([jnp.broadcast_to(x[:, lane0 + h:lane0 + h + 1], (c, width)) for h in range(n_heads)], axis=0)


def _bmm(a, b):
    return jnp.einsum("hij,hjk->hik", a.astype(_MXU_DTYPE), b.astype(_MXU_DTYPE), preferred_element_type=F32)


def _bmm_nt(a, b):
    return jnp.einsum("hik,hjk->hij", a.astype(_MXU_DTYPE), b.astype(_MXU_DTYPE), preferred_element_type=F32)


def _bmm_tn(a, b):
    return jnp.einsum("hck,hcv->hkv", a.astype(_MXU_DTYPE), b.astype(_MXU_DTYPE), preferred_element_type=F32)


def _rms(x):
    return x * lax.rsqrt(jnp.mean(x * x, axis=-1, keepdims=True) + EPS)


def _group_norm(x, group, mean):
    outs = []
    for g in range(x.shape[-1] // group):
        xs = x[:, g * group:(g + 1) * group]
        ss = jnp.sum(xs * xs, axis=-1, keepdims=True)
        outs.append(xs * lax.rsqrt((ss / group if mean else ss) + EPS))
    return jnp.concatenate(outs, axis=-1)


def _fold_rows(w, rows=8):
    parts = [w[i * rows:(i + 1) * rows, :] for i in range(w.shape[0] // rows)]
    while len(parts) > 1:
        parts = [parts[i] + parts[i + 1] for i in range(0, len(parts), 2)]
    return parts[0]


def _const_spec(shape, single_buffer=False):
    zeros = (0,) * len(shape)
    if single_buffer:
        return pl.BlockSpec(shape, lambda *_: zeros, pipeline_mode=pl.Buffered(1))
    return pl.BlockSpec(shape, lambda *_: zeros)


def _inproj_body(x_ref, g1_ref, w_ref, gq_ref, gk_ref, *out_refs, head_major, kt):
    h = (_rms(x_ref[...]) * g1_ref[...]).astype(_MXU_DTYPE)

    def proj(c0, width):
        return jnp.dot(h, w_ref[:, c0:c0 + width], preferred_element_type=F32)

    qn = _group_norm(proj(_C_QA, A_WIDTH), HEAD_DIM, True) * gq_ref[...] * (HEAD_DIM ** -0.5 * _LOG2E)
    kn = _group_norm(proj(_C_KA, A_WIDTH), HEAD_DIM, True) * gk_ref[...]
    v = proj(_C_VA, A_WIDTH)
    qi = proj(_C_QI, IDX_HEADS * IDX_DIM) * (IDX_DIM ** -0.5)
    misc = proj(_C_MISC, _LANES)
    if head_major:
        qt_ref, qit_ref, wit_ref, khm_ref, vt_ref, ki_ref, ka_ref, va_ref, misc_ref, zb_ref, zc_ref = out_refs
        for hd in range(A_HEADS):
            hs = slice(hd * HEAD_DIM, (hd + 1) * HEAD_DIM)
            qt_ref[hd] = qn[:, hs].T.astype(qt_ref.dtype)
            khm_ref[hd] = kn[:, hs].astype(khm_ref.dtype)
            for j in range(v.shape[0] // kt):
                vt_ref[hd, j] = v[j * kt:(j + 1) * kt, hs].T.astype(vt_ref.dtype)
        for hd in range(IDX_HEADS):
            qit_ref[hd] = qi[:, hd * IDX_DIM:(hd + 1) * IDX_DIM].T.astype(qit_ref.dtype)
        wit_ref[...] = misc.T[_M_WI:_M_WI + IDX_HEADS, :]
        ki_ref[...] = misc[:, 0:IDX_DIM].astype(ki_ref.dtype)
    else:
        qa_ref, qi_ref, ka_ref, va_ref, misc_ref, zb_ref, zc_ref = out_refs
        qa_ref[...] = qn.astype(qa_ref.dtype)
        qi_ref[...] = qi.astype(qi_ref.dtype)
    ka_ref[...] = kn
    va_ref[...] = v
    misc_ref[...] = misc
    zb_ref[...] = proj(_C_B, _W_B)
    zc_ref[...] = proj(_C_C, _W_C)


def _inproj(x2d, g1, w_perm, gq, gk):
    n = x2d.shape[0]
    tm = min(_ROW_TILE, n)
    assert n % tm == 0
    row = lambda w: pl.BlockSpec((tm, w), lambda i: (i, 0))
    widths = (A_WIDTH, IDX_HEADS * IDX_DIM, A_WIDTH, A_WIDTH, _LANES, _W_B, _W_C)
    dtypes = (_MXU_DTYPE, _MXU_DTYPE, F32, F32, F32, F32, F32)
    return pl.pallas_call(
        functools.partial(_inproj_body, head_major=False, kt=None),
        grid=(n // tm,),
        in_specs=[row(D_MODEL), _const_spec((1, D_MODEL)), _const_spec((D_MODEL, D_Z), True),
                  _const_spec((1, A_WIDTH)), _const_spec((1, A_WIDTH))],
        out_specs=[row(w) for w in widths],
        out_shape=tuple(jax.ShapeDtypeStruct((n, w), d) for w, d in zip(widths, dtypes)),
        compiler_params=pltpu.CompilerParams(dimension_semantics=("parallel",), vmem_limit_bytes=_VMEM_LIMIT),
        name="inproj",
    )(x2d, g1, w_perm, gq, gk)


def _inproj_head_major(x, g1, w_perm, gq, gk, kt):
    bsz, t, _ = x.shape
    tm = min(_ROW_TILE, t)
    assert t % tm == 0 and tm % kt == 0
    row = lambda w: pl.BlockSpec((None, tm, w), lambda b, i: (b, i, 0))
    rows = lambda w, d: jax.ShapeDtypeStruct((bsz, t, w), d)
    lanes = lambda nh: pl.BlockSpec((None, nh, HEAD_DIM, tm), lambda b, i: (b, 0, 0, i))
    out_specs = [lanes(A_HEADS), lanes(IDX_HEADS), pl.BlockSpec((None, IDX_HEADS, tm), lambda b, i: (b, 0, i)),
                 pl.BlockSpec((None, A_HEADS, tm, HEAD_DIM), lambda b, i: (b, 0, i, 0)),
                 pl.BlockSpec((None, A_HEADS, tm // kt, HEAD_DIM, kt), lambda b, i: (b, 0, i, 0, 0)),
                 row(IDX_DIM), row(A_WIDTH), row(A_WIDTH), row(_LANES), row(_W_B), row(_W_C)]
    out_shape = (jax.ShapeDtypeStruct((bsz, A_HEADS, HEAD_DIM, t), _MXU_DTYPE),
                 jax.ShapeDtypeStruct((bsz, IDX_HEADS, IDX_DIM, t), _MXU_DTYPE),
                 jax.ShapeDtypeStruct((bsz, IDX_HEADS, t), F32),
                 jax.ShapeDtypeStruct((bsz, A_HEADS, t, HEAD_DIM), _MXU_DTYPE),
                 jax.ShapeDtypeStruct((bsz, A_HEADS, t // kt, HEAD_DIM, kt), _MXU_DTYPE),
                 rows(IDX_DIM, _MXU_DTYPE), rows(A_WIDTH, F32), rows(A_WIDTH, F32), rows(_LANES, F32),
                 rows(_W_B, F32), rows(_W_C, F32))
    return pl.pallas_call(
        functools.partial(_inproj_body, head_major=True, kt=kt),
        grid=(bsz, t // tm),
        in_specs=[row(D_MODEL), _const_spec((1, D_MODEL)), _const_spec((D_MODEL, D_Z), True),
                  _const_spec((1, A_WIDTH)), _const_spec((1, A_WIDTH))],
        out_specs=out_specs,
        out_shape=out_shape,
        compiler_params=pltpu.CompilerParams(dimension_semantics=("parallel", "parallel"),
                                             vmem_limit_bytes=_VMEM_LIMIT),
        name="inproj",
    )(x, g1, w_perm, gq, gk)


def _attn_body(q_ref, qi_ref, wi_ref, k_ref, vt_ref, ki_ref, tbl_ref, o_ref, key_ref, half_ref, acc_ref, s_ref, s2_ref, p_ref, p2_ref, *,
               tq, kt, n_tiles, past, n_keys, topk, tbl_off, n_tbl, idx_bits):
    qb = pl.program_id(1)
    q0 = past + qb * tq
    n_kt = jnp.minimum(n_tiles, (q0 + tq + kt - 1) // kt)
    krow = lax.broadcasted_iota(jnp.int32, (kt, tq), 0)
    q_chunk = (q0 + lax.broadcasted_iota(jnp.int32, (1, tq), 1)) // CHUNK
    adm_end = jnp.minimum((q_chunk + 1) * CHUNK, n_keys)
    wi = wi_ref[...] * (IDX_HEADS ** -0.5)

    def score_tile(t, carry):
        off = pl.multiple_of(t * kt, kt)
        ki_t = ki_ref[pl.ds(off, kt), :]
        acc = jnp.zeros((kt, tq), F32)
        for h in range(IDX_HEADS):
            acc = acc + wi[h:h + 1, :] * jnp.maximum(_mm(ki_t, qi_ref[h]), 0.0)
        bits = pltpu.bitcast(acc, jnp.int32)
        bits = jnp.where(bits == _INT_MIN, 0, bits)
        key = jnp.where(bits >= 0, bits, bits ^ jnp.int32(0x7FFFFFFF))
        key = jnp.where(krow < adm_end - off, key, _INT_MIN)
        key_ref[t] = key
        half_ref[t] = (key >> 16).astype(jnp.int16)
        return carry

    lax.fori_loop(0, n_kt, score_tile, 0)

    @pl.when(n_kt % 2 == 1)
    def _():
        key_ref[n_kt] = jnp.full((kt, tq), _INT_MIN, jnp.int32)
        half_ref[n_kt] = jnp.full((kt, tq), -2 ** 15, jnp.int16)

    n_pair = (n_kt + 1) // 2

    def count(*pred_fns):
        def body(j, cnts):
            t = 2 * j
            k0, k1 = key_ref[t], key_ref[t + 1]
            return tuple(cnt + _fold_rows(jnp.where(fn(t, k0), 1.0, 0.0) + jnp.where(fn(t + 1, k1), 1.0, 0.0))
                         for fn, cnt in zip(pred_fns, cnts))
        cnts = lax.fori_loop(0, n_pair, body, tuple(jnp.zeros((8, tq), F32) for _ in pred_fns))
        return tuple(jnp.sum(cnt, axis=0, keepdims=True) for cnt in cnts)

    def count16(pred_fn):
        one, zero = jnp.int16(1), jnp.int16(0)
        def body(j, cnt):
            w = jnp.where(pred_fn(half_ref[2 * j]), one, zero) + jnp.where(pred_fn(half_ref[2 * j + 1]), one, zero)
            return cnt + _fold_rows(w, 16)
        cnt = lax.fori_loop(0, n_pair, body, jnp.zeros((16, tq), jnp.int16))
        return jnp.sum(cnt.astype(jnp.int32), axis=0, keepdims=True)

    def kth_largest16(kth):
        def bit(i, v):
            cand = v + lax.shift_left(jnp.int32(1), 15 - i)
            cand16 = cand.astype(jnp.int16)
            return jnp.where(count16(lambda k: k >= cand16) >= kth, cand, v)
        return lax.fori_loop(0, 16, bit, jnp.full((1, tq), -2 ** 15, jnp.int32))

    thr_hi = kth_largest16(topk)
    thr_hi16 = thr_hi.astype(jnp.int16)
    above = count16(lambda k: k > thr_hi16)

    def low_halves(t, carry):
        key = key_ref[t]
        half_ref[t] = jnp.where((key >> 16) == thr_hi, (key & 0xFFFF) - 2 ** 15, -2 ** 15).astype(jnp.int16)
        return carry

    lax.fori_loop(0, 2 * n_pair, low_halves, 0)
    thr = thr_hi * 2 ** 16 + (kth_largest16(topk - above) + 2 ** 15)
    thr = jnp.maximum(thr, _INT_MIN + 1)

    cnt_gt, cnt_ge = count(lambda t, k: k > thr, lambda t, k: k >= thr)
    need = topk - cnt_gt
    has_excess = jnp.max(jnp.where(cnt_ge > topk, 1.0, 0.0)) > 0.0

    def tie_search():
        def bit(i, last):
            cand = last + lax.shift_left(jnp.int32(1), idx_bits - 1 - i)
            below, = count(lambda t, k: (k == thr) & ((t * kt + krow) < cand))
            return jnp.where(below < need, cand, last)
        return lax.fori_loop(0, idx_bits, bit, jnp.zeros((1, tq), jnp.int32))

    last = lax.cond(has_excess, tie_search, lambda: jnp.full((1, tq), 2 ** 30, jnp.int32))

    acc_ref[...] = jnp.zeros(acc_ref.shape, F32)

    def logits(t, dst_ref):
        off = pl.multiple_of(t * kt, kt)
        for h in range(A_HEADS):
            dst_ref[h] = _mm(k_ref[h, pl.ds(off, kt), :], q_ref[h])

    def softmax_pv(t, src_ref, pr_ref, carry):
        ms, ls = carry
        off = pl.multiple_of(t * kt, kt)
        key = key_ref[t]
        sel = (key > thr) | ((key == thr) & ((off + krow) <= last))
        ti = jnp.clip(t - qb + tbl_off, 0, n_tbl - 1)
        new_ms, new_ls, alphas = [], [], []
        for h in range(A_HEADS):
            s = jnp.where(sel, src_ref[h] + tbl_ref[ti, h], _NEG)
            m_new = jnp.maximum(ms[h], jnp.max(s, axis=0, keepdims=True))
            alpha = jnp.exp2(ms[h] - m_new)
            p = jnp.exp2(s - m_new)
            new_ls.append(alpha * ls[h] + jnp.sum(p, axis=0, keepdims=True))
            new_ms.append(m_new)
            alphas.append(alpha)
            pr_ref[h] = p.astype(pr_ref.dtype)
        for h in range(A_HEADS):
            acc_ref[h] = alphas[h] * acc_ref[h] + jnp.dot(vt_ref[h, t], pr_ref[h], preferred_element_type=F32)
        return tuple(new_ms), tuple(new_ls)

    logits(0, s_ref)

    def attend_pair(j, carry):
        t0 = 2 * j
        logits(t0 + 1, s2_ref)
        carry = softmax_pv(t0, s_ref, p_ref, carry)
        logits(jnp.minimum(t0 + 2, 2 * n_pair - 2), s_ref)
        return softmax_pv(t0 + 1, s2_ref, p2_ref, carry)

    m0 = tuple(jnp.full((1, tq), _NEG, F32) for _ in range(A_HEADS))
    l0 = tuple(jnp.zeros((1, tq), F32) for _ in range(A_HEADS))
    _, ls = lax.fori_loop(0, n_pair, attend_pair, (m0, l0))
    for h in range(A_HEADS):
        o_ref[h] = acc_ref[h] / ls[h]


def _rel_bucket(rel):
    nb = N_BUCKETS // 2
    ret = jnp.where(rel > 0, nb, 0)
    n = jnp.abs(rel)
    max_exact = nb // 2
    nf = jnp.maximum(n, 1).astype(F32)
    large = max_exact + (jnp.log(nf / max_exact) / math.log(MAX_DISTANCE / max_exact)
                         * (nb - max_exact)).astype(jnp.int32)
    large = jnp.minimum(large, nb - 1)
    return ret + jnp.where(n < max_exact, n, large)


def _attn_tiling(t, n_keys):
    kt = 256
    tq = min(kt, t)
    assert t % tq == 0
    return tq, kt, t // tq, 2 * -(-n_keys // (2 * kt))


def _bias_tables(rel_bias, t, n_keys, past):
    tq, kt, nqb, n_tiles = _attn_tiling(t, n_keys)
    j = jnp.arange(kt)[:, None]
    i = jnp.arange(tq)[None, :]
    if nqb == 1:
        rels = [tile * kt + j - (past + i) for tile in range(n_tiles)]
        off = 0
    else:
        assert past == 0 and tq == kt and kt >= MAX_DISTANCE
        rels = [j - i - 2 * kt, j - i - kt, j - i]
        off = 2
    onehot = jax.nn.one_hot(_rel_bucket(jnp.stack(rels, axis=0)), N_BUCKETS, dtype=F32)
    tbl = jnp.einsum("nktb,bh->nhkt", onehot, rel_bias.astype(F32) * _LOG2E, precision=_HI)
    return tbl, off


def _attn_head_major(q_t, qi_t, wi_t, k_hm, vt_hm, ki_all, tbl, tbl_off, past, n_keys):
    bsz, _, _, t = q_t.shape
    topk = min(TOPK_MAX, n_keys // 4)
    tq, kt, nqb, n_tiles = _attn_tiling(t, n_keys)
    s_pad = n_tiles * kt
    assert s_pad >= topk and k_hm.shape[2] == s_pad and vt_hm.shape[2] == n_tiles and ki_all.shape[1] == s_pad
    n_tbl = tbl.shape[0]
    body = functools.partial(_attn_body, tq=tq, kt=kt, n_tiles=n_tiles, past=past, n_keys=n_keys, topk=topk,
                             tbl_off=tbl_off, n_tbl=n_tbl, idx_bits=int(s_pad).bit_length())
    return pl.pallas_call(
        body,
        grid=(bsz, nqb),
        in_specs=[pl.BlockSpec((None, A_HEADS, HEAD_DIM, tq), lambda b, i: (b, 0, 0, i)),
                  pl.BlockSpec((None, IDX_HEADS, IDX_DIM, tq), lambda b, i: (b, 0, 0, i)),
                  pl.BlockSpec((None, IDX_HEADS, tq), lambda b, i: (b, 0, i)),
                  pl.BlockSpec((None, A_HEADS, s_pad, HEAD_DIM), lambda b, i: (b, 0, 0, 0)),
                  pl.BlockSpec((None, A_HEADS, n_tiles, HEAD_DIM, kt), lambda b, i: (b, 0, 0, 0, 0)),
                  pl.BlockSpec((None, s_pad, IDX_DIM), lambda b, i: (b, 0, 0)),
                  _const_spec((n_tbl, A_HEADS, kt, tq), True)],
        out_specs=pl.BlockSpec((None, A_HEADS, HEAD_DIM, tq), lambda b, i: (b, 0, 0, i)),
        out_shape=jax.ShapeDtypeStruct((bsz, A_HEADS, HEAD_DIM, t), F32),
        scratch_shapes=[pltpu.VMEM((n_tiles, kt, tq), jnp.int32), pltpu.VMEM((n_tiles, kt, tq), jnp.int16),
                        pltpu.VMEM((A_HEADS, HEAD_DIM, tq), F32),
                        pltpu.VMEM((A_HEADS, kt, tq), F32), pltpu.VMEM((A_HEADS, kt, tq), F32),
                        pltpu.VMEM((A_HEADS, kt, tq), _MXU_DTYPE), pltpu.VMEM((A_HEADS, kt, tq), _MXU_DTYPE)],
        compiler_params=pltpu.CompilerParams(dimension_semantics=("parallel", "parallel"),
                                             vmem_limit_bytes=_VMEM_LIMIT),
        name="attn",
    )(q_t, qi_t, wi_t, k_hm, vt_hm, ki_all, tbl)


def _attn(q, qi, wi, k_all, v_all, ki_all, tbl, tbl_off, past):
    bsz, t, _ = q.shape
    n_keys = k_all.shape[1]
    _, kt, _, n_tiles = _attn_tiling(t, n_keys)
    s_pad = n_tiles * kt
    if s_pad != n_keys:
        k_all, v_all, ki_all = (jnp.pad(a, ((0, 0), (0, s_pad - n_keys), (0, 0))) for a in (k_all, v_all, ki_all))
    q_t = q.reshape(bsz, t, A_HEADS, HEAD_DIM).transpose(0, 2, 3, 1)
    qi_t = qi.reshape(bsz, t, IDX_HEADS, IDX_DIM).transpose(0, 2, 3, 1)
    k_hm = k_all.reshape(bsz, s_pad, A_HEADS, HEAD_DIM).transpose(0, 2, 1, 3)
    vt_hm = v_all.reshape(bsz, n_tiles, kt, A_HEADS, HEAD_DIM).transpose(0, 3, 1, 4, 2)
    o_t = _attn_head_major(q_t, qi_t, wi.transpose(0, 2, 1), k_hm, vt_hm, ki_all, tbl, tbl_off, past, n_keys)
    return o_t.transpose(0, 3, 1, 2).reshape(bsz, t, A_WIDTH)


def _gla_chunk(zb, misc, wg, bg, og, st):
    c = zb.shape[0]
    dkw = B_HEADS * B_DK
    q = zb[:, 0:dkw] * (B_DK ** -0.5)
    k = zb[:, dkw:2 * dkw]
    v3 = _heads3(zb[:, 2 * dkw:2 * dkw + B_WIDTH], B_HEADS, B_DV)
    gate = zb[:, 2 * dkw + B_WIDTH:]
    log_a = jax.nn.log_sigmoid(_mmx(misc, wg) + bg) / GATE_TAU
    b = _cumsum_rows(log_a)
    b_last = b[c - 1:c, :]
    b_mid = b[c // 2 - 1:c // 2, :]
    att = _bmm_nt(_heads3(q * jnp.exp(b - b_mid), B_HEADS, B_DK), _heads3(k * jnp.exp(b_mid - b), B_HEADS, B_DK))
    ri = lax.broadcasted_iota(jnp.int32, att.shape, 1)
    cj = lax.broadcasted_iota(jnp.int32, att.shape, 2)
    att = jnp.where(ri >= cj, att, 0.0)
    o3 = _bmm_nt(_heads3(q * jnp.exp(b), B_HEADS, B_DK), st) + _bmm(att, v3)
    upd = _bmm_tn(v3, _heads3(k * jnp.exp(b_last - b), B_HEADS, B_DK))
    st_new = st * _heads3(jnp.exp(b_last), B_HEADS, B_DK) + upd
    o = _lanes2(o3 * lax.rsqrt(jnp.mean(o3 * o3, axis=-1, keepdims=True) + EPS))
    return o * og * jax.nn.silu(gate), st_new


def _gla_body(zb_ref, misc_ref, wg_ref, bg_ref, og_ref, s0_ref, o_ref, s_out_ref, s_ref):
    @pl.when(pl.program_id(1) == 0)
    def _():
        s_ref[...] = s0_ref[...]

    for g in range(zb_ref.shape[0]):
        o, st = _gla_chunk(zb_ref[g], misc_ref[g], wg_ref[...], bg_ref[...], og_ref[...], s_ref[g])
        o_ref[g] = o
        s_ref[g] = st
        s_out_ref[g] = st


def _gla(zb, misc, wg_emb, bg, og, s0_t, c):
    bsz, t, _ = zb.shape
    dkw = B_HEADS * B_DK
    g = _SEQS_PER_STEP
    assert bsz % g == 0
    blk = lambda w: pl.BlockSpec((g, c, w), lambda b, i: (b, i, 0))
    st = pl.BlockSpec((g, B_HEADS, B_DV, B_DK), lambda b, i: (b, 0, 0, 0))
    return pl.pallas_call(
        _gla_body,
        grid=(bsz // g, t // c),
        in_specs=[blk(_W_B), blk(_LANES), _const_spec((_LANES, dkw)), _const_spec((1, dkw)),
                  _const_spec((1, B_WIDTH)), st],
        out_specs=[blk(B_WIDTH), st],
        out_shape=(jax.ShapeDtypeStruct((bsz, t, B_WIDTH), F32),
                   jax.ShapeDtypeStruct((bsz, B_HEADS, B_DV, B_DK), F32)),
        scratch_shapes=[pltpu.VMEM((g, B_HEADS, B_DV, B_DK), F32)],
        compiler_params=pltpu.CompilerParams(dimension_semantics=("parallel", "arbitrary"),
                                             vmem_limit_bytes=_VMEM_LIMIT),
        name="gla",
    )(zb, misc, wg_emb, bg, og, s0_t)


def _gdn_chunk(zc_ref, misc, cw_ref, alog, dtb, og, xpad_ref, s3):
    c = misc.shape[0]
    w3 = 3 * C_WIDTH
    xpad_ref[8:8 + c, :] = zc_ref[:, 0:w3]
    conv = xpad_ref[5:5 + c, :] * cw_ref[0:1, :]
    for i in range(1, CONV_W):
        conv = conv + xpad_ref[5 + i:5 + i + c, :] * cw_ref[i:i + 1, :]
    xpad_ref[0:8, :] = xpad_ref[c:c + 8, :]
    act = jax.nn.silu(conv)
    l2 = lambda x3: x3 * lax.rsqrt(jnp.sum(x3 * x3, axis=-1, keepdims=True) + EPS)
    q3 = l2(_heads3(act[:, 0:C_WIDTH], C_HEADS, C_DK)) * (C_DK ** -0.5)
    k3 = l2(_heads3(act[:, C_WIDTH:2 * C_WIDTH], C_HEADS, C_DK))
    v3 = _heads3(act[:, 2 * C_WIDTH:w3], C_HEADS, C_DV)

    g_full = -jnp.exp(alog) * jax.nn.softplus(misc + dtb)
    beta3 = _cols3(jax.nn.sigmoid(misc), _M_CB, C_HEADS, C_DV)
    gcum = _cumsum_rows(g_full)
    gexp3 = _cols3(gcum, _M_CA, C_HEADS, C_DV)
    g_last3 = gexp3[:, c - 1:c, :]
    eg3 = jnp.exp(gexp3)
    gcum_t = gcum.T
    diff3 = _cols3(gcum, _M_CA, C_HEADS, c) - jnp.stack(
        [jnp.broadcast_to(gcum_t[_M_CA + h:_M_CA + h + 1, :], (c, c)) for h in range(C_HEADS)], axis=0)
    ri = lax.broadcasted_iota(jnp.int32, diff3.shape, 1)
    cj = lax.broadcasted_iota(jnp.int32, diff3.shape, 2)
    lmask = jnp.where(ri >= cj, jnp.exp(jnp.minimum(diff3, 0.0)), 0.0)

    kb3 = k3 * beta3
    kq = _bmm_nt(jnp.concatenate([kb3, q3], axis=1), k3)
    m3 = jnp.where(ri > cj, kq[:, 0:c, :] * lmask, 0.0)
    qk3 = kq[:, c:, :] * lmask
    eye = (lax.broadcasted_iota(jnp.int32, (c, c), 0) == lax.broadcasted_iota(jnp.int32, (c, c), 1)).astype(F32)
    inv = eye[None] - m3
    pw = _split_bf16(m3)
    for _ in range(int(math.log2(c)) - 1):
        pw = _split_bf16(_bmm3(pw, pw))
        inv = inv + _bmm3(_split_bf16(inv), pw)
    u3 = _bmm(inv, v3 * beta3)
    w3_ = _bmm(inv, kb3 * eg3)
    wq = _bmm(jnp.concatenate([w3_, q3 * eg3], axis=1), s3)
    v_new = u3 - wq[:, 0:c, :]
    o3 = wq[:, c:, :] + _bmm(qk3, v_new)
    s_new = s3 * jnp.exp(g_last3) + _bmm_tn(k3 * jnp.exp(g_last3 - gexp3), v_new)
    o = _lanes2(o3 * lax.rsqrt(jnp.mean(o3 * o3, axis=-1, keepdims=True) + EPS))
    return o * og * jax.nn.silu(zc_ref[:, w3:]), s_new


def _gdn_body(zc_ref, misc_ref, cw_ref, alog_ref, dtb_ref, og_ref, s0_ref, hist_ref,
              o_ref, s_out_ref, xpad_ref, s_ref):
    @pl.when(pl.program_id(1) == 0)
    def _():
        s_ref[...] = s0_ref[...]
        xpad_ref[:, 0:8, :] = hist_ref[...]

    for g in range(zc_ref.shape[0]):
        o, s_new = _gdn_chunk(zc_ref.at[g], misc_ref[g], cw_ref, alog_ref[...], dtb_ref[...], og_ref[...],
                              xpad_ref.at[g], s_ref[g])
        o_ref[g] = o
        s_ref[g] = s_new
        s_out_ref[g] = s_new


def _gdn(zc, misc, conv_w, alog_emb, dtb_emb, og, s0, hist8, c):
    bsz, t, _ = zc.shape
    w3 = 3 * C_WIDTH
    g = _SEQS_PER_STEP
    assert bsz % g == 0
    blk = lambda w: pl.BlockSpec((g, c, w), lambda b, i: (b, i, 0))
    st = pl.BlockSpec((g, C_HEADS, C_DK, C_DV), lambda b, i: (b, 0, 0, 0))
    return pl.pallas_call(
        _gdn_body,
        grid=(bsz // g, t // c),
        in_specs=[blk(_W_C), blk(_LANES), _const_spec((CONV_W, w3)), _const_spec((1, _LANES)),
                  _const_spec((1, _LANES)), _const_spec((1, C_WIDTH)), st,
                  pl.BlockSpec((g, 8, w3), lambda b, i: (b, 0, 0))],
        out_specs=[blk(C_WIDTH), st],
        out_shape=(jax.ShapeDtypeStruct((bsz, t, C_WIDTH), F32),
                   jax.ShapeDtypeStruct((bsz, C_HEADS, C_DK, C_DV), F32)),
        scratch_shapes=[pltpu.VMEM((g, 8 + c, w3), F32), pltpu.VMEM((g, C_HEADS, C_DK, C_DV), F32)],
        compiler_params=pltpu.CompilerParams(dimension_semantics=("parallel", "arbitrary"),
                                             vmem_limit_bytes=_VMEM_LIMIT),
        name="gdn",
    )(zc, misc, conv_w, alog_emb, dtb_emb, og, s0, hist8)


def _mlp_body(x_ref, oa_ref, ob_ref, oc_ref, wo_ref, g2_ref, wup_ref, wdn_ref, y_ref, *, ff_tile, head_major):
    if head_major:
        oa = jnp.concatenate([oa_ref[h].T for h in range(A_HEADS)], axis=-1)
    else:
        oa = oa_ref[...]
    mixed = jnp.concatenate([oa, ob_ref[...], oc_ref[...]], axis=-1)
    y_ref[...] = x_ref[...] + _mm(mixed, wo_ref[...])
    h2 = (_rms(y_ref[...]) * g2_ref[...]).astype(_MXU_DTYPE)
    for j in range(D_FF // ff_tile):
        up = jnp.dot(h2, wup_ref[:, j * ff_tile:(j + 1) * ff_tile], preferred_element_type=F32)
        y_ref[...] += _mm(jnp.square(jnp.maximum(up, 0.0)), wdn_ref[j * ff_tile:(j + 1) * ff_tile, :])


_MLP_WEIGHT_SPECS = lambda: [_const_spec((D_MODEL, D_MODEL), True), _const_spec((1, D_MODEL)),
                             _const_spec((D_MODEL, D_FF), True), _const_spec((D_FF, D_MODEL), True)]


def _mlp(x2d, oa, ob, oc, wo, g2, wup, wdn):
    n = x2d.shape[0]
    tm = min(_ROW_TILE, n)
    assert n % tm == 0
    row = lambda w: pl.BlockSpec((tm, w), lambda i: (i, 0))
    return pl.pallas_call(
        functools.partial(_mlp_body, ff_tile=1024, head_major=False),
        grid=(n // tm,),
        in_specs=[row(D_MODEL), row(A_WIDTH), row(B_WIDTH), row(C_WIDTH)] + _MLP_WEIGHT_SPECS(),
        out_specs=row(D_MODEL),
        out_shape=jax.ShapeDtypeStruct((n, D_MODEL), F32),
        compiler_params=pltpu.CompilerParams(dimension_semantics=("parallel",), vmem_limit_bytes=_VMEM_LIMIT),
        name="mlp",
    )(x2d, oa, ob, oc, wo, g2, wup, wdn)


def _mlp_head_major(x, oa_t, ob, oc, wo, g2, wup, wdn):
    bsz, t, _ = x.shape
    tm = min(_ROW_TILE, t)
    assert t % tm == 0
    row = lambda w: pl.BlockSpec((None, tm, w), lambda b, i: (b, i, 0))
    return pl.pallas_call(
        functools.partial(_mlp_body, ff_tile=1024, head_major=True),
        grid=(bsz, t // tm),
        in_specs=[row(D_MODEL), pl.BlockSpec((None, A_HEADS, HEAD_DIM, tm), lambda b, i: (b, 0, 0, i)),
                  row(B_WIDTH), row(C_WIDTH)] + _MLP_WEIGHT_SPECS(),
        out_specs=row(D_MODEL),
        out_shape=jax.ShapeDtypeStruct((bsz, t, D_MODEL), F32),
        compiler_params=pltpu.CompilerParams(dimension_semantics=("parallel", "parallel"),
                                             vmem_limit_bytes=_VMEM_LIMIT),
        name="mlp",
    )(x, oa_t, ob, oc, wo, g2, wup, wdn)


def _prep_layer_params(ln1_g, w_in, a_qnorm_g, a_knorm_g, gla_w_gate, gla_b_gate, gla_onorm_g,
                       conv_w, gdn_a_log, gdn_dt_bias, gdn_onorm_g, w_o, ln2_g, w_up, w_down):
    cols = [jnp.zeros((D_MODEL, _PAD[name]), w_in.dtype) if name in _PAD
            else w_in[:, _SRC[name]:_SRC[name] + _SIZE[name]] for name in _NEW_ORDER]
    lane_emb = lambda vec, start: jnp.zeros((1, _LANES), F32).at[0, start:start + vec.shape[0]].set(vec.astype(F32))
    return dict(
        g1=ln1_g.reshape(1, D_MODEL).astype(F32),
        w_perm=jnp.concatenate(cols, axis=1).astype(_MXU_DTYPE),
        gq=jnp.tile(a_qnorm_g.astype(F32), A_HEADS).reshape(1, A_WIDTH),
        gk=jnp.tile(a_knorm_g.astype(F32), A_HEADS).reshape(1, A_WIDTH),
        wg_emb=jnp.zeros((_LANES, B_HEADS * B_DK), F32).at[_M_GLR:_M_GLR + GATE_RANK].set(gla_w_gate.astype(F32)),
        bg=gla_b_gate.reshape(1, -1).astype(F32),
        og_b=jnp.tile(gla_onorm_g.astype(F32), B_HEADS).reshape(1, B_WIDTH),
        conv_w=conv_w.astype(F32),
        alog_emb=lane_emb(gdn_a_log, _M_CA),
        dtb_emb=lane_emb(gdn_dt_bias, _M_CA),
        og_c=jnp.tile(gdn_onorm_g.astype(F32), C_HEADS).reshape(1, C_WIDTH),
        wo=w_o.astype(_MXU_DTYPE), g2=ln2_g.reshape(1, D_MODEL).astype(F32),
        wup=w_up.astype(_MXU_DTYPE), wdn=w_down.astype(_MXU_DTYPE),
    )


def _layer(x, cache, p, tbl, tbl_off):
    bsz, t, _ = x.shape
    n = bsz * t
    assert t >= CONV_W - 1
    c = min(CHUNK, t)
    r3 = lambda a: a.reshape(bsz, t, a.shape[-1])
    mixer_b = lambda zb, misc, s0: _gla(zb, misc, p["wg_emb"], p["bg"], p["og_b"], s0, c)
    mixer_c = lambda zc, misc, s0, hist8: _gdn(zc, misc, p["conv_w"], p["alog_emb"], p["dtb_emb"], p["og_c"], s0, hist8, c)

    if cache is None:
        _, kt, _, n_tiles = _attn_tiling(t, t)
        assert n_tiles * kt == t
        q_t, qi_t, wi_t, k_hm, vt_hm, ki16, ka, va, misc, zb, zc = _inproj_head_major(
            x, p["g1"], p["w_perm"], p["gq"], p["gk"], kt)
        o_a_t = _attn_head_major(q_t, qi_t, wi_t, k_hm, vt_hm, ki16, tbl, tbl_off, 0, t)
        o_b, s_gla_t = mixer_b(zb, misc, jnp.zeros((bsz, B_HEADS, B_DV, B_DK), F32))
        o_c, s_gdn = mixer_c(zc, misc, jnp.zeros((bsz, C_HEADS, C_DK, C_DV), F32), jnp.zeros((bsz, 8, 3 * C_WIDTH), F32))
        y = _mlp_head_major(x, o_a_t, o_b, o_c, p["wo"], p["g2"], p["wup"], p["wdn"])
    else:
        ck, cv, cki, sg, sd, cbuf = cache
        past = ck.shape[1]
        x2d = x.reshape(n, D_MODEL)
        qa, qi, ka, va, misc, zb, zc = _inproj(x2d, p["g1"], p["w_perm"], p["gq"], p["gk"])
        ka, va, misc, zb, zc = r3(ka), r3(va), r3(misc), r3(zb), r3(zc)
        k_all = jnp.concatenate([ck.reshape(bsz, past, A_WIDTH).astype(F32), ka], axis=1)
        v_all = jnp.concatenate([cv.reshape(bsz, past, A_WIDTH).astype(F32), va], axis=1)
        ki_all = jnp.concatenate([cki.astype(F32), misc[..., 0:IDX_DIM]], axis=1)
        o_a = _attn(r3(qa), r3(qi), misc[..., _M_WI:_M_WI + IDX_HEADS], k_all.astype(_MXU_DTYPE),
                    v_all.astype(_MXU_DTYPE), ki_all.astype(_MXU_DTYPE), tbl, tbl_off, past)
        o_b, s_gla_t = mixer_b(zb, misc, jnp.swapaxes(sg.astype(F32), 2, 3))
        hist8 = jnp.pad(cbuf.astype(F32), ((0, 0), (8 - (CONV_W - 1), 0), (0, 0)))
        o_c, s_gdn = mixer_c(zc, misc, sd.astype(F32), hist8)
        y = _mlp(x2d, o_a.reshape(n, A_WIDTH), o_b.reshape(n, B_WIDTH), o_c.reshape(n, C_WIDTH),
                 p["wo"], p["g2"], p["wup"], p["wdn"]).reshape(bsz, t, D_MODEL)

    s_gla = jnp.swapaxes(s_gla_t, 2, 3)
    new_conv = zc[:, t - (CONV_W - 1):, 0:3 * C_WIDTH]
    state = (ka.reshape(bsz, t, A_HEADS, HEAD_DIM), va.reshape(bsz, t, A_HEADS, HEAD_DIM), misc[..., 0:IDX_DIM],
             s_gla, s_gdn, new_conv)
    return y, state


def kernel(x_prompt, x_sample, cache_a_k, cache_a_v, cache_a_kidx, state_gla, state_gdn, state_conv, rel_bias, ln1_g, w_in, a_qnorm_g, a_knorm_g, gla_w_gate, gla_b_gate, gla_onorm_g, conv_w, gdn_a_log, gdn_dt_bias, gdn_onorm_g, w_o, ln2_g, w_up, w_down):
    depth = ln1_g.shape[0]
    yp, ys = x_prompt, x_sample
    past = cache_a_k.shape[2]
    tbl_p = _bias_tables(rel_bias, x_prompt.shape[1], x_prompt.shape[1], 0)
    tbl_s = _bias_tables(rel_bias, x_sample.shape[1], past + x_sample.shape[1], past)
    new_p, new_s = [], []
    for l in range(depth):
        p = _prep_layer_params(ln1_g[l], w_in[l], a_qnorm_g[l], a_knorm_g[l], gla_w_gate[l],
                               gla_b_gate[l], gla_onorm_g[l], conv_w[l], gdn_a_log[l], gdn_dt_bias[l],
                               gdn_onorm_g[l], w_o[l], ln2_g[l], w_up[l], w_down[l])
        yp, st_p = _layer(yp, None, p, *tbl_p)
        cache_l = (cache_a_k[l], cache_a_v[l], cache_a_kidx[l], state_gla[l], state_gdn[l], state_conv[l])
        ys, st_s = _layer(ys, cache_l, p, *tbl_s)
        new_p.append(st_p)
        new_s.append(st_s)
    stack = lambda states, i: jnp.stack([s[i] for s in states], axis=0)
    return (yp, ys) + tuple(stack(new_p, i) for i in range(6)) + tuple(stack(new_s, i) for i in range(6))
```

```python
import functools
import math

import numpy as np
import jax
import jax.numpy as jnp
from jax import lax
from jax.experimental import pallas as pl
from jax.experimental.pallas import tpu as pltpu

D_MODEL = 1024
CHUNK = 64
HEAD_DIM = 64
A_HEADS = 4
A_WIDTH = A_HEADS * HEAD_DIM
IDX_HEADS = 8
IDX_DIM = 64
TOPK_MAX = 256
N_BUCKETS = 32
MAX_DISTANCE = 128
B_HEADS = 4
B_DK = 32
B_DV = 64
B_WIDTH = B_HEADS * B_DV
GATE_RANK = 16
GATE_TAU = 16.0
C_HEADS = 8
C_DK = 64
C_DV = 64
C_WIDTH = C_HEADS * C_DV
CONV_W = 4
D_FF = 4 * D_MODEL
EPS = 1e-6

F32 = jnp.float32
BF16 = jnp.bfloat16
_MXU_DTYPE = BF16
_HI = lax.Precision.HIGHEST
_VMEM_LIMIT = 56 * 1024 * 1024
_LANES = 128
_ROW_TILE = 512
_SEQS_PER_STEP = 4
_NEG = -1e30
_LOG2E = math.log2(math.e)
_INT_MIN = -2 ** 31

_SPLIT_NAMES = ("aq", "ak", "av", "aqi", "aki", "awi", "bq", "bk", "bv", "bglr", "bog",
                "cq", "ck", "cv", "ca", "cb", "cog")
_SPLIT_SIZES = (A_WIDTH, A_WIDTH, A_WIDTH, IDX_HEADS * IDX_DIM, IDX_DIM, IDX_HEADS,
                B_HEADS * B_DK, B_HEADS * B_DK, B_WIDTH, GATE_RANK, B_WIDTH,
                C_HEADS * C_DK, C_HEADS * C_DK, C_WIDTH, C_HEADS, C_HEADS, C_WIDTH)
D_IN = sum(_SPLIT_SIZES)
_SRC = dict(zip(_SPLIT_NAMES, np.concatenate([[0], np.cumsum(_SPLIT_SIZES)[:-1]]).tolist()))
_SIZE = dict(zip(_SPLIT_NAMES, _SPLIT_SIZES))

_M_WI, _M_GLR, _M_CA, _M_CB = 64, 72, 88, 96
_NEW_ORDER = ("aq", "ak", "av", "aqi", "aki", "awi", "bglr", "ca", "cb", "pad24",
              "bq", "bk", "bv", "bog", "cq", "ck", "cv", "cog")
_PAD = {"pad24": 24}


def _new_offsets():
    offs, pos = {}, 0
    for name in _NEW_ORDER:
        offs[name] = pos
        pos += _PAD.get(name, 0) or _SIZE[name]
    return offs, pos


_OFF, D_Z = _new_offsets()
_C_QA, _C_KA, _C_VA, _C_QI, _C_MISC = _OFF["aq"], _OFF["ak"], _OFF["av"], _OFF["aqi"], _OFF["aki"]
_C_B, _C_C = _OFF["bq"], _OFF["cq"]
_W_B = 2 * B_HEADS * B_DK + 2 * B_WIDTH
_W_C = 4 * C_WIDTH
assert _C_MISC % _LANES == 0 and _C_B == _C_MISC + _LANES and _C_C == _C_B + _W_B and D_Z == _C_C + _W_C
assert (_OFF["awi"], _OFF["bglr"], _OFF["ca"], _OFF["cb"]) == tuple(_C_MISC + m for m in (_M_WI, _M_GLR, _M_CA, _M_CB))


def _mm(a, b):
    return jnp.dot(a.astype(_MXU_DTYPE), b.astype(_MXU_DTYPE), preferred_element_type=F32)


def _mmx(a, b):
    return jnp.dot(a, b, preferred_element_type=F32, precision=_HI)


def _split_bf16(x):
    hi = x.astype(BF16)
    return hi, (x - hi.astype(F32)).astype(BF16)


def _cumsum_rows(x):
    r = lax.broadcasted_iota(jnp.int32, x.shape, 0)
    s = 1
    while s < x.shape[0]:
        x = x + jnp.where(r >= s, pltpu.roll(x, s, 0), 0.0)
        s *= 2
    return x


def _heads3(x, n_heads, width):
    return jnp.stack([x[:, h * width:(h + 1) * width] for h in range(n_heads)], axis=0)


def _lanes2(x3):
    return jnp.concatenate([x3[h] for h in range(x3.shape[0])], axis=-1)


def _expand_heads(x, lane0, n_heads, width):
    c = x.shape[0]
    return jnp.concatenate([jnp.broadcast_to(x[:, lane0 + h:lane0 + h + 1], (c, width)) for h in range(n_heads)],
                           axis=-1)


def _bmm(a, b):
    return jnp.einsum("hij,hjk->hik", a.astype(_MXU_DTYPE), b.astype(_MXU_DTYPE), preferred_element_type=F32)


def _bmm_nt(a, b):
    return jnp.einsum("hik,hjk->hij", a.astype(_MXU_DTYPE), b.astype(_MXU_DTYPE), preferred_element_type=F32)


def _bmm_tn(a, b):
    return jnp.einsum("hck,hcv->hkv", a.astype(_MXU_DTYPE), b.astype(_MXU_DTYPE), preferred_element_type=F32)


def _rms(x):
    return x * lax.rsqrt(jnp.mean(x * x, axis=-1, keepdims=True) + EPS)


def _group_norm(x, group, mean):
    outs = []
    for g in range(x.shape[-1] // group):
        xs = x[:, g * group:(g + 1) * group]
        ss = jnp.sum(xs * xs, axis=-1, keepdims=True)
        outs.append(xs * lax.rsqrt((ss / group if mean else ss) + EPS))
    return jnp.concatenate(outs, axis=-1)


def _fold_rows(w, rows=8):
    parts = [w[i * rows:(i + 1) * rows, :] for i in range(w.shape[0] // rows)]
    while len(parts) > 1:
        parts = [parts[i] + parts[i + 1] for i in range(0, len(parts), 2)]
    return parts[0]


def _round_robin(gens):
    results = [None] * len(gens)
    live = list(range(len(gens)))
    while live:
        for i in list(live):
            try:
                next(gens[i])
            except StopIteration as stop:
                results[i] = stop.value
                live.remove(i)
    return results


def _const_spec(shape, single_buffer=False):
    zeros = (0,) * len(shape)
    if single_buffer:
        return pl.BlockSpec(shape, lambda *_: zeros, pipeline_mode=pl.Buffered(1))
    return pl.BlockSpec(shape, lambda *_: zeros)


def _inproj_body(x_ref, g1_ref, w_ref, gq_ref, gk_ref, *out_refs, head_major, kt):
    h = (_rms(x_ref[...]) * g1_ref[...]).astype(_MXU_DTYPE)

    def proj(c0, width):
        return jnp.dot(h, w_ref[:, c0:c0 + width], preferred_element_type=F32)

    qn = _group_norm(proj(_C_QA, A_WIDTH), HEAD_DIM, True) * gq_ref[...] * (HEAD_DIM ** -0.5 * _LOG2E)
    kn = _group_norm(proj(_C_KA, A_WIDTH), HEAD_DIM, True) * gk_ref[...]
    v = proj(_C_VA, A_WIDTH)
    qi = proj(_C_QI, IDX_HEADS * IDX_DIM) * (IDX_DIM ** -0.5)
    misc = proj(_C_MISC, _LANES)
    if head_major:
        qt_ref, qit_ref, wit_ref, khm_ref, vt_ref, ki_ref, ka_ref, va_ref, misc_ref, zb_ref, zc_ref = out_refs
        for hd in range(A_HEADS):
            hs = slice(hd * HEAD_DIM, (hd + 1) * HEAD_DIM)
            qt_ref[hd] = qn[:, hs].T.astype(qt_ref.dtype)
            khm_ref[hd] = kn[:, hs].astype(khm_ref.dtype)
            for j in range(v.shape[0] // kt):
                vt_ref[hd, j] = v[j * kt:(j + 1) * kt, hs].T.astype(vt_ref.dtype)
        for hd in range(IDX_HEADS):
            qit_ref[hd] = qi[:, hd * IDX_DIM:(hd + 1) * IDX_DIM].T.astype(qit_ref.dtype)
        wit_ref[...] = misc.T[_M_WI:_M_WI + IDX_HEADS, :]
        ki_ref[...] = misc[:, 0:IDX_DIM].astype(ki_ref.dtype)
    else:
        qa_ref, qi_ref, ka_ref, va_ref, misc_ref, zb_ref, zc_ref = out_refs
        qa_ref[...] = qn.astype(qa_ref.dtype)
        qi_ref[...] = qi.astype(qi_ref.dtype)
    ka_ref[...] = kn
    va_ref[...] = v
    misc_ref[...] = misc
    zb_ref[...] = proj(_C_B, _W_B)
    zc_ref[...] = proj(_C_C, _W_C)


def _inproj(x2d, g1, w_perm, gq, gk):
    n = x2d.shape[0]
    tm = min(_ROW_TILE, n)
    assert n % tm == 0
    row = lambda w: pl.BlockSpec((tm, w), lambda i: (i, 0))
    widths = (A_WIDTH, IDX_HEADS * IDX_DIM, A_WIDTH, A_WIDTH, _LANES, _W_B, _W_C)
    dtypes = (_MXU_DTYPE, _MXU_DTYPE, F32, F32, F32, F32, F32)
    return pl.pallas_call(
        functools.partial(_inproj_body, head_major=False, kt=None),
        grid=(n // tm,),
        in_specs=[row(D_MODEL), _const_spec((1, D_MODEL)), _const_spec((D_MODEL, D_Z), True),
                  _const_spec((1, A_WIDTH)), _const_spec((1, A_WIDTH))],
        out_specs=[row(w) for w in widths],
        out_shape=tuple(jax.ShapeDtypeStruct((n, w), d) for w, d in zip(widths, dtypes)),
        compiler_params=pltpu.CompilerParams(dimension_semantics=("parallel",), vmem_limit_bytes=_VMEM_LIMIT),
        name="inproj",
    )(x2d, g1, w_perm, gq, gk)


def _inproj_head_major(x, g1, w_perm, gq, gk, kt):
    bsz, t, _ = x.shape
    tm = min(_ROW_TILE, t)
    assert t % tm == 0 and tm % kt == 0
    row = lambda w: pl.BlockSpec((None, tm, w), lambda b, i: (b, i, 0))
    rows = lambda w, d: jax.ShapeDtypeStruct((bsz, t, w), d)
    lanes = lambda nh: pl.BlockSpec((None, nh, HEAD_DIM, tm), lambda b, i: (b, 0, 0, i))
    out_specs = [lanes(A_HEADS), lanes(IDX_HEADS), pl.BlockSpec((None, IDX_HEADS, tm), lambda b, i: (b, 0, i)),
                 pl.BlockSpec((None, A_HEADS, tm, HEAD_DIM), lambda b, i: (b, 0, i, 0)),
                 pl.BlockSpec((None, A_HEADS, tm // kt, HEAD_DIM, kt), lambda b, i: (b, 0, i, 0, 0)),
                 row(IDX_DIM), row(A_WIDTH), row(A_WIDTH), row(_LANES), row(_W_B), row(_W_C)]
    out_shape = (jax.ShapeDtypeStruct((bsz, A_HEADS, HEAD_DIM, t), _MXU_DTYPE),
                 jax.ShapeDtypeStruct((bsz, IDX_HEADS, IDX_DIM, t), _MXU_DTYPE),
                 jax.ShapeDtypeStruct((bsz, IDX_HEADS, t), F32),
                 jax.ShapeDtypeStruct((bsz, A_HEADS, t, HEAD_DIM), _MXU_DTYPE),
                 jax.ShapeDtypeStruct((bsz, A_HEADS, t // kt, HEAD_DIM, kt), _MXU_DTYPE),
                 rows(IDX_DIM, _MXU_DTYPE), rows(A_WIDTH, F32), rows(A_WIDTH, F32), rows(_LANES, F32),
                 rows(_W_B, F32), rows(_W_C, F32))
    return pl.pallas_call(
        functools.partial(_inproj_body, head_major=True, kt=kt),
        grid=(bsz, t // tm),
        in_specs=[row(D_MODEL), _const_spec((1, D_MODEL)), _const_spec((D_MODEL, D_Z), True),
                  _const_spec((1, A_WIDTH)), _const_spec((1, A_WIDTH))],
        out_specs=out_specs,
        out_shape=out_shape,
        compiler_params=pltpu.CompilerParams(dimension_semantics=("parallel", "parallel"),
                                             vmem_limit_bytes=_VMEM_LIMIT),
        name="inproj",
    )(x, g1, w_perm, gq, gk)


def _attn_body(q_ref, qi_ref, wi_ref, k_ref, vt_ref, ki_ref, tbl_ref, o_ref, key_ref, half_ref, acc_ref, s_ref, s2_ref, p_ref, p2_ref, *,
               tq, kt, n_tiles, past, n_keys, topk, tbl_off, n_tbl, idx_bits):
    qb = pl.program_id(1)
    q0 = past + qb * tq
    n_kt = jnp.minimum(n_tiles, (q0 + tq + kt - 1) // kt)
    krow = lax.broadcasted_iota(jnp.int32, (kt, tq), 0)
    q_chunk = (q0 + lax.broadcasted_iota(jnp.int32, (1, tq), 1)) // CHUNK
    adm_end = jnp.minimum((q_chunk + 1) * CHUNK, n_keys)
    wi = wi_ref[...] * (IDX_HEADS ** -0.5)

    def score_tile(t, carry):
        off = pl.multiple_of(t * kt, kt)
        ki_t = ki_ref[pl.ds(off, kt), :]
        acc = jnp.zeros((kt, tq), F32)
        for h in range(IDX_HEADS):
            acc = acc + wi[h:h + 1, :] * jnp.maximum(_mm(ki_t, qi_ref[h]), 0.0)
        bits = pltpu.bitcast(acc, jnp.int32)
        bits = jnp.where(bits == _INT_MIN, 0, bits)
        key = jnp.where(bits >= 0, bits, bits ^ jnp.int32(0x7FFFFFFF))
        key = jnp.where(krow < adm_end - off, key, _INT_MIN)
        key_ref[t] = key
        half_ref[t] = (key >> 16).astype(jnp.int16)
        return carry

    lax.fori_loop(0, n_kt, score_tile, 0)

    @pl.when(n_kt % 2 == 1)
    def _():
        key_ref[n_kt] = jnp.full((kt, tq), _INT_MIN, jnp.int32)
        half_ref[n_kt] = jnp.full((kt, tq), -2 ** 15, jnp.int16)

    n_pair = (n_kt + 1) // 2

    def count(*pred_fns):
        def body(j, cnts):
            t = 2 * j
            k0, k1 = key_ref[t], key_ref[t + 1]
            return tuple(cnt + _fold_rows(jnp.where(fn(t, k0), 1.0, 0.0) + jnp.where(fn(t + 1, k1), 1.0, 0.0))
                         for fn, cnt in zip(pred_fns, cnts))
        cnts = lax.fori_loop(0, n_pair, body, tuple(jnp.zeros((8, tq), F32) for _ in pred_fns))
        return tuple(jnp.sum(cnt, axis=0, keepdims=True) for cnt in cnts)

    def count16(pred_fn):
        one, zero = jnp.int16(1), jnp.int16(0)
        def body(j, cnt):
            w = jnp.where(pred_fn(half_ref[2 * j]), one, zero) + jnp.where(pred_fn(half_ref[2 * j + 1]), one, zero)
            return cnt + _fold_rows(w, 16)
        cnt = lax.fori_loop(0, n_pair, body, jnp.zeros((16, tq), jnp.int16))
        return jnp.sum(cnt.astype(jnp.int32), axis=0, keepdims=True)

    def kth_largest16(kth):
        def bit(i, v):
            cand = v + lax.shift_left(jnp.int32(1), 15 - i)
            cand16 = cand.astype(jnp.int16)
            return jnp.where(count16(lambda k: k >= cand16) >= kth, cand, v)
        return lax.fori_loop(0, 16, bit, jnp.full((1, tq), -2 ** 15, jnp.int32))

    thr_hi = kth_largest16(topk)
    thr_hi16 = thr_hi.astype(jnp.int16)
    above = count16(lambda k: k > thr_hi16)

    def low_halves(t, carry):
        key = key_ref[t]
        half_ref[t] = jnp.where((key >> 16) == thr_hi, (key & 0xFFFF) - 2 ** 15, -2 ** 15).astype(jnp.int16)
        return carry

    lax.fori_loop(0, 2 * n_pair, low_halves, 0)
    thr = thr_hi * 2 ** 16 + (kth_largest16(topk - above) + 2 ** 15)
    thr = jnp.maximum(thr, _INT_MIN + 1)

    cnt_gt, cnt_ge = count(lambda t, k: k > thr, lambda t, k: k >= thr)
    need = topk - cnt_gt
    has_excess = jnp.max(jnp.where(cnt_ge > topk, 1.0, 0.0)) > 0.0

    def tie_search():
        def bit(i, last):
            cand = last + lax.shift_left(jnp.int32(1), idx_bits - 1 - i)
            below, = count(lambda t, k: (k == thr) & ((t * kt + krow) < cand))
            return jnp.where(below < need, cand, last)
        return lax.fori_loop(0, idx_bits, bit, jnp.zeros((1, tq), jnp.int32))

    last = lax.cond(has_excess, tie_search, lambda: jnp.full((1, tq), 2 ** 30, jnp.int32))

    acc_ref[...] = jnp.zeros(acc_ref.shape, F32)

    def logits(t, dst_ref):
        off = pl.multiple_of(t * kt, kt)
        for h in range(A_HEADS):
            dst_ref[h] = _mm(k_ref[h, pl.ds(off, kt), :], q_ref[h])

    def softmax_pv(t, src_ref, pr_ref, carry):
        ms, ls = carry
        off = pl.multiple_of(t * kt, kt)
        key = key_ref[t]
        sel = (key > thr) | ((key == thr) & ((off + krow) <= last))
        ti = jnp.clip(t - qb + tbl_off, 0, n_tbl - 1)
        new_ms, new_ls, alphas = [], [], []
        for h in range(A_HEADS):
            s = jnp.where(sel, src_ref[h] + tbl_ref[ti, h], _NEG)
            m_new = jnp.maximum(ms[h], jnp.max(s, axis=0, keepdims=True))
            alpha = jnp.exp2(ms[h] - m_new)
            p = jnp.exp2(s - m_new)
            new_ls.append(alpha * ls[h] + jnp.sum(p, axis=0, keepdims=True))
            new_ms.append(m_new)
            alphas.append(alpha)
            pr_ref[h] = p.astype(pr_ref.dtype)
        for h in range(A_HEADS):
            acc_ref[h] = alphas[h] * acc_ref[h] + jnp.dot(vt_ref[h, t], pr_ref[h], preferred_element_type=F32)
        return tuple(new_ms), tuple(new_ls)

    logits(0, s_ref)

    def attend_pair(j, carry):
        t0 = 2 * j
        logits(t0 + 1, s2_ref)
        carry = softmax_pv(t0, s_ref, p_ref, carry)
        logits(jnp.minimum(t0 + 2, 2 * n_pair - 2), s_ref)
        return softmax_pv(t0 + 1, s2_ref, p2_ref, carry)

    m0 = tuple(jnp.full((1, tq), _NEG, F32) for _ in range(A_HEADS))
    l0 = tuple(jnp.zeros((1, tq), F32) for _ in range(A_HEADS))
    _, ls = lax.fori_loop(0, n_pair, attend_pair, (m0, l0))
    for h in range(A_HEADS):
        o_ref[h] = acc_ref[h] / ls[h]


def _rel_bucket(rel):
    nb = N_BUCKETS // 2
    ret = jnp.where(rel > 0, nb, 0)
    n = jnp.abs(rel)
    max_exact = nb // 2
    nf = jnp.maximum(n, 1).astype(F32)
    large = max_exact + (jnp.log(nf / max_exact) / math.log(MAX_DISTANCE / max_exact)
                         * (nb - max_exact)).astype(jnp.int32)
    large = jnp.minimum(large, nb - 1)
    return ret + jnp.where(n < max_exact, n, large)


def _attn_tiling(t, n_keys):
    kt = 256
    tq = min(kt, t)
    assert t % tq == 0
    return tq, kt, t // tq, 2 * -(-n_keys // (2 * kt))


def _bias_tables(rel_bias, t, n_keys, past):
    tq, kt, nqb, n_tiles = _attn_tiling(t, n_keys)
    j = jnp.arange(kt)[:, None]
    i = jnp.arange(tq)[None, :]
    if nqb == 1:
        rels = [tile * kt + j - (past + i) for tile in range(n_tiles)]
        off = 0
    else:
        assert past == 0 and tq == kt and kt >= MAX_DISTANCE
        rels = [j - i - 2 * kt, j - i - kt, j - i]
        off = 2
    onehot = jax.nn.one_hot(_rel_bucket(jnp.stack(rels, axis=0)), N_BUCKETS, dtype=F32)
    tbl = jnp.einsum("nktb,bh->nhkt", onehot, rel_bias.astype(F32) * _LOG2E, precision=_HI)
    return tbl, off


def _attn_head_major(q_t, qi_t, wi_t, k_hm, vt_hm, ki_all, tbl, tbl_off, past, n_keys):
    bsz, _, _, t = q_t.shape
    topk = min(TOPK_MAX, n_keys // 4)
    tq, kt, nqb, n_tiles = _attn_tiling(t, n_keys)
    s_pad = n_tiles * kt
    assert s_pad >= topk and k_hm.shape[2] == s_pad and vt_hm.shape[2] == n_tiles and ki_all.shape[1] == s_pad
    n_tbl = tbl.shape[0]
    body = functools.partial(_attn_body, tq=tq, kt=kt, n_tiles=n_tiles, past=past, n_keys=n_keys, topk=topk,
                             tbl_off=tbl_off, n_tbl=n_tbl, idx_bits=int(s_pad).bit_length())
    return pl.pallas_call(
        body,
        grid=(bsz, nqb),
        in_specs=[pl.BlockSpec((None, A_HEADS, HEAD_DIM, tq), lambda b, i: (b, 0, 0, i)),
                  pl.BlockSpec((None, IDX_HEADS, IDX_DIM, tq), lambda b, i: (b, 0, 0, i)),
                  pl.BlockSpec((None, IDX_HEADS, tq), lambda b, i: (b, 0, i)),
                  pl.BlockSpec((None, A_HEADS, s_pad, HEAD_DIM), lambda b, i: (b, 0, 0, 0)),
                  pl.BlockSpec((None, A_HEADS, n_tiles, HEAD_DIM, kt), lambda b, i: (b, 0, 0, 0, 0)),
                  pl.BlockSpec((None, s_pad, IDX_DIM), lambda b, i: (b, 0, 0)),
                  _const_spec((n_tbl, A_HEADS, kt, tq), True)],
        out_specs=pl.BlockSpec((None, A_HEADS, HEAD_DIM, tq), lambda b, i: (b, 0, 0, i)),
        out_shape=jax.ShapeDtypeStruct((bsz, A_HEADS, HEAD_DIM, t), F32),
        scratch_shapes=[pltpu.VMEM((n_tiles, kt, tq), jnp.int32), pltpu.VMEM((n_tiles, kt, tq), jnp.int16),
                        pltpu.VMEM((A_HEADS, HEAD_DIM, tq), F32),
                        pltpu.VMEM((A_HEADS, kt, tq), F32), pltpu.VMEM((A_HEADS, kt, tq), F32),
                        pltpu.VMEM((A_HEADS, kt, tq), _MXU_DTYPE), pltpu.VMEM((A_HEADS, kt, tq), _MXU_DTYPE)],
        compiler_params=pltpu.CompilerParams(dimension_semantics=("parallel", "parallel"),
                                             vmem_limit_bytes=_VMEM_LIMIT),
        name="attn",
    )(q_t, qi_t, wi_t, k_hm, vt_hm, ki_all, tbl)


def _attn(q, qi, wi, k_all, v_all, ki_all, tbl, tbl_off, past):
    bsz, t, _ = q.shape
    n_keys = k_all.shape[1]
    _, kt, _, n_tiles = _attn_tiling(t, n_keys)
    s_pad = n_tiles * kt
    if s_pad != n_keys:
        k_all, v_all, ki_all = (jnp.pad(a, ((0, 0), (0, s_pad - n_keys), (0, 0))) for a in (k_all, v_all, ki_all))
    q_t = q.reshape(bsz, t, A_HEADS, HEAD_DIM).transpose(0, 2, 3, 1)
    qi_t = qi.reshape(bsz, t, IDX_HEADS, IDX_DIM).transpose(0, 2, 3, 1)
    k_hm = k_all.reshape(bsz, s_pad, A_HEADS, HEAD_DIM).transpose(0, 2, 1, 3)
    vt_hm = v_all.reshape(bsz, n_tiles, kt, A_HEADS, HEAD_DIM).transpose(0, 3, 1, 4, 2)
    o_t = _attn_head_major(q_t, qi_t, wi.transpose(0, 2, 1), k_hm, vt_hm, ki_all, tbl, tbl_off, past, n_keys)
    return o_t.transpose(0, 3, 1, 2).reshape(bsz, t, A_WIDTH)


def _gla_chunk(zb, misc, wg, bg, og, st):
    c = zb.shape[0]
    dkw = B_HEADS * B_DK
    q = zb[:, 0:dkw] * (B_DK ** -0.5)
    k = zb[:, dkw:2 * dkw]
    v3 = _heads3(zb[:, 2 * dkw:2 * dkw + B_WIDTH], B_HEADS, B_DV)
    gate = zb[:, 2 * dkw + B_WIDTH:]
    log_a = jax.nn.log_sigmoid(_mmx(misc, wg) + bg) / GATE_TAU
    b = _cumsum_rows(log_a)
    b_last = b[c - 1:c, :]
    b_mid = b[c // 2 - 1:c // 2, :]
    att = _bmm_nt(_heads3(q * jnp.exp(b - b_mid), B_HEADS, B_DK), _heads3(k * jnp.exp(b_mid - b), B_HEADS, B_DK))
    ri = lax.broadcasted_iota(jnp.int32, att.shape, 1)
    cj = lax.broadcasted_iota(jnp.int32, att.shape, 2)
    att = jnp.where(ri >= cj, att, 0.0)
    o3 = _bmm_nt(_heads3(q * jnp.exp(b), B_HEADS, B_DK), st) + _bmm(att, v3)
    upd = _bmm_tn(v3, _heads3(k * jnp.exp(b_last - b), B_HEADS, B_DK))
    st_new = st * _heads3(jnp.exp(b_last), B_HEADS, B_DK) + upd
    o = _lanes2(o3 * lax.rsqrt(jnp.mean(o3 * o3, axis=-1, keepdims=True) + EPS))
    return o * og * jax.nn.silu(gate), st_new


def _gla_body(zb_ref, misc_ref, wg_ref, bg_ref, og_ref, s0_ref, o_ref, s_out_ref, s_ref):
    @pl.when(pl.program_id(1) == 0)
    def _():
        s_ref[...] = s0_ref[...]

    for g in range(zb_ref.shape[0]):
        o, st = _gla_chunk(zb_ref[g], misc_ref[g], wg_ref[...], bg_ref[...], og_ref[...], s_ref[g])
        o_ref[g] = o
        s_ref[g] = st
        s_out_ref[g] = st


def _gla(zb, misc, wg_emb, bg, og, s0_t, c):
    bsz, t, _ = zb.shape
    dkw = B_HEADS * B_DK
    g = math.gcd(_SEQS_PER_STEP, bsz)
    blk = lambda w: pl.BlockSpec((g, c, w), lambda b, i: (b, i, 0))
    st = pl.BlockSpec((g, B_HEADS, B_DV, B_DK), lambda b, i: (b, 0, 0, 0))
    return pl.pallas_call(
        _gla_body,
        grid=(bsz // g, t // c),
        in_specs=[blk(_W_B), blk(_LANES), _const_spec((_LANES, dkw)), _const_spec((1, dkw)),
                  _const_spec((1, B_WIDTH)), st],
        out_specs=[blk(B_WIDTH), st],
        out_shape=(jax.ShapeDtypeStruct((bsz, t, B_WIDTH), F32),
                   jax.ShapeDtypeStruct((bsz, B_HEADS, B_DV, B_DK), F32)),
        scratch_shapes=[pltpu.VMEM((g, B_HEADS, B_DV, B_DK), F32)],
        compiler_params=pltpu.CompilerParams(dimension_semantics=("parallel", "arbitrary"),
                                             vmem_limit_bytes=_VMEM_LIMIT),
        name="gla",
    )(zb, misc, wg_emb, bg, og, s0_t)


_PAIRS = C_HEADS // 2


def _pairs3(x):
    return jnp.stack([x[:, p * _LANES:(p + 1) * _LANES] for p in range(_PAIRS)], axis=0)


def _pair_diag(x3):
    left = lax.broadcasted_iota(jnp.int32, x3.shape, 2) < x3.shape[2] // 2
    zero = jnp.zeros((), x3.dtype)
    return jnp.concatenate([jnp.where(left, x3, zero), jnp.where(left, zero, x3)], axis=1)


def _pair_diag_wide(x3):
    left = lax.broadcasted_iota(jnp.int32, x3.shape, 2) % _LANES < _LANES // 2
    zero = jnp.zeros((), x3.dtype)
    return jnp.concatenate([jnp.where(left, x3, zero), jnp.where(left, zero, x3)], axis=1)


def _pmm(a, b):
    return jnp.einsum("pij,pjk->pik", a, b, preferred_element_type=F32)


def _pmm3(a, b):
    (ah, al), (bh, bl) = a, b
    n = ah.shape[1]
    both = _pmm(jnp.concatenate([ah, al], axis=1), _pair_diag(bh))
    return both[:, 0:n, :] + (both[:, n:, :] + _pmm(ah, _pair_diag(bl)))


def _gdn_chunk(zc_ref, misc, cw_ref, alog, dtb, og, xpad_ref, s_bd):
    c = misc.shape[0]
    w3 = 3 * C_WIDTH
    bf = lambda x: x.astype(_MXU_DTYPE)
    xpad_ref[8:8 + c, :] = zc_ref[:, 0:w3]
    conv = xpad_ref[5:5 + c, :] * cw_ref[0:1, :]
    for i in range(1, CONV_W):
        conv = conv + xpad_ref[5 + i:5 + i + c, :] * cw_ref[i:i + 1, :]
    xpad_ref[0:8, :] = xpad_ref[c:c + 8, :]
    act = jax.nn.silu(conv)
    q2 = _group_norm(act[:, 0:C_WIDTH], C_DK, False) * (C_DK ** -0.5)
    k2 = _group_norm(act[:, C_WIDTH:2 * C_WIDTH], C_DK, False)
    v2 = act[:, 2 * C_WIDTH:w3]

    g_full = -jnp.exp(alog) * jax.nn.softplus(misc + dtb)
    beta_full = jax.nn.sigmoid(misc)
    gcum = _cumsum_rows(g_full)
    gexp2 = _expand_heads(gcum, _M_CA, C_HEADS, C_DV)
    bexp2 = _expand_heads(beta_full, _M_CB, C_HEADS, C_DV)
    g_last2 = gexp2[c - 1:c, :]
    eg2 = jnp.exp(gexp2)

    gcum_t = gcum.T
    col = lambda h: jnp.broadcast_to(gcum[:, _M_CA + h:_M_CA + h + 1], (c, c))
    row = lambda h: jnp.broadcast_to(gcum_t[_M_CA + h:_M_CA + h + 1, :], (c, c))
    diff = jnp.stack([jnp.concatenate([col(2 * p) - row(2 * p), col(2 * p + 1) - row(2 * p + 1)], axis=-1)
                      for p in range(_PAIRS)], axis=0)
    ri = lax.broadcasted_iota(jnp.int32, diff.shape, 1)
    cj = lax.broadcasted_iota(jnp.int32, diff.shape, 2) % c
    lmask = jnp.where(ri >= cj, jnp.exp(jnp.minimum(diff, 0.0)), 0.0)

    kb2 = k2 * bexp2
    k_bd = _pair_diag(bf(_pairs3(k2)))
    kq = jnp.einsum("pil,pjl->pij", bf(_pairs3(jnp.concatenate([kb2, q2], axis=0))), k_bd,
                    preferred_element_type=F32)
    yield
    m = jnp.where(ri > cj, kq[:, 0:c, :] * lmask, 0.0)
    qk = kq[:, c:, :] * lmask
    inv = jnp.where(ri == cj, 1.0, 0.0) - m
    pw = _split_bf16(m)
    for _ in range(int(math.log2(c)) - 1):
        sq = _pmm3(pw, pw)
        yield
        pw = _split_bf16(sq)
        step = _pmm3(_split_bf16(inv), pw)
        yield
        inv = inv + step
    rhs = jnp.concatenate([_pairs3(v2 * bexp2), _pairs3(kb2 * eg2)], axis=-1)
    uw = _pmm(bf(inv), _pair_diag_wide(bf(rhs)))
    yield
    u, w = uw[:, :, 0:_LANES], uw[:, :, _LANES:]
    wq = _pmm(bf(jnp.concatenate([w, _pairs3(q2 * eg2)], axis=1)), bf(s_bd))
    yield
    v_new = u - wq[:, 0:c, :]
    o3 = wq[:, c:, :] + _pmm(bf(qk), _pair_diag(bf(v_new)))
    upd = jnp.einsum("pck,pcv->pkv", bf(_pairs3(k2 * jnp.exp(g_last2 - gexp2))), bf(v_new),
                     preferred_element_type=F32)
    yield
    same_head = (lax.broadcasted_iota(jnp.int32, upd.shape, 1) // C_DK) == (lax.broadcasted_iota(jnp.int32, upd.shape, 2) // C_DV)
    s_new = s_bd * _pairs3(jnp.exp(g_last2)) + jnp.where(same_head, upd, 0.0)
    o2 = jnp.concatenate([o3[p] for p in range(_PAIRS)], axis=-1)
    return _group_norm(o2, C_DV, True) * og * jax.nn.silu(zc_ref[:, w3:]), s_new


def _gdn_body(zc_ref, misc_ref, cw_ref, alog_ref, dtb_ref, og_ref, s0_ref, hist_ref,
              o_ref, s_out_ref, xpad_ref, s_ref):
    n_seq = zc_ref.shape[0]

    @pl.when(pl.program_id(1) == 0)
    def _():
        xpad_ref[:, 0:8, :] = hist_ref[...]
        zero = jnp.zeros((C_DK, C_DV), F32)
        for g in range(n_seq):
            for p in range(_PAIRS):
                s_ref[g, p] = jnp.concatenate([jnp.concatenate([s0_ref[g, 2 * p], zero], axis=-1),
                                               jnp.concatenate([zero, s0_ref[g, 2 * p + 1]], axis=-1)], axis=0)

    chunks = [_gdn_chunk(zc_ref.at[g], misc_ref[g], cw_ref, alog_ref[...], dtb_ref[...], og_ref[...],
                         xpad_ref.at[g], s_ref[g]) for g in range(n_seq)]
    for g, (o, s_new) in enumerate(_round_robin(chunks)):
        o_ref[g] = o
        s_ref[g] = s_new

    @pl.when(pl.program_id(1) == pl.num_programs(1) - 1)
    def _():
        for g in range(n_seq):
            for p in range(_PAIRS):
                s_out_ref[g, 2 * p] = s_ref[g, p, 0:C_DK, 0:C_DV]
                s_out_ref[g, 2 * p + 1] = s_ref[g, p, C_DK:, C_DV:]


def _gdn(zc, misc, conv_w, alog_emb, dtb_emb, og, s0, hist8, c):
    bsz, t, _ = zc.shape
    w3 = 3 * C_WIDTH
    g = math.gcd(_SEQS_PER_STEP, bsz)
    blk = lambda w: pl.BlockSpec((g, c, w), lambda b, i: (b, i, 0))
    st = pl.BlockSpec((g, C_HEADS, C_DK, C_DV), lambda b, i: (b, 0, 0, 0))
    return pl.pallas_call(
        _gdn_body,
        grid=(bsz // g, t // c),
        in_specs=[blk(_W_C), blk(_LANES), _const_spec((CONV_W, w3)), _const_spec((1, _LANES)),
                  _const_spec((1, _LANES)), _const_spec((1, C_WIDTH)), st,
                  pl.BlockSpec((g, 8, w3), lambda b, i: (b, 0, 0))],
        out_specs=[blk(C_WIDTH), st],
        out_shape=(jax.ShapeDtypeStruct((bsz, t, C_WIDTH), F32),
                   jax.ShapeDtypeStruct((bsz, C_HEADS, C_DK, C_DV), F32)),
        scratch_shapes=[pltpu.VMEM((g, 8 + c, w3), F32), pltpu.VMEM((g, _PAIRS, 2 * C_DK, 2 * C_DV), F32)],
        compiler_params=pltpu.CompilerParams(dimension_semantics=("parallel", "arbitrary"),
                                             vmem_limit_bytes=_VMEM_LIMIT),
        name="gdn",
    )(zc, misc, conv_w, alog_emb, dtb_emb, og, s0, hist8)


def _mlp_body(x_ref, oa_ref, ob_ref, oc_ref, wo_ref, g2_ref, wup_ref, wdn_ref, y_ref, *, ff_tile, head_major):
    if head_major:
        oa = jnp.concatenate([oa_ref[h].T for h in range(A_HEADS)], axis=-1)
    else:
        oa = oa_ref[...]
    mixed = jnp.concatenate([oa, ob_ref[...], oc_ref[...]], axis=-1)
    y_ref[...] = x_ref[...] + _mm(mixed, wo_ref[...])
    h2 = (_rms(y_ref[...]) * g2_ref[...]).astype(_MXU_DTYPE)
    for j in range(D_FF // ff_tile):
        up = jnp.dot(h2, wup_ref[:, j * ff_tile:(j + 1) * ff_tile], preferred_element_type=F32)
        y_ref[...] += _mm(jnp.square(jnp.maximum(up, 0.0)), wdn_ref[j * ff_tile:(j + 1) * ff_tile, :])


_MLP_WEIGHT_SPECS = lambda: [_const_spec((D_MODEL, D_MODEL), True), _const_spec((1, D_MODEL)),
                             _const_spec((D_MODEL, D_FF), True), _const_spec((D_FF, D_MODEL), True)]


def _mlp(x2d, oa, ob, oc, wo, g2, wup, wdn):
    n = x2d.shape[0]
    tm = min(_ROW_TILE, n)
    assert n % tm == 0
    row = lambda w: pl.BlockSpec((tm, w), lambda i: (i, 0))
    return pl.pallas_call(
        functools.partial(_mlp_body, ff_tile=1024, head_major=False),
        grid=(n // tm,),
        in_specs=[row(D_MODEL), row(A_WIDTH), row(B_WIDTH), row(C_WIDTH)] + _MLP_WEIGHT_SPECS(),
        out_specs=row(D_MODEL),
        out_shape=jax.ShapeDtypeStruct((n, D_MODEL), F32),
        compiler_params=pltpu.CompilerParams(dimension_semantics=("parallel",), vmem_limit_bytes=_VMEM_LIMIT),
        name="mlp",
    )(x2d, oa, ob, oc, wo, g2, wup, wdn)


def _mlp_head_major(x, oa_t, ob, oc, wo, g2, wup, wdn):
    bsz, t, _ = x.shape
    tm = min(_ROW_TILE, t)
    assert t % tm == 0
    row = lambda w: pl.BlockSpec((None, tm, w), lambda b, i: (b, i, 0))
    return pl.pallas_call(
        functools.partial(_mlp_body, ff_tile=1024, head_major=True),
        grid=(bsz, t // tm),
        in_specs=[row(D_MODEL), pl.BlockSpec((None, A_HEADS, HEAD_DIM, tm), lambda b, i: (b, 0, 0, i)),
                  row(B_WIDTH), row(C_WIDTH)] + _MLP_WEIGHT_SPECS(),
        out_specs=row(D_MODEL),
        out_shape=jax.ShapeDtypeStruct((bsz, t, D_MODEL), F32),
        compiler_params=pltpu.CompilerParams(dimension_semantics=("parallel", "parallel"),
                                             vmem_limit_bytes=_VMEM_LIMIT),
        name="mlp",
    )(x, oa_t, ob, oc, wo, g2, wup, wdn)


def _prep_layer_params(ln1_g, w_in, a_qnorm_g, a_knorm_g, gla_w_gate, gla_b_gate, gla_onorm_g,
                       conv_w, gdn_a_log, gdn_dt_bias, gdn_onorm_g, w_o, ln2_g, w_up, w_down):
    cols = [jnp.zeros((D_MODEL, _PAD[name]), w_in.dtype) if name in _PAD
            else w_in[:, _SRC[name]:_SRC[name] + _SIZE[name]] for name in _NEW_ORDER]
    lane_emb = lambda vec, start: jnp.zeros((1, _LANES), F32).at[0, start:start + vec.shape[0]].set(vec.astype(F32))
    return dict(
        g1=ln1_g.reshape(1, D_MODEL).astype(F32),
        w_perm=jnp.concatenate(cols, axis=1).astype(_MXU_DTYPE),
        gq=jnp.tile(a_qnorm_g.astype(F32), A_HEADS).reshape(1, A_WIDTH),
        gk=jnp.tile(a_knorm_g.astype(F32), A_HEADS).reshape(1, A_WIDTH),
        wg_emb=jnp.zeros((_LANES, B_HEADS * B_DK), F32).at[_M_GLR:_M_GLR + GATE_RANK].set(gla_w_gate.astype(F32)),
        bg=gla_b_gate.reshape(1, -1).astype(F32),
        og_b=jnp.tile(gla_onorm_g.astype(F32), B_HEADS).reshape(1, B_WIDTH),
        conv_w=conv_w.astype(F32),
        alog_emb=lane_emb(gdn_a_log, _M_CA),
        dtb_emb=lane_emb(gdn_dt_bias, _M_CA),
        og_c=jnp.tile(gdn_onorm_g.astype(F32), C_HEADS).reshape(1, C_WIDTH),
        wo=w_o.astype(_MXU_DTYPE), g2=ln2_g.reshape(1, D_MODEL).astype(F32),
        wup=w_up.astype(_MXU_DTYPE), wdn=w_down.astype(_MXU_DTYPE),
    )


def _layer(x, cache, p, tbl, tbl_off):
    bsz, t, _ = x.shape
    n = bsz * t
    assert t >= CONV_W - 1
    c = min(CHUNK, t)
    r3 = lambda a: a.reshape(bsz, t, a.shape[-1])
    mixer_b = lambda zb, misc, s0: _gla(zb, misc, p["wg_emb"], p["bg"], p["og_b"], s0, c)
    mixer_c = lambda zc, misc, s0, hist8: _gdn(zc, misc, p["conv_w"], p["alog_emb"], p["dtb_emb"], p["og_c"], s0, hist8, c)

    if cache is None:
        _, kt, _, n_tiles = _attn_tiling(t, t)
        assert n_tiles * kt == t
        q_t, qi_t, wi_t, k_hm, vt_hm, ki16, ka, va, misc, zb, zc = _inproj_head_major(
            x, p["g1"], p["w_perm"], p["gq"], p["gk"], kt)
        o_a_t = _attn_head_major(q_t, qi_t, wi_t, k_hm, vt_hm, ki16, tbl, tbl_off, 0, t)
        o_b, s_gla_t = mixer_b(zb, misc, jnp.zeros((bsz, B_HEADS, B_DV, B_DK), F32))
        o_c, s_gdn = mixer_c(zc, misc, jnp.zeros((bsz, C_HEADS, C_DK, C_DV), F32), jnp.zeros((bsz, 8, 3 * C_WIDTH), F32))
        y = _mlp_head_major(x, o_a_t, o_b, o_c, p["wo"], p["g2"], p["wup"], p["wdn"])
    else:
        ck, cv, cki, sg, sd, cbuf = cache
        past = ck.shape[1]
        x2d = x.reshape(n, D_MODEL)
        qa, qi, ka, va, misc, zb, zc = _inproj(x2d, p["g1"], p["w_perm"], p["gq"], p["gk"])
        ka, va, misc, zb, zc = r3(ka), r3(va), r3(misc), r3(zb), r3(zc)
        k_all = jnp.concatenate([ck.reshape(bsz, past, A_WIDTH).astype(F32), ka], axis=1)
        v_all = jnp.concatenate([cv.reshape(bsz, past, A_WIDTH).astype(F32), va], axis=1)
        ki_all = jnp.concatenate([cki.astype(F32), misc[..., 0:IDX_DIM]], axis=1)
        o_a = _attn(r3(qa), r3(qi), misc[..., _M_WI:_M_WI + IDX_HEADS], k_all.astype(_MXU_DTYPE),
                    v_all.astype(_MXU_DTYPE), ki_all.astype(_MXU_DTYPE), tbl, tbl_off, past)
        o_b, s_gla_t = mixer_b(zb, misc, jnp.swapaxes(sg.astype(F32), 2, 3))
        hist8 = jnp.pad(cbuf.astype(F32), ((0, 0), (8 - (CONV_W - 1), 0), (0, 0)))
        o_c, s_gdn = mixer_c(zc, misc, sd.astype(F32), hist8)
        y = _mlp(x2d, o_a.reshape(n, A_WIDTH), o_b.reshape(n, B_WIDTH), o_c.reshape(n, C_WIDTH),
                 p["wo"], p["g2"], p["wup"], p["wdn"]).reshape(bsz, t, D_MODEL)

    s_gla = jnp.swapaxes(s_gla_t, 2, 3)
    new_conv = zc[:, t - (CONV_W - 1):, 0:3 * C_WIDTH]
    state = (ka.reshape(bsz, t, A_HEADS, HEAD_DIM), va.reshape(bsz, t, A_HEADS, HEAD_DIM), misc[..., 0:IDX_DIM],
             s_gla, s_gdn, new_conv)
    return y, state


def kernel(x_prompt, x_sample, cache_a_k, cache_a_v, cache_a_kidx, state_gla, state_gdn, state_conv, rel_bias, ln1_g, w_in, a_qnorm_g, a_knorm_g, gla_w_gate, gla_b_gate, gla_onorm_g, conv_w, gdn_a_log, gdn_dt_bias, gdn_onorm_g, w_o, ln2_g, w_up, w_down):
    depth = ln1_g.shape[0]
    yp, ys = x_prompt, x_sample
    past = cache_a_k.shape[2]
    tbl_p = _bias_tables(rel_bias, x_prompt.shape[1], x_prompt.shape[1], 0)
    tbl_s = _bias_tables(rel_bias, x_sample.shape[1], past + x_sample.shape[1], past)
    new_p, new_s = [], []
    for l in range(depth):
        p = _prep_layer_params(ln1_g[l], w_in[l], a_qnorm_g[l], a_knorm_g[l], gla_w_gate[l],
                               gla_b_gate[l], gla_onorm_g[l], conv_w[l], gdn_a_log[l], gdn_dt_bias[l],
                               gdn_onorm_g[l], w_o[l], ln2_g[l], w_up[l], w_down[l])
        yp, st_p = _layer(yp, None, p, *tbl_p)
        cache_l = (cache_a_k[l], cache_a_v[l], cache_a_kidx[l], state_gla[l], state_gdn[l], state_conv[l])
        ys, st_s = _layer(ys, cache_l, p, *tbl_s)
        new_p.append(st_p)
        new_s.append(st_s)
    stack = lambda states, i: jnp.stack([s[i] for s in states], axis=0)
    return (yp, ys) + tuple(stack(new_p, i) for i in range(6)) + tuple(stack(new_s, i) for i in range(6))
```

```python
import functools
import math

import numpy as np
import jax
import jax.numpy as jnp
from jax import lax
from jax.experimental import pallas as pl
from jax.experimental.pallas import tpu as pltpu

D_MODEL = 1024
CHUNK = 64
HEAD_DIM = 64
A_HEADS = 4
A_WIDTH = A_HEADS * HEAD_DIM
IDX_HEADS = 8
IDX_DIM = 64
TOPK_MAX = 256
N_BUCKETS = 32
MAX_DISTANCE = 128
B_HEADS = 4
B_DK = 32
B_DV = 64
B_WIDTH = B_HEADS * B_DV
GATE_RANK = 16
GATE_TAU = 16.0
C_HEADS = 8
C_DK = 64
C_DV = 64
C_WIDTH = C_HEADS * C_DV
CONV_W = 4
D_FF = 4 * D_MODEL
EPS = 1e-6

F32 = jnp.float32
BF16 = jnp.bfloat16
_MXU_DTYPE = BF16
_HI = lax.Precision.HIGHEST
_VMEM_LIMIT = 56 * 1024 * 1024
_LANES = 128
_ROW_TILE = 512
_SEQS_PER_STEP = 4
_NEG = -1e30
_LOG2E = math.log2(math.e)
_INT_MIN = -2 ** 31

_SPLIT_NAMES = ("aq", "ak", "av", "aqi", "aki", "awi", "bq", "bk", "bv", "bglr", "bog",
                "cq", "ck", "cv", "ca", "cb", "cog")
_SPLIT_SIZES = (A_WIDTH, A_WIDTH, A_WIDTH, IDX_HEADS * IDX_DIM, IDX_DIM, IDX_HEADS,
                B_HEADS * B_DK, B_HEADS * B_DK, B_WIDTH, GATE_RANK, B_WIDTH,
                C_HEADS * C_DK, C_HEADS * C_DK, C_WIDTH, C_HEADS, C_HEADS, C_WIDTH)
D_IN = sum(_SPLIT_SIZES)
_SRC = dict(zip(_SPLIT_NAMES, np.concatenate([[0], np.cumsum(_SPLIT_SIZES)[:-1]]).tolist()))
_SIZE = dict(zip(_SPLIT_NAMES, _SPLIT_SIZES))

_M_WI, _M_GLR, _M_CA, _M_CB = 64, 72, 88, 96
_NEW_ORDER = ("aq", "ak", "av", "aqi", "aki", "awi", "bglr", "ca", "cb", "pad24",
              "bq", "bk", "bv", "bog", "cq", "ck", "cv", "cog")
_PAD = {"pad24": 24}


def _new_offsets():
    offs, pos = {}, 0
    for name in _NEW_ORDER:
        offs[name] = pos
        pos += _PAD.get(name, 0) or _SIZE[name]
    return offs, pos


_OFF, D_Z = _new_offsets()
_C_QA, _C_KA, _C_VA, _C_QI, _C_MISC = _OFF["aq"], _OFF["ak"], _OFF["av"], _OFF["aqi"], _OFF["aki"]
_C_B, _C_C = _OFF["bq"], _OFF["cq"]
_W_B = 2 * B_HEADS * B_DK + 2 * B_WIDTH
_W_C = 4 * C_WIDTH
assert _C_MISC % _LANES == 0 and _C_B == _C_MISC + _LANES and _C_C == _C_B + _W_B and D_Z == _C_C + _W_C
assert (_OFF["awi"], _OFF["bglr"], _OFF["ca"], _OFF["cb"]) == tuple(_C_MISC + m for m in (_M_WI, _M_GLR, _M_CA, _M_CB))


def _mm(a, b):
    return jnp.dot(a.astype(_MXU_DTYPE), b.astype(_MXU_DTYPE), preferred_element_type=F32)


def _mmx(a, b):
    return jnp.dot(a, b, preferred_element_type=F32, precision=_HI)


def _split_bf16(x):
    hi = x.astype(BF16)
    return hi, (x - hi.astype(F32)).astype(BF16)


def _cumsum_rows(x):
    r = lax.broadcasted_iota(jnp.int32, x.shape, 0)
    s = 1
    while s < x.shape[0]:
        x = x + jnp.where(r >= s, pltpu.roll(x, s, 0), 0.0)
        s *= 2
    return x


def _heads3(x, n_heads, width):
    return jnp.stack([x[:, h * width:(h + 1) * width] for h in range(n_heads)], axis=0)


def _lanes2(x3):
    return jnp.concatenate([x3[h] for h in range(x3.shape[0])], axis=-1)


def _expand_heads(x, lane0, n_heads, width):
    c = x.shape[0]
    return jnp.concatenate([jnp.broadcast_to(x[:, lane0 + h:lane0 + h + 1], (c, width)) for h in range(n_heads)],
                           axis=-1)


def _bmm(a, b):
    return jnp.einsum("hij,hjk->hik", a.astype(_MXU_DTYPE), b.astype(_MXU_DTYPE), preferred_element_type=F32)


def _bmm_nt(a, b):
    return jnp.einsum("hik,hjk->hij", a.astype(_MXU_DTYPE), b.astype(_MXU_DTYPE), preferred_element_type=F32)


def _bmm_tn(a, b):
    return jnp.einsum("hck,hcv->hkv", a.astype(_MXU_DTYPE), b.astype(_MXU_DTYPE), preferred_element_type=F32)


def _rms(x):
    return x * lax.rsqrt(jnp.mean(x * x, axis=-1, keepdims=True) + EPS)


def _group_norm(x, group, mean):
    outs = []
    for g in range(x.shape[-1] // group):
        xs = x[:, g * group:(g + 1) * group]
        ss = jnp.sum(xs * xs, axis=-1, keepdims=True)
        outs.append(xs * lax.rsqrt((ss / group if mean else ss) + EPS))
    return jnp.concatenate(outs, axis=-1)


def _fold_rows(w, rows=8):
    parts = [w[i * rows:(i + 1) * rows, :] for i in range(w.shape[0] // rows)]
    while len(parts) > 1:
        parts = [parts[i] + parts[i + 1] for i in range(0, len(parts), 2)]
    return parts[0]


def _round_robin(gens):
    results = [None] * len(gens)
    live = list(range(len(gens)))
    while live:
        for i in list(live):
            try:
                next(gens[i])
            except StopIteration as stop:
                results[i] = stop.value
                live.remove(i)
    return results


def _const_spec(shape, single_buffer=False):
    zeros = (0,) * len(shape)
    if single_buffer:
        return pl.BlockSpec(shape, lambda *_: zeros, pipeline_mode=pl.Buffered(1))
    return pl.BlockSpec(shape, lambda *_: zeros)


def _inproj_body(x_ref, g1_ref, w_ref, gq_ref, gk_ref, *out_refs, head_major, kt):
    h = (_rms(x_ref[...]) * g1_ref[...]).astype(_MXU_DTYPE)

    def proj(c0, width):
        return jnp.dot(h, w_ref[:, c0:c0 + width], preferred_element_type=F32)

    qn = _group_norm(proj(_C_QA, A_WIDTH), HEAD_DIM, True) * gq_ref[...] * (HEAD_DIM ** -0.5 * _LOG2E)
    kn = _group_norm(proj(_C_KA, A_WIDTH), HEAD_DIM, True) * gk_ref[...]
    v = proj(_C_VA, A_WIDTH)
    qi = proj(_C_QI, IDX_HEADS * IDX_DIM) * (IDX_DIM ** -0.5)
    misc = proj(_C_MISC, _LANES)
    if head_major:
        qt_ref, qit_ref, wit_ref, khm_ref, vt_ref, ki_ref, ka_ref, va_ref, misc_ref, zb_ref, zc_ref = out_refs
        for hd in range(A_HEADS):
            hs = slice(hd * HEAD_DIM, (hd + 1) * HEAD_DIM)
            qt_ref[hd] = qn[:, hs].T.astype(qt_ref.dtype)
            khm_ref[hd] = kn[:, hs].astype(khm_ref.dtype)
            for j in range(v.shape[0] // kt):
                vt_ref[hd, j] = v[j * kt:(j + 1) * kt, hs].T.astype(vt_ref.dtype)
        for hd in range(IDX_HEADS):
            qit_ref[hd] = qi[:, hd * IDX_DIM:(hd + 1) * IDX_DIM].T.astype(qit_ref.dtype)
        wit_ref[...] = misc.T[_M_WI:_M_WI + IDX_HEADS, :]
        ki_ref[...] = misc[:, 0:IDX_DIM].astype(ki_ref.dtype)
    else:
        qa_ref, qi_ref, ka_ref, va_ref, misc_ref, zb_ref, zc_ref = out_refs
        qa_ref[...] = qn.astype(qa_ref.dtype)
        qi_ref[...] = qi.astype(qi_ref.dtype)
    ka_ref[...] = kn
    va_ref[...] = v
    misc_ref[...] = misc
    zb_ref[...] = proj(_C_B, _W_B)
    zc_ref[...] = proj(_C_C, _W_C)


def _inproj(x2d, g1, w_perm, gq, gk):
    n = x2d.shape[0]
    tm = min(_ROW_TILE, n)
    assert n % tm == 0
    row = lambda w: pl.BlockSpec((tm, w), lambda i: (i, 0))
    widths = (A_WIDTH, IDX_HEADS * IDX_DIM, A_WIDTH, A_WIDTH, _LANES, _W_B, _W_C)
    dtypes = (_MXU_DTYPE, _MXU_DTYPE, F32, F32, F32, F32, F32)
    return pl.pallas_call(
        functools.partial(_inproj_body, head_major=False, kt=None),
        grid=(n // tm,),
        in_specs=[row(D_MODEL), _const_spec((1, D_MODEL)), _const_spec((D_MODEL, D_Z), True),
                  _const_spec((1, A_WIDTH)), _const_spec((1, A_WIDTH))],
        out_specs=[row(w) for w in widths],
        out_shape=tuple(jax.ShapeDtypeStruct((n, w), d) for w, d in zip(widths, dtypes)),
        compiler_params=pltpu.CompilerParams(dimension_semantics=("parallel",), vmem_limit_bytes=_VMEM_LIMIT),
        name="inproj",
    )(x2d, g1, w_perm, gq, gk)


def _inproj_head_major(x, g1, w_perm, gq, gk, kt):
    bsz, t, _ = x.shape
    tm = min(_ROW_TILE, t)
    assert t % tm == 0 and tm % kt == 0
    row = lambda w: pl.BlockSpec((None, tm, w), lambda b, i: (b, i, 0))
    rows = lambda w, d: jax.ShapeDtypeStruct((bsz, t, w), d)
    lanes = lambda nh: pl.BlockSpec((None, nh, HEAD_DIM, tm), lambda b, i: (b, 0, 0, i))
    out_specs = [lanes(A_HEADS), lanes(IDX_HEADS), pl.BlockSpec((None, IDX_HEADS, tm), lambda b, i: (b, 0, i)),
                 pl.BlockSpec((None, A_HEADS, tm, HEAD_DIM), lambda b, i: (b, 0, i, 0)),
                 pl.BlockSpec((None, A_HEADS, tm // kt, HEAD_DIM, kt), lambda b, i: (b, 0, i, 0, 0)),
                 row(IDX_DIM), row(A_WIDTH), row(A_WIDTH), row(_LANES), row(_W_B), row(_W_C)]
    out_shape = (jax.ShapeDtypeStruct((bsz, A_HEADS, HEAD_DIM, t), _MXU_DTYPE),
                 jax.ShapeDtypeStruct((bsz, IDX_HEADS, IDX_DIM, t), _MXU_DTYPE),
                 jax.ShapeDtypeStruct((bsz, IDX_HEADS, t), F32),
                 jax.ShapeDtypeStruct((bsz, A_HEADS, t, HEAD_DIM), _MXU_DTYPE),
                 jax.ShapeDtypeStruct((bsz, A_HEADS, t // kt, HEAD_DIM, kt), _MXU_DTYPE),
                 rows(IDX_DIM, _MXU_DTYPE), rows(A_WIDTH, F32), rows(A_WIDTH, F32), rows(_LANES, F32),
                 rows(_W_B, F32), rows(_W_C, F32))
    return pl.pallas_call(
        functools.partial(_inproj_body, head_major=True, kt=kt),
        grid=(bsz, t // tm),
        in_specs=[row(D_MODEL), _const_spec((1, D_MODEL)), _const_spec((D_MODEL, D_Z), True),
                  _const_spec((1, A_WIDTH)), _const_spec((1, A_WIDTH))],
        out_specs=out_specs,
        out_shape=out_shape,
        compiler_params=pltpu.CompilerParams(dimension_semantics=("parallel", "parallel"),
                                             vmem_limit_bytes=_VMEM_LIMIT),
        name="inproj",
    )(x, g1, w_perm, gq, gk)


def _attn_body(q_ref, qi_ref, wi_ref, k_ref, vt_ref, ki_ref, tbl_ref, o_ref, key_ref, half_ref, sc_ref, sc2_ref, acc_ref, s_ref, s2_ref, p_ref, p2_ref, *,
               tq, kt, n_tiles, past, n_keys, topk, tbl_off, n_tbl, idx_bits):
    qb = pl.program_id(1)
    q0 = past + qb * tq
    n_kt = jnp.minimum(n_tiles, (q0 + tq + kt - 1) // kt)
    krow = lax.broadcasted_iota(jnp.int32, (kt, tq), 0)
    q_chunk = (q0 + lax.broadcasted_iota(jnp.int32, (1, tq), 1)) // CHUNK
    adm_end = jnp.minimum((q_chunk + 1) * CHUNK, n_keys)
    wi = wi_ref[...] * (IDX_HEADS ** -0.5)

    n_pair = (n_kt + 1) // 2

    def head_scores(t, dst_ref):
        ki_t = ki_ref[pl.ds(pl.multiple_of(t * kt, kt), kt), :]
        for h in range(IDX_HEADS):
            dst_ref[h] = _mm(ki_t, qi_ref[h])

    def combine(t, src_ref):
        acc = jnp.zeros((kt, tq), F32)
        for h in range(IDX_HEADS):
            acc = acc + wi[h:h + 1, :] * jnp.maximum(src_ref[h], 0.0)
        bits = pltpu.bitcast(acc, jnp.int32)
        bits = jnp.where(bits == _INT_MIN, 0, bits)
        key = jnp.where(bits >= 0, bits, bits ^ jnp.int32(0x7FFFFFFF))
        key = jnp.where(krow < adm_end - t * kt, key, _INT_MIN)
        key_ref[t] = key
        half_ref[t] = (key >> 16).astype(jnp.int16)

    head_scores(0, sc_ref)

    def score_pair(j, carry):
        t0 = 2 * j
        head_scores(t0 + 1, sc2_ref)
        combine(t0, sc_ref)
        head_scores(jnp.minimum(t0 + 2, 2 * n_pair - 2), sc_ref)
        combine(t0 + 1, sc2_ref)
        return carry

    lax.fori_loop(0, n_pair, score_pair, 0)

    def count(*pred_fns):
        def body(j, cnts):
            t = 2 * j
            k0, k1 = key_ref[t], key_ref[t + 1]
            return tuple(cnt + _fold_rows(jnp.where(fn(t, k0), 1.0, 0.0) + jnp.where(fn(t + 1, k1), 1.0, 0.0))
                         for fn, cnt in zip(pred_fns, cnts))
        cnts = lax.fori_loop(0, n_pair, body, tuple(jnp.zeros((8, tq), F32) for _ in pred_fns))
        return tuple(jnp.sum(cnt, axis=0, keepdims=True) for cnt in cnts)

    def count16(pred_fn):
        one, zero = jnp.int16(1), jnp.int16(0)
        def body(j, cnt):
            w = jnp.where(pred_fn(half_ref[2 * j]), one, zero) + jnp.where(pred_fn(half_ref[2 * j + 1]), one, zero)
            return cnt + _fold_rows(w, 16)
        cnt = lax.fori_loop(0, n_pair, body, jnp.zeros((16, tq), jnp.int16))
        return jnp.sum(cnt.astype(jnp.int32), axis=0, keepdims=True)

    def kth_largest16(kth):
        def bit(i, v):
            cand = v + lax.shift_left(jnp.int32(1), 15 - i)
            cand16 = cand.astype(jnp.int16)
            return jnp.where(count16(lambda k: k >= cand16) >= kth, cand, v)
        return lax.fori_loop(0, 16, bit, jnp.full((1, tq), -2 ** 15, jnp.int32))

    thr_hi = kth_largest16(topk)
    thr_hi16 = thr_hi.astype(jnp.int16)
    above = count16(lambda k: k > thr_hi16)

    def low_halves(t, carry):
        key = key_ref[t]
        half_ref[t] = jnp.where((key >> 16) == thr_hi, (key & 0xFFFF) - 2 ** 15, -2 ** 15).astype(jnp.int16)
        return carry

    lax.fori_loop(0, 2 * n_pair, low_halves, 0)
    thr = thr_hi * 2 ** 16 + (kth_largest16(topk - above) + 2 ** 15)
    thr = jnp.maximum(thr, _INT_MIN + 1)

    cnt_gt, cnt_ge = count(lambda t, k: k > thr, lambda t, k: k >= thr)
    need = topk - cnt_gt
    has_excess = jnp.max(jnp.where(cnt_ge > topk, 1.0, 0.0)) > 0.0

    def tie_search():
        def bit(i, last):
            cand = last + lax.shift_left(jnp.int32(1), idx_bits - 1 - i)
            below, = count(lambda t, k: (k == thr) & ((t * kt + krow) < cand))
            return jnp.where(below < need, cand, last)
        return lax.fori_loop(0, idx_bits, bit, jnp.zeros((1, tq), jnp.int32))

    last = lax.cond(has_excess, tie_search, lambda: jnp.full((1, tq), 2 ** 30, jnp.int32))

    acc_ref[...] = jnp.zeros(acc_ref.shape, F32)

    def logits(t, dst_ref):
        off = pl.multiple_of(t * kt, kt)
        for h in range(A_HEADS):
            dst_ref[h] = _mm(k_ref[h, pl.ds(off, kt), :], q_ref[h])

    def softmax_pv(t, src_ref, pr_ref, carry):
        ms, ls = carry
        off = pl.multiple_of(t * kt, kt)
        key = key_ref[t]
        sel = (key > thr) | ((key == thr) & ((off + krow) <= last))
        ti = jnp.clip(t - qb + tbl_off, 0, n_tbl - 1)
        new_ms, new_ls, alphas = [], [], []
        for h in range(A_HEADS):
            s = jnp.where(sel, src_ref[h] + tbl_ref[ti, h], _NEG)
            m_new = jnp.maximum(ms[h], jnp.max(s, axis=0, keepdims=True))
            alpha = jnp.exp2(ms[h] - m_new)
            p = jnp.exp2(s - m_new)
            new_ls.append(alpha * ls[h] + jnp.sum(p, axis=0, keepdims=True))
            new_ms.append(m_new)
            alphas.append(alpha)
            pr_ref[h] = p.astype(pr_ref.dtype)
        for h in range(A_HEADS):
            acc_ref[h] = alphas[h] * acc_ref[h] + jnp.dot(vt_ref[h, t], pr_ref[h], preferred_element_type=F32)
        return tuple(new_ms), tuple(new_ls)

    logits(0, s_ref)

    def attend_pair(j, carry):
        t0 = 2 * j
        logits(t0 + 1, s2_ref)
        carry = softmax_pv(t0, s_ref, p_ref, carry)
        logits(jnp.minimum(t0 + 2, 2 * n_pair - 2), s_ref)
        return softmax_pv(t0 + 1, s2_ref, p2_ref, carry)

    m0 = tuple(jnp.full((1, tq), _NEG, F32) for _ in range(A_HEADS))
    l0 = tuple(jnp.zeros((1, tq), F32) for _ in range(A_HEADS))
    _, ls = lax.fori_loop(0, n_pair, attend_pair, (m0, l0))
    for h in range(A_HEADS):
        o_ref[h] = acc_ref[h] / ls[h]


def _rel_bucket(rel):
    nb = N_BUCKETS // 2
    ret = jnp.where(rel > 0, nb, 0)
    n = jnp.abs(rel)
    max_exact = nb // 2
    nf = jnp.maximum(n, 1).astype(F32)
    large = max_exact + (jnp.log(nf / max_exact) / math.log(MAX_DISTANCE / max_exact)
                         * (nb - max_exact)).astype(jnp.int32)
    large = jnp.minimum(large, nb - 1)
    return ret + jnp.where(n < max_exact, n, large)


def _attn_tiling(t, n_keys):
    kt = 256
    tq = min(kt, t)
    assert t % tq == 0
    return tq, kt, t // tq, 2 * -(-n_keys // (2 * kt))


def _bias_tables(rel_bias, t, n_keys, past):
    tq, kt, nqb, n_tiles = _attn_tiling(t, n_keys)
    if nqb == 1:
        shifts = [tile * kt - past for tile in range(n_tiles)]
        off = 0
    else:
        assert past == 0 and tq == kt and kt >= MAX_DISTANCE
        shifts = [-2 * kt, -kt, 0]
        off = 2
    n_diff = kt + tq - 1
    rel = jnp.arange(n_diff)[None, :] - (tq - 1) + jnp.asarray(shifts, jnp.int32)[:, None]
    w = jnp.moveaxis(rel_bias.astype(F32)[_rel_bucket(rel)], -1, 1) * _LOG2E
    reps = -(-(kt * (n_diff + 1)) // n_diff)
    hankel = jnp.tile(w, reps)[..., :kt * (n_diff + 1)].reshape(w.shape[:2] + (kt, n_diff + 1))[..., :tq]
    return hankel[..., ::-1], off


def _attn_head_major(q_t, qi_t, wi_t, k_hm, vt_hm, ki_all, tbl, tbl_off, past, n_keys):
    bsz, _, _, t = q_t.shape
    topk = min(TOPK_MAX, n_keys // 4)
    tq, kt, nqb, n_tiles = _attn_tiling(t, n_keys)
    s_pad = n_tiles * kt
    assert s_pad >= topk and k_hm.shape[2] == s_pad and vt_hm.shape[2] == n_tiles and ki_all.shape[1] == s_pad
    n_tbl = tbl.shape[0]
    body = functools.partial(_attn_body, tq=tq, kt=kt, n_tiles=n_tiles, past=past, n_keys=n_keys, topk=topk,
                             tbl_off=tbl_off, n_tbl=n_tbl, idx_bits=int(s_pad).bit_length())
    return pl.pallas_call(
        body,
        grid=(bsz, nqb),
        in_specs=[pl.BlockSpec((None, A_HEADS, HEAD_DIM, tq), lambda b, i: (b, 0, 0, i)),
                  pl.BlockSpec((None, IDX_HEADS, IDX_DIM, tq), lambda b, i: (b, 0, 0, i)),
                  pl.BlockSpec((None, IDX_HEADS, tq), lambda b, i: (b, 0, i)),
                  pl.BlockSpec((None, A_HEADS, s_pad, HEAD_DIM), lambda b, i: (b, 0, 0, 0)),
                  pl.BlockSpec((None, A_HEADS, n_tiles, HEAD_DIM, kt), lambda b, i: (b, 0, 0, 0, 0)),
                  pl.BlockSpec((None, s_pad, IDX_DIM), lambda b, i: (b, 0, 0)),
                  _const_spec((n_tbl, A_HEADS, kt, tq), True)],
        out_specs=pl.BlockSpec((None, A_HEADS, HEAD_DIM, tq), lambda b, i: (b, 0, 0, i)),
        out_shape=jax.ShapeDtypeStruct((bsz, A_HEADS, HEAD_DIM, t), F32),
        scratch_shapes=[pltpu.VMEM((n_tiles, kt, tq), jnp.int32), pltpu.VMEM((n_tiles, kt, tq), jnp.int16),
                        pltpu.VMEM((IDX_HEADS, kt, tq), F32), pltpu.VMEM((IDX_HEADS, kt, tq), F32),
                        pltpu.VMEM((A_HEADS, HEAD_DIM, tq), F32),
                        pltpu.VMEM((A_HEADS, kt, tq), F32), pltpu.VMEM((A_HEADS, kt, tq), F32),
                        pltpu.VMEM((A_HEADS, kt, tq), _MXU_DTYPE), pltpu.VMEM((A_HEADS, kt, tq), _MXU_DTYPE)],
        compiler_params=pltpu.CompilerParams(dimension_semantics=("parallel", "parallel"),
                                             vmem_limit_bytes=_VMEM_LIMIT),
        name="attn",
    )(q_t, qi_t, wi_t, k_hm, vt_hm, ki_all, tbl)


def _attn(q, qi, wi, k_all, v_all, ki_all, tbl, tbl_off, past):
    bsz, t, _ = q.shape
    n_keys = k_all.shape[1]
    _, kt, _, n_tiles = _attn_tiling(t, n_keys)
    s_pad = n_tiles * kt
    if s_pad != n_keys:
        k_all, v_all, ki_all = (jnp.pad(a, ((0, 0), (0, s_pad - n_keys), (0, 0))) for a in (k_all, v_all, ki_all))
    q_t = q.reshape(bsz, t, A_HEADS, HEAD_DIM).transpose(0, 2, 3, 1)
    qi_t = qi.reshape(bsz, t, IDX_HEADS, IDX_DIM).transpose(0, 2, 3, 1)
    k_hm = k_all.reshape(bsz, s_pad, A_HEADS, HEAD_DIM).transpose(0, 2, 1, 3)
    vt_hm = v_all.reshape(bsz, n_tiles, kt, A_HEADS, HEAD_DIM).transpose(0, 3, 1, 4, 2)
    o_t = _attn_head_major(q_t, qi_t, wi.transpose(0, 2, 1), k_hm, vt_hm, ki_all, tbl, tbl_off, past, n_keys)
    return o_t.transpose(0, 3, 1, 2).reshape(bsz, t, A_WIDTH)


def _gla_chunk(zb, misc, wg, bg, og, st):
    c = zb.shape[0]
    dkw = B_HEADS * B_DK
    q = zb[:, 0:dkw] * (B_DK ** -0.5)
    k = zb[:, dkw:2 * dkw]
    v3 = _heads3(zb[:, 2 * dkw:2 * dkw + B_WIDTH], B_HEADS, B_DV)
    gate = zb[:, 2 * dkw + B_WIDTH:]
    log_a = jax.nn.log_sigmoid(_mmx(misc, wg) + bg) / GATE_TAU
    b = _cumsum_rows(log_a)
    b_last = b[c - 1:c, :]
    b_mid = b[c // 2 - 1:c // 2, :]
    att = _bmm_nt(_heads3(q * jnp.exp(b - b_mid), B_HEADS, B_DK), _heads3(k * jnp.exp(b_mid - b), B_HEADS, B_DK))
    ri = lax.broadcasted_iota(jnp.int32, att.shape, 1)
    cj = lax.broadcasted_iota(jnp.int32, att.shape, 2)
    att = jnp.where(ri >= cj, att, 0.0)
    o3 = _bmm_nt(_heads3(q * jnp.exp(b), B_HEADS, B_DK), st) + _bmm(att, v3)
    upd = _bmm_tn(v3, _heads3(k * jnp.exp(b_last - b), B_HEADS, B_DK))
    st_new = st * _heads3(jnp.exp(b_last), B_HEADS, B_DK) + upd
    o = _lanes2(o3 * lax.rsqrt(jnp.mean(o3 * o3, axis=-1, keepdims=True) + EPS))
    return o * og * jax.nn.silu(gate), st_new


def _gla_body(zb_ref, misc_ref, wg_ref, bg_ref, og_ref, s0_ref, o_ref, s_out_ref, s_ref):
    @pl.when(pl.program_id(1) == 0)
    def _():
        s_ref[...] = s0_ref[...]

    for g in range(zb_ref.shape[0]):
        o, st = _gla_chunk(zb_ref[g], misc_ref[g], wg_ref[...], bg_ref[...], og_ref[...], s_ref[g])
        o_ref[g] = o
        s_ref[g] = st
        s_out_ref[g] = st


def _gla(zb, misc, wg_emb, bg, og, s0_t, c):
    bsz, t, _ = zb.shape
    dkw = B_HEADS * B_DK
    g = math.gcd(_SEQS_PER_STEP, bsz)
    blk = lambda w: pl.BlockSpec((g, c, w), lambda b, i: (b, i, 0))
    st = pl.BlockSpec((g, B_HEADS, B_DV, B_DK), lambda b, i: (b, 0, 0, 0))
    return pl.pallas_call(
        _gla_body,
        grid=(bsz // g, t // c),
        in_specs=[blk(_W_B), blk(_LANES), _const_spec((_LANES, dkw)), _const_spec((1, dkw)),
                  _const_spec((1, B_WIDTH)), st],
        out_specs=[blk(B_WIDTH), st],
        out_shape=(jax.ShapeDtypeStruct((bsz, t, B_WIDTH), F32),
                   jax.ShapeDtypeStruct((bsz, B_HEADS, B_DV, B_DK), F32)),
        scratch_shapes=[pltpu.VMEM((g, B_HEADS, B_DV, B_DK), F32)],
        compiler_params=pltpu.CompilerParams(dimension_semantics=("parallel", "arbitrary"),
                                             vmem_limit_bytes=_VMEM_LIMIT),
        name="gla",
    )(zb, misc, wg_emb, bg, og, s0_t)


_PAIRS = C_HEADS // 2


def _pairs3(x):
    return jnp.stack([x[:, p * _LANES:(p + 1) * _LANES] for p in range(_PAIRS)], axis=0)


def _pair_diag(x3):
    left = lax.broadcasted_iota(jnp.int32, x3.shape, 2) < x3.shape[2] // 2
    zero = jnp.zeros((), x3.dtype)
    return jnp.concatenate([jnp.where(left, x3, zero), jnp.where(left, zero, x3)], axis=1)


def _pair_diag_wide(x3):
    left = lax.broadcasted_iota(jnp.int32, x3.shape, 2) % _LANES < _LANES // 2
    zero = jnp.zeros((), x3.dtype)
    return jnp.concatenate([jnp.where(left, x3, zero), jnp.where(left, zero, x3)], axis=1)


def _pmm(a, b):
    return jnp.einsum("pij,pjk->pik", a, b, preferred_element_type=F32)


def _pmm3(a, b):
    (ah, al), (bh, bl) = a, b
    n = ah.shape[1]
    both = _pmm(jnp.concatenate([ah, al], axis=1), _pair_diag(bh))
    return both[:, 0:n, :] + (both[:, n:, :] + _pmm(ah, _pair_diag(bl)))


def _gdn_chunk(zc_ref, misc, cw_ref, alog, dtb, og, xpad_ref, s_bd):
    c = misc.shape[0]
    w3 = 3 * C_WIDTH
    bf = lambda x: x.astype(_MXU_DTYPE)
    xpad_ref[8:8 + c, :] = zc_ref[:, 0:w3]
    conv = xpad_ref[5:5 + c, :] * cw_ref[0:1, :]
    for i in range(1, CONV_W):
        conv = conv + xpad_ref[5 + i:5 + i + c, :] * cw_ref[i:i + 1, :]
    xpad_ref[0:8, :] = xpad_ref[c:c + 8, :]
    act = jax.nn.silu(conv)
    q2 = _group_norm(act[:, 0:C_WIDTH], C_DK, False) * (C_DK ** -0.5)
    k2 = _group_norm(act[:, C_WIDTH:2 * C_WIDTH], C_DK, False)
    v2 = act[:, 2 * C_WIDTH:w3]

    g_full = -jnp.exp(alog) * jax.nn.softplus(misc + dtb)
    beta_full = jax.nn.sigmoid(misc)
    gcum = _cumsum_rows(g_full)
    gexp2 = _expand_heads(gcum, _M_CA, C_HEADS, C_DV)
    bexp2 = _expand_heads(beta_full, _M_CB, C_HEADS, C_DV)
    g_last2 = gexp2[c - 1:c, :]
    eg2 = jnp.exp(gexp2)

    gcum_t = gcum.T
    col = lambda h: jnp.broadcast_to(gcum[:, _M_CA + h:_M_CA + h + 1], (c, c))
    row = lambda h: jnp.broadcast_to(gcum_t[_M_CA + h:_M_CA + h + 1, :], (c, c))
    diff = jnp.stack([jnp.concatenate([col(2 * p) - row(2 * p), col(2 * p + 1) - row(2 * p + 1)], axis=-1)
                      for p in range(_PAIRS)], axis=0)
    ri = lax.broadcasted_iota(jnp.int32, diff.shape, 1)
    cj = lax.broadcasted_iota(jnp.int32, diff.shape, 2) % c
    lmask = jnp.where(ri >= cj, jnp.exp(jnp.minimum(diff, 0.0)), 0.0)

    kb2 = k2 * bexp2
    k_bd = _pair_diag(bf(_pairs3(k2)))
    kq = jnp.einsum("pil,pjl->pij", bf(_pairs3(jnp.concatenate([kb2, q2], axis=0))), k_bd,
                    preferred_element_type=F32)
    yield
    m = jnp.where(ri > cj, kq[:, 0:c, :] * lmask, 0.0)
    qk = kq[:, c:, :] * lmask
    inv = jnp.where(ri == cj, 1.0, 0.0) - m
    pw = _split_bf16(m)
    for _ in range(int(math.log2(c)) - 1):
        sq = _pmm3(pw, pw)
        yield
        pw = _split_bf16(sq)
        step = _pmm3(_split_bf16(inv), pw)
        yield
        inv = inv + step
    rhs = jnp.concatenate([_pairs3(v2 * bexp2), _pairs3(kb2 * eg2)], axis=-1)
    uw = _pmm(bf(inv), _pair_diag_wide(bf(rhs)))
    yield
    u, w = uw[:, :, 0:_LANES], uw[:, :, _LANES:]
    wq = _pmm(bf(jnp.concatenate([w, _pairs3(q2 * eg2)], axis=1)), bf(s_bd))
    yield
    v_new = u - wq[:, 0:c, :]
    o3 = wq[:, c:, :] + _pmm(bf(qk), _pair_diag(bf(v_new)))
    upd = jnp.einsum("pck,pcv->pkv", bf(_pairs3(k2 * jnp.exp(g_last2 - gexp2))), bf(v_new),
                     preferred_element_type=F32)
    yield
    same_head = (lax.broadcasted_iota(jnp.int32, upd.shape, 1) // C_DK) == (lax.broadcasted_iota(jnp.int32, upd.shape, 2) // C_DV)
    s_new = s_bd * _pairs3(jnp.exp(g_last2)) + jnp.where(same_head, upd, 0.0)
    o2 = jnp.concatenate([o3[p] for p in range(_PAIRS)], axis=-1)
    return _group_norm(o2, C_DV, True) * og * jax.nn.silu(zc_ref[:, w3:]), s_new


def _gdn_body(zc_ref, misc_ref, cw_ref, alog_ref, dtb_ref, og_ref, s0_ref, hist_ref,
              o_ref, s_out_ref, xpad_ref, s_ref):
    n_seq = zc_ref.shape[0]

    @pl.when(pl.program_id(1) == 0)
    def _():
        xpad_ref[:, 0:8, :] = hist_ref[...]
        zero = jnp.zeros((C_DK, C_DV), F32)
        for g in range(n_seq):
            for p in range(_PAIRS):
                s_ref[g, p] = jnp.concatenate([jnp.concatenate([s0_ref[g, 2 * p], zero], axis=-1),
                                               jnp.concatenate([zero, s0_ref[g, 2 * p + 1]], axis=-1)], axis=0)

    chunks = [_gdn_chunk(zc_ref.at[g], misc_ref[g], cw_ref, alog_ref[...], dtb_ref[...], og_ref[...],
                         xpad_ref.at[g], s_ref[g]) for g in range(n_seq)]
    for g, (o, s_new) in enumerate(_round_robin(chunks)):
        o_ref[g] = o
        s_ref[g] = s_new

    @pl.when(pl.program_id(1) == pl.num_programs(1) - 1)
    def _():
        for g in range(n_seq):
            for p in range(_PAIRS):
                s_out_ref[g, 2 * p] = s_ref[g, p, 0:C_DK, 0:C_DV]
                s_out_ref[g, 2 * p + 1] = s_ref[g, p, C_DK:, C_DV:]


def _gdn(zc, misc, conv_w, alog_emb, dtb_emb, og, s0, hist8, c):
    bsz, t, _ = zc.shape
    w3 = 3 * C_WIDTH
    g = math.gcd(_SEQS_PER_STEP, bsz)
    blk = lambda w: pl.BlockSpec((g, c, w), lambda b, i: (b, i, 0))
    st = pl.BlockSpec((g, C_HEADS, C_DK, C_DV), lambda b, i: (b, 0, 0, 0))
    return pl.pallas_call(
        _gdn_body,
        grid=(bsz // g, t // c),
        in_specs=[blk(_W_C), blk(_LANES), _const_spec((CONV_W, w3)), _const_spec((1, _LANES)),
                  _const_spec((1, _LANES)), _const_spec((1, C_WIDTH)), st,
                  pl.BlockSpec((g, 8, w3), lambda b, i: (b, 0, 0))],
        out_specs=[blk(C_WIDTH), st],
        out_shape=(jax.ShapeDtypeStruct((bsz, t, C_WIDTH), F32),
                   jax.ShapeDtypeStruct((bsz, C_HEADS, C_DK, C_DV), F32)),
        scratch_shapes=[pltpu.VMEM((g, 8 + c, w3), F32), pltpu.VMEM((g, _PAIRS, 2 * C_DK, 2 * C_DV), F32)],
        compiler_params=pltpu.CompilerParams(dimension_semantics=("parallel", "arbitrary"),
                                             vmem_limit_bytes=_VMEM_LIMIT),
        name="gdn",
    )(zc, misc, conv_w, alog_emb, dtb_emb, og, s0, hist8)


def _mlp_body(x_ref, oa_ref, ob_ref, oc_ref, wo_ref, g2_ref, wup_ref, wdn_ref, y_ref, *, ff_tile, head_major):
    if head_major:
        oa = jnp.concatenate([oa_ref[h].T for h in range(A_HEADS)], axis=-1)
    else:
        oa = oa_ref[...]
    mixed = jnp.concatenate([oa, ob_ref[...], oc_ref[...]], axis=-1)
    y_ref[...] = x_ref[...] + _mm(mixed, wo_ref[...])
    h2 = (_rms(y_ref[...]) * g2_ref[...]).astype(_MXU_DTYPE)
    for j in range(D_FF // ff_tile):
        up = jnp.dot(h2, wup_ref[:, j * ff_tile:(j + 1) * ff_tile], preferred_element_type=F32)
        y_ref[...] += _mm(jnp.square(jnp.maximum(up, 0.0)), wdn_ref[j * ff_tile:(j + 1) * ff_tile, :])


_MLP_WEIGHT_SPECS = lambda: [_const_spec((D_MODEL, D_MODEL), True), _const_spec((1, D_MODEL)),
                             _const_spec((D_MODEL, D_FF), True), _const_spec((D_FF, D_MODEL), True)]


def _mlp(x2d, oa, ob, oc, wo, g2, wup, wdn):
    n = x2d.shape[0]
    tm = min(_ROW_TILE, n)
    assert n % tm == 0
    row = lambda w: pl.BlockSpec((tm, w), lambda i: (i, 0))
    return pl.pallas_call(
        functools.partial(_mlp_body, ff_tile=1024, head_major=False),
        grid=(n // tm,),
        in_specs=[row(D_MODEL), row(A_WIDTH), row(B_WIDTH), row(C_WIDTH)] + _MLP_WEIGHT_SPECS(),
        out_specs=row(D_MODEL),
        out_shape=jax.ShapeDtypeStruct((n, D_MODEL), F32),
        compiler_params=pltpu.CompilerParams(dimension_semantics=("parallel",), vmem_limit_bytes=_VMEM_LIMIT),
        name="mlp",
    )(x2d, oa, ob, oc, wo, g2, wup, wdn)


def _mlp_head_major(x, oa_t, ob, oc, wo, g2, wup, wdn):
    bsz, t, _ = x.shape
    tm = min(_ROW_TILE, t)
    assert t % tm == 0
    row = lambda w: pl.BlockSpec((None, tm, w), lambda b, i: (b, i, 0))
    return pl.pallas_call(
        functools.partial(_mlp_body, ff_tile=1024, head_major=True),
        grid=(bsz, t // tm),
        in_specs=[row(D_MODEL), pl.BlockSpec((None, A_HEADS, HEAD_DIM, tm), lambda b, i: (b, 0, 0, i)),
                  row(B_WIDTH), row(C_WIDTH)] + _MLP_WEIGHT_SPECS(),
        out_specs=row(D_MODEL),
        out_shape=jax.ShapeDtypeStruct((bsz, t, D_MODEL), F32),
        compiler_params=pltpu.CompilerParams(dimension_semantics=("parallel", "parallel"),
                                             vmem_limit_bytes=_VMEM_LIMIT),
        name="mlp",
    )(x, oa_t, ob, oc, wo, g2, wup, wdn)


def _prep_layer_params(ln1_g, w_in, a_qnorm_g, a_knorm_g, gla_w_gate, gla_b_gate, gla_onorm_g,
                       conv_w, gdn_a_log, gdn_dt_bias, gdn_onorm_g, w_o, ln2_g, w_up, w_down):
    cols = [jnp.zeros((D_MODEL, _PAD[name]), w_in.dtype) if name in _PAD
            else w_in[:, _SRC[name]:_SRC[name] + _SIZE[name]] for name in _NEW_ORDER]
    lane_emb = lambda vec, start: jnp.zeros((1, _LANES), F32).at[0, start:start + vec.shape[0]].set(vec.astype(F32))
    return dict(
        g1=ln1_g.reshape(1, D_MODEL).astype(F32),
        w_perm=jnp.concatenate(cols, axis=1).astype(_MXU_DTYPE),
        gq=jnp.tile(a_qnorm_g.astype(F32), A_HEADS).reshape(1, A_WIDTH),
        gk=jnp.tile(a_knorm_g.astype(F32), A_HEADS).reshape(1, A_WIDTH),
        wg_emb=jnp.zeros((_LANES, B_HEADS * B_DK), F32).at[_M_GLR:_M_GLR + GATE_RANK].set(gla_w_gate.astype(F32)),
        bg=gla_b_gate.reshape(1, -1).astype(F32),
        og_b=jnp.tile(gla_onorm_g.astype(F32), B_HEADS).reshape(1, B_WIDTH),
        conv_w=conv_w.astype(F32),
        alog_emb=lane_emb(gdn_a_log, _M_CA),
        dtb_emb=lane_emb(gdn_dt_bias, _M_CA),
        og_c=jnp.tile(gdn_onorm_g.astype(F32), C_HEADS).reshape(1, C_WIDTH),
        wo=w_o.astype(_MXU_DTYPE), g2=ln2_g.reshape(1, D_MODEL).astype(F32),
        wup=w_up.astype(_MXU_DTYPE), wdn=w_down.astype(_MXU_DTYPE),
    )


def _layer(x, cache, p, tbl, tbl_off):
    bsz, t, _ = x.shape
    n = bsz * t
    assert t >= CONV_W - 1
    c = min(CHUNK, t)
    r3 = lambda a: a.reshape(bsz, t, a.shape[-1])
    mixer_b = lambda zb, misc, s0: _gla(zb, misc, p["wg_emb"], p["bg"], p["og_b"], s0, c)
    mixer_c = lambda zc, misc, s0, hist8: _gdn(zc, misc, p["conv_w"], p["alog_emb"], p["dtb_emb"], p["og_c"], s0, hist8, c)

    if cache is None:
        _, kt, _, n_tiles = _attn_tiling(t, t)
        assert n_tiles * kt == t
        q_t, qi_t, wi_t, k_hm, vt_hm, ki16, ka, va, misc, zb, zc = _inproj_head_major(
            x, p["g1"], p["w_perm"], p["gq"], p["gk"], kt)
        o_a_t = _attn_head_major(q_t, qi_t, wi_t, k_hm, vt_hm, ki16, tbl, tbl_off, 0, t)
        o_b, s_gla_t = mixer_b(zb, misc, jnp.zeros((bsz, B_HEADS, B_DV, B_DK), F32))
        o_c, s_gdn = mixer_c(zc, misc, jnp.zeros((bsz, C_HEADS, C_DK, C_DV), F32), jnp.zeros((bsz, 8, 3 * C_WIDTH), F32))
        y = _mlp_head_major(x, o_a_t, o_b, o_c, p["wo"], p["g2"], p["wup"], p["wdn"])
    else:
        ck, cv, cki, sg, sd, cbuf = cache
        past = ck.shape[1]
        x2d = x.reshape(n, D_MODEL)
        qa, qi, ka, va, misc, zb, zc = _inproj(x2d, p["g1"], p["w_perm"], p["gq"], p["gk"])
        ka, va, misc, zb, zc = r3(ka), r3(va), r3(misc), r3(zb), r3(zc)
        k_all = jnp.concatenate([ck.reshape(bsz, past, A_WIDTH).astype(F32), ka], axis=1)
        v_all = jnp.concatenate([cv.reshape(bsz, past, A_WIDTH).astype(F32), va], axis=1)
        ki_all = jnp.concatenate([cki.astype(F32), misc[..., 0:IDX_DIM]], axis=1)
        o_a = _attn(r3(qa), r3(qi), misc[..., _M_WI:_M_WI + IDX_HEADS], k_all.astype(_MXU_DTYPE),
                    v_all.astype(_MXU_DTYPE), ki_all.astype(_MXU_DTYPE), tbl, tbl_off, past)
        o_b, s_gla_t = mixer_b(zb, misc, jnp.swapaxes(sg.astype(F32), 2, 3))
        hist8 = jnp.pad(cbuf.astype(F32), ((0, 0), (8 - (CONV_W - 1), 0), (0, 0)))
        o_c, s_gdn = mixer_c(zc, misc, sd.astype(F32), hist8)
        y = _mlp(x2d, o_a.reshape(n, A_WIDTH), o_b.reshape(n, B_WIDTH), o_c.reshape(n, C_WIDTH),
                 p["wo"], p["g2"], p["wup"], p["wdn"]).reshape(bsz, t, D_MODEL)

    s_gla = jnp.swapaxes(s_gla_t, 2, 3)
    new_conv = zc[:, t - (CONV_W - 1):, 0:3 * C_WIDTH]
    state = (ka.reshape(bsz, t, A_HEADS, HEAD_DIM), va.reshape(bsz, t, A_HEADS, HEAD_DIM), misc[..., 0:IDX_DIM],
             s_gla, s_gdn, new_conv)
    return y, state


def kernel(x_prompt, x_sample, cache_a_k, cache_a_v, cache_a_kidx, state_gla, state_gdn, state_conv, rel_bias, ln1_g, w_in, a_qnorm_g, a_knorm_g, gla_w_gate, gla_b_gate, gla_onorm_g, conv_w, gdn_a_log, gdn_dt_bias, gdn_onorm_g, w_o, ln2_g, w_up, w_down):
    depth = ln1_g.shape[0]
    yp, ys = x_prompt, x_sample
    past = cache_a_k.shape[2]
    tbl_p = _bias_tables(rel_bias, x_prompt.shape[1], x_prompt.shape[1], 0)
    tbl_s = _bias_tables(rel_bias, x_sample.shape[1], past + x_sample.shape[1], past)
    new_p, new_s = [], []
    for l in range(depth):
        p = _prep_layer_params(ln1_g[l], w_in[l], a_qnorm_g[l], a_knorm_g[l], gla_w_gate[l],
                               gla_b_gate[l], gla_onorm_g[l], conv_w[l], gdn_a_log[l], gdn_dt_bias[l],
                               gdn_onorm_g[l], w_o[l], ln2_g[l], w_up[l], w_down[l])
        yp, st_p = _layer(yp, None, p, *tbl_p)
        cache_l = (cache_a_k[l], cache_a_v[l], cache_a_kidx[l], state_gla[l], state_gdn[l], state_conv[l])
        ys, st_s = _layer(ys, cache_l, p, *tbl_s)
        new_p.append(st_p)
        new_s.append(st_s)
    stack = lambda states, i: jnp.stack([s[i] for s in states], axis=0)
    return (yp, ys) + tuple(stack(new_p, i) for i in range(6)) + tuple(stack(new_s, i) for i in range(6))
```

```python
import functools
import math

import numpy as np
import jax
import jax.numpy as jnp
from jax import lax
from jax.experimental import pallas as pl
from jax.experimental.pallas import tpu as pltpu

D_MODEL = 1024
CHUNK = 64
HEAD_DIM = 64
A_HEADS = 4
A_WIDTH = A_HEADS * HEAD_DIM
IDX_HEADS = 8
IDX_DIM = 64
TOPK_MAX = 256
N_BUCKETS = 32
MAX_DISTANCE = 128
B_HEADS = 4
B_DK = 32
B_DV = 64
B_WIDTH = B_HEADS * B_DV
GATE_RANK = 16
GATE_TAU = 16.0
C_HEADS = 8
C_DK = 64
C_DV = 64
C_WIDTH = C_HEADS * C_DV
CONV_W = 4
D_FF = 4 * D_MODEL
EPS = 1e-6

F32 = jnp.float32
BF16 = jnp.bfloat16
_MXU_DTYPE = BF16
_HI = lax.Precision.HIGHEST
_VMEM_LIMIT = 56 * 1024 * 1024
_LANES = 128
_ROW_TILE = 512
_SEQS_PER_STEP = 8
_NEG = -1e30
_LOG2E = math.log2(math.e)
_INT_MIN = -2 ** 31

_SPLIT_NAMES = ("aq", "ak", "av", "aqi", "aki", "awi", "bq", "bk", "bv", "bglr", "bog",
                "cq", "ck", "cv", "ca", "cb", "cog")
_SPLIT_SIZES = (A_WIDTH, A_WIDTH, A_WIDTH, IDX_HEADS * IDX_DIM, IDX_DIM, IDX_HEADS,
                B_HEADS * B_DK, B_HEADS * B_DK, B_WIDTH, GATE_RANK, B_WIDTH,
                C_HEADS * C_DK, C_HEADS * C_DK, C_WIDTH, C_HEADS, C_HEADS, C_WIDTH)
D_IN = sum(_SPLIT_SIZES)
_SRC = dict(zip(_SPLIT_NAMES, np.concatenate([[0], np.cumsum(_SPLIT_SIZES)[:-1]]).tolist()))
_SIZE = dict(zip(_SPLIT_NAMES, _SPLIT_SIZES))

_M_WI, _M_GLR, _M_CA, _M_CB = 64, 72, 88, 96
_NEW_ORDER = ("aq", "ak", "av", "aqi", "aki", "awi", "bglr", "ca", "cb", "pad24",
              "bq", "bk", "bv", "bog", "cq", "ck", "cv", "cog")
_PAD = {"pad24": 24}


def _new_offsets():
    offs, pos = {}, 0
    for name in _NEW_ORDER:
        offs[name] = pos
        pos += _PAD.get(name, 0) or _SIZE[name]
    return offs, pos


_OFF, D_Z = _new_offsets()
_C_QA, _C_KA, _C_VA, _C_QI, _C_MISC = _OFF["aq"], _OFF["ak"], _OFF["av"], _OFF["aqi"], _OFF["aki"]
_C_B, _C_C = _OFF["bq"], _OFF["cq"]
_W_B = 2 * B_HEADS * B_DK + 2 * B_WIDTH
_W_C = 4 * C_WIDTH
assert _C_MISC % _LANES == 0 and _C_B == _C_MISC + _LANES and _C_C == _C_B + _W_B and D_Z == _C_C + _W_C
assert (_OFF["awi"], _OFF["bglr"], _OFF["ca"], _OFF["cb"]) == tuple(_C_MISC + m for m in (_M_WI, _M_GLR, _M_CA, _M_CB))


def _mm(a, b):
    return jnp.dot(a.astype(_MXU_DTYPE), b.astype(_MXU_DTYPE), preferred_element_type=F32)


def _mmx(a, b):
    return jnp.dot(a, b, preferred_element_type=F32, precision=_HI)


def _split_bf16(x):
    hi = pltpu.bitcast(pltpu.bitcast(x, jnp.int32) & jnp.int32(-65536), F32)
    return hi.astype(BF16), (x - hi).astype(BF16)


def _cumsum_rows(x):
    r = lax.broadcasted_iota(jnp.int32, x.shape, 0)
    s = 1
    while s < x.shape[0]:
        x = x + jnp.where(r >= s, pltpu.roll(x, s, 0), 0.0)
        s *= 2
    return x


def _heads3(x, n_heads, width):
    return jnp.stack([x[:, h * width:(h + 1) * width] for h in range(n_heads)], axis=0)


def _lanes2(x3):
    return jnp.concatenate([x3[h] for h in range(x3.shape[0])], axis=-1)


def _expand_heads(x, lane0, n_heads, width):
    c = x.shape[0]
    return jnp.concatenate([jnp.broadcast_to(x[:, lane0 + h:lane0 + h + 1], (c, width)) for h in range(n_heads)],
                           axis=-1)


def _bmm(a, b):
    return jnp.einsum("hij,hjk->hik", a.astype(_MXU_DTYPE), b.astype(_MXU_DTYPE), preferred_element_type=F32)


def _bmm_nt(a, b):
    return jnp.einsum("hik,hjk->hij", a.astype(_MXU_DTYPE), b.astype(_MXU_DTYPE), preferred_element_type=F32)


def _bmm_tn(a, b):
    return jnp.einsum("hck,hcv->hkv", a.astype(_MXU_DTYPE), b.astype(_MXU_DTYPE), preferred_element_type=F32)


def _rms(x):
    return x * lax.rsqrt(jnp.mean(x * x, axis=-1, keepdims=True) + EPS)


def _group_norm(x, group, mean):
    outs = []
    for g in range(x.shape[-1] // group):
        xs = x[:, g * group:(g + 1) * group]
        ss = jnp.sum(xs * xs, axis=-1, keepdims=True)
        outs.append(xs * lax.rsqrt((ss / group if mean else ss) + EPS))
    return jnp.concatenate(outs, axis=-1)


def _fold_rows(w, rows=8):
    parts = [w[i * rows:(i + 1) * rows, :] for i in range(w.shape[0] // rows)]
    while len(parts) > 1:
        parts = [parts[i] + parts[i + 1] for i in range(0, len(parts), 2)]
    return parts[0]


def _round_robin(gens):
    results = [None] * len(gens)
    live = list(range(len(gens)))
    while live:
        for i in list(live):
            try:
                next(gens[i])
            except StopIteration as stop:
                results[i] = stop.value
                live.remove(i)
    return results


def _const_spec(shape, single_buffer=False):
    zeros = (0,) * len(shape)
    if single_buffer:
        return pl.BlockSpec(shape, lambda *_: zeros, pipeline_mode=pl.Buffered(1))
    return pl.BlockSpec(shape, lambda *_: zeros)


def _inproj_body(x_ref, g1_ref, w_ref, gq_ref, gk_ref, *out_refs, head_major, kt):
    h = (_rms(x_ref[...]) * g1_ref[...]).astype(_MXU_DTYPE)

    def proj(c0, width):
        return jnp.dot(h, w_ref[:, c0:c0 + width], preferred_element_type=F32)

    qn = _group_norm(proj(_C_QA, A_WIDTH), HEAD_DIM, True) * gq_ref[...] * (HEAD_DIM ** -0.5 * _LOG2E)
    kn = _group_norm(proj(_C_KA, A_WIDTH), HEAD_DIM, True) * gk_ref[...]
    v = proj(_C_VA, A_WIDTH)
    qi = proj(_C_QI, IDX_HEADS * IDX_DIM) * (IDX_DIM ** -0.5)
    misc = proj(_C_MISC, _LANES)
    if head_major:
        qt_ref, qit_ref, wit_ref, khm_ref, vt_ref, ki_ref, ka_ref, va_ref, misc_ref, zb_ref, zc_ref = out_refs
        for hd in range(A_HEADS):
            hs = slice(hd * HEAD_DIM, (hd + 1) * HEAD_DIM)
            qt_ref[hd] = qn[:, hs].T.astype(qt_ref.dtype)
            khm_ref[hd] = kn[:, hs].astype(khm_ref.dtype)
            for j in range(v.shape[0] // kt):
                vt_ref[hd, j] = v[j * kt:(j + 1) * kt, hs].T.astype(vt_ref.dtype)
        for hd in range(IDX_HEADS):
            qit_ref[hd] = qi[:, hd * IDX_DIM:(hd + 1) * IDX_DIM].T.astype(qit_ref.dtype)
        wit_ref[...] = misc.T[_M_WI:_M_WI + IDX_HEADS, :]
        ki_ref[...] = misc[:, 0:IDX_DIM].astype(ki_ref.dtype)
    else:
        qa_ref, qi_ref, ka_ref, va_ref, misc_ref, zb_ref, zc_ref = out_refs
        qa_ref[...] = qn.astype(qa_ref.dtype)
        qi_ref[...] = qi.astype(qi_ref.dtype)
    ka_ref[...] = kn
    va_ref[...] = v
    misc_ref[...] = misc
    zb_ref[...] = proj(_C_B, _W_B)
    zc_ref[...] = proj(_C_C, _W_C)


def _inproj(x2d, g1, w_perm, gq, gk):
    n = x2d.shape[0]
    tm = min(_ROW_TILE, n)
    assert n % tm == 0
    row = lambda w: pl.BlockSpec((tm, w), lambda i: (i, 0))
    widths = (A_WIDTH, IDX_HEADS * IDX_DIM, A_WIDTH, A_WIDTH, _LANES, _W_B, _W_C)
    dtypes = (_MXU_DTYPE, _MXU_DTYPE, F32, F32, F32, F32, F32)
    return pl.pallas_call(
        functools.partial(_inproj_body, head_major=False, kt=None),
        grid=(n // tm,),
        in_specs=[row(D_MODEL), _const_spec((1, D_MODEL)), _const_spec((D_MODEL, D_Z), True),
                  _const_spec((1, A_WIDTH)), _const_spec((1, A_WIDTH))],
        out_specs=[row(w) for w in widths],
        out_shape=tuple(jax.ShapeDtypeStruct((n, w), d) for w, d in zip(widths, dtypes)),
        compiler_params=pltpu.CompilerParams(dimension_semantics=("parallel",), vmem_limit_bytes=_VMEM_LIMIT),
        name="inproj",
    )(x2d, g1, w_perm, gq, gk)


def _inproj_head_major(x, g1, w_perm, gq, gk, kt):
    bsz, t, _ = x.shape
    tm = min(_ROW_TILE, t)
    assert t % tm == 0 and tm % kt == 0
    row = lambda w: pl.BlockSpec((None, tm, w), lambda b, i: (b, i, 0))
    rows = lambda w, d: jax.ShapeDtypeStruct((bsz, t, w), d)
    lanes = lambda nh: pl.BlockSpec((None, nh, HEAD_DIM, tm), lambda b, i: (b, 0, 0, i))
    out_specs = [lanes(A_HEADS), lanes(IDX_HEADS), pl.BlockSpec((None, IDX_HEADS, tm), lambda b, i: (b, 0, i)),
                 pl.BlockSpec((None, A_HEADS, tm, HEAD_DIM), lambda b, i: (b, 0, i, 0)),
                 pl.BlockSpec((None, A_HEADS, tm // kt, HEAD_DIM, kt), lambda b, i: (b, 0, i, 0, 0)),
                 row(IDX_DIM), row(A_WIDTH), row(A_WIDTH), row(_LANES), row(_W_B), row(_W_C)]
    out_shape = (jax.ShapeDtypeStruct((bsz, A_HEADS, HEAD_DIM, t), _MXU_DTYPE),
                 jax.ShapeDtypeStruct((bsz, IDX_HEADS, IDX_DIM, t), _MXU_DTYPE),
                 jax.ShapeDtypeStruct((bsz, IDX_HEADS, t), F32),
                 jax.ShapeDtypeStruct((bsz, A_HEADS, t, HEAD_DIM), _MXU_DTYPE),
                 jax.ShapeDtypeStruct((bsz, A_HEADS, t // kt, HEAD_DIM, kt), _MXU_DTYPE),
                 rows(IDX_DIM, _MXU_DTYPE), rows(A_WIDTH, F32), rows(A_WIDTH, F32), rows(_LANES, F32),
                 rows(_W_B, F32), rows(_W_C, F32))
    return pl.pallas_call(
        functools.partial(_inproj_body, head_major=True, kt=kt),
        grid=(bsz, t // tm),
        in_specs=[row(D_MODEL), _const_spec((1, D_MODEL)), _const_spec((D_MODEL, D_Z), True),
                  _const_spec((1, A_WIDTH)), _const_spec((1, A_WIDTH))],
        out_specs=out_specs,
        out_shape=out_shape,
        compiler_params=pltpu.CompilerParams(dimension_semantics=("parallel", "parallel"),
                                             vmem_limit_bytes=_VMEM_LIMIT),
        name="inproj",
    )(x, g1, w_perm, gq, gk)


def _attn_body(q_ref, qi_ref, wi_ref, k_ref, vt_ref, ki_ref, tbl_ref, o_ref, key_ref, half_ref, sc_ref, sc2_ref, acc_ref, s_ref, s2_ref, p_ref, p2_ref, *,
               tq, kt, n_tiles, past, n_keys, topk, tbl_off, n_tbl, idx_bits):
    qb = pl.program_id(1)
    q0 = past + qb * tq
    n_kt = jnp.minimum(n_tiles, (q0 + tq + kt - 1) // kt)
    krow = lax.broadcasted_iota(jnp.int32, (kt, tq), 0)
    q_chunk = (q0 + lax.broadcasted_iota(jnp.int32, (1, tq), 1)) // CHUNK
    adm_end = jnp.minimum((q_chunk + 1) * CHUNK, n_keys)
    wi = wi_ref[...] * (IDX_HEADS ** -0.5)

    n_pair = (n_kt + 1) // 2

    def head_scores(t, dst_ref):
        ki_t = ki_ref[pl.ds(pl.multiple_of(t * kt, kt), kt), :]
        for h in range(IDX_HEADS):
            dst_ref[h] = _mm(ki_t, qi_ref[h])

    def combine(t, src_ref):
        acc = jnp.zeros((kt, tq), F32)
        for h in range(IDX_HEADS):
            acc = acc + wi[h:h + 1, :] * jnp.maximum(src_ref[h], 0.0)
        bits = pltpu.bitcast(acc, jnp.int32)
        bits = jnp.where(bits == _INT_MIN, 0, bits)
        key = jnp.where(bits >= 0, bits, bits ^ jnp.int32(0x7FFFFFFF))
        key = jnp.where(krow < adm_end - t * kt, key, _INT_MIN)
        key_ref[t] = key
        half_ref[t] = (key >> 16).astype(jnp.int16)

    head_scores(0, sc_ref)

    def score_pair(j, carry):
        t0 = 2 * j
        head_scores(t0 + 1, sc2_ref)
        combine(t0, sc_ref)
        head_scores(jnp.minimum(t0 + 2, 2 * n_pair - 2), sc_ref)
        combine(t0 + 1, sc2_ref)
        return carry

    lax.fori_loop(0, n_pair, score_pair, 0)

    def count(*pred_fns):
        def body(j, cnts):
            t = 2 * j
            k0, k1 = key_ref[t], key_ref[t + 1]
            return tuple(cnt + _fold_rows(jnp.where(fn(t, k0), 1.0, 0.0) + jnp.where(fn(t + 1, k1), 1.0, 0.0))
                         for fn, cnt in zip(pred_fns, cnts))
        cnts = lax.fori_loop(0, n_pair, body, tuple(jnp.zeros((8, tq), F32) for _ in pred_fns))
        return tuple(jnp.sum(cnt, axis=0, keepdims=True) for cnt in cnts)

    def count16(pred_fn):
        one, zero = jnp.int16(1), jnp.int16(0)
        def body(j, cnt):
            w = jnp.where(pred_fn(half_ref[2 * j]), one, zero) + jnp.where(pred_fn(half_ref[2 * j + 1]), one, zero)
            return cnt + _fold_rows(w, 16)
        cnt = lax.fori_loop(0, n_pair, body, jnp.zeros((16, tq), jnp.int16))
        return jnp.sum(cnt.astype(jnp.int32), axis=0, keepdims=True)

    def kth_largest16(kth):
        def bit(i, v):
            cand = v + lax.shift_left(jnp.int32(1), 15 - i)
            cand16 = cand.astype(jnp.int16)
            return jnp.where(count16(lambda k: k >= cand16) >= kth, cand, v)
        return lax.fori_loop(0, 16, bit, jnp.full((1, tq), -2 ** 15, jnp.int32))

    thr_hi = kth_largest16(topk)
    thr_hi16 = thr_hi.astype(jnp.int16)
    above = count16(lambda k: k > thr_hi16)

    def low_halves(t, carry):
        key = key_ref[t]
        half_ref[t] = jnp.where((key >> 16) == thr_hi, (key & 0xFFFF) - 2 ** 15, -2 ** 15).astype(jnp.int16)
        return carry

    lax.fori_loop(0, 2 * n_pair, low_halves, 0)
    thr = thr_hi * 2 ** 16 + (kth_largest16(topk - above) + 2 ** 15)
    thr = jnp.maximum(thr, _INT_MIN + 1)

    cnt_gt, cnt_ge = count(lambda t, k: k > thr, lambda t, k: k >= thr)
    need = topk - cnt_gt
    has_excess = jnp.max(jnp.where(cnt_ge > topk, 1.0, 0.0)) > 0.0

    def tie_search():
        def bit(i, last):
            cand = last + lax.shift_left(jnp.int32(1), idx_bits - 1 - i)
            below, = count(lambda t, k: (k == thr) & ((t * kt + krow) < cand))
            return jnp.where(below < need, cand, last)
        return lax.fori_loop(0, idx_bits, bit, jnp.zeros((1, tq), jnp.int32))

    last = lax.cond(has_excess, tie_search, lambda: jnp.full((1, tq), 2 ** 30, jnp.int32))

    acc_ref[...] = jnp.zeros(acc_ref.shape, F32)

    def logits(t, dst_ref):
        off = pl.multiple_of(t * kt, kt)
        for h in range(A_HEADS):
            dst_ref[h] = _mm(k_ref[h, pl.ds(off, kt), :], q_ref[h])

    def softmax_pv(t, src_ref, pr_ref, carry):
        ms, ls = carry
        off = pl.multiple_of(t * kt, kt)
        key = key_ref[t]
        sel = (key > thr) | ((key == thr) & ((off + krow) <= last))
        ti = jnp.clip(t - qb + tbl_off, 0, n_tbl - 1)
        new_ms, new_ls, alphas = [], [], []
        for h in range(A_HEADS):
            s = jnp.where(sel, src_ref[h] + tbl_ref[ti, h], _NEG)
            m_new = jnp.maximum(ms[h], jnp.max(s, axis=0, keepdims=True))
            alpha = jnp.exp2(ms[h] - m_new)
            p = jnp.exp2(s - m_new)
            new_ls.append(alpha * ls[h] + jnp.sum(p, axis=0, keepdims=True))
            new_ms.append(m_new)
            alphas.append(alpha)
            pr_ref[h] = p.astype(pr_ref.dtype)
        for h in range(A_HEADS):
            acc_ref[h] = alphas[h] * acc_ref[h] + jnp.dot(vt_ref[h, t], pr_ref[h], preferred_element_type=F32)
        return tuple(new_ms), tuple(new_ls)

    logits(0, s_ref)

    def attend_pair(j, carry):
        t0 = 2 * j
        logits(t0 + 1, s2_ref)
        carry = softmax_pv(t0, s_ref, p_ref, carry)
        logits(jnp.minimum(t0 + 2, 2 * n_pair - 2), s_ref)
        return softmax_pv(t0 + 1, s2_ref, p2_ref, carry)

    m0 = tuple(jnp.full((1, tq), _NEG, F32) for _ in range(A_HEADS))
    l0 = tuple(jnp.zeros((1, tq), F32) for _ in range(A_HEADS))
    _, ls = lax.fori_loop(0, n_pair, attend_pair, (m0, l0))
    for h in range(A_HEADS):
        o_ref[h] = acc_ref[h] / ls[h]


def _rel_bucket(rel):
    nb = N_BUCKETS // 2
    ret = jnp.where(rel > 0, nb, 0)
    n = jnp.abs(rel)
    max_exact = nb // 2
    nf = jnp.maximum(n, 1).astype(F32)
    large = max_exact + (jnp.log(nf / max_exact) / math.log(MAX_DISTANCE / max_exact)
                         * (nb - max_exact)).astype(jnp.int32)
    large = jnp.minimum(large, nb - 1)
    return ret + jnp.where(n < max_exact, n, large)


def _attn_tiling(t, n_keys):
    kt = 256
    tq = min(kt, t)
    assert t % tq == 0
    return tq, kt, t // tq, 2 * -(-n_keys // (2 * kt))


def _bias_tables(rel_bias, t, n_keys, past):
    tq, kt, nqb, n_tiles = _attn_tiling(t, n_keys)
    j = jnp.arange(kt)[:, None]
    i = jnp.arange(tq)[None, :]
    if nqb == 1:
        rels = [tile * kt + j - (past + i) for tile in range(n_tiles)]
        off = 0
    else:
        assert past == 0 and tq == kt and kt >= MAX_DISTANCE
        rels = [j - i - 2 * kt, j - i - kt, j - i]
        off = 2
    onehot = jax.nn.one_hot(_rel_bucket(jnp.stack(rels, axis=0)), N_BUCKETS, dtype=F32)
    tbl = jnp.einsum("nktb,bh->nhkt", onehot, rel_bias.astype(F32) * _LOG2E, precision=_HI)
    return tbl, off


def _attn_head_major(q_t, qi_t, wi_t, k_hm, vt_hm, ki_all, tbl, tbl_off, past, n_keys):
    bsz, _, _, t = q_t.shape
    topk = min(TOPK_MAX, n_keys // 4)
    tq, kt, nqb, n_tiles = _attn_tiling(t, n_keys)
    s_pad = n_tiles * kt
    assert s_pad >= topk and k_hm.shape[2] == s_pad and vt_hm.shape[2] == n_tiles and ki_all.shape[1] == s_pad
    n_tbl = tbl.shape[0]
    body = functools.partial(_attn_body, tq=tq, kt=kt, n_tiles=n_tiles, past=past, n_keys=n_keys, topk=topk,
                             tbl_off=tbl_off, n_tbl=n_tbl, idx_bits=int(s_pad).bit_length())
    return pl.pallas_call(
        body,
        grid=(bsz, nqb),
        in_specs=[pl.BlockSpec((None, A_HEADS, HEAD_DIM, tq), lambda b, i: (b, 0, 0, i)),
                  pl.BlockSpec((None, IDX_HEADS, IDX_DIM, tq), lambda b, i: (b, 0, 0, i)),
                  pl.BlockSpec((None, IDX_HEADS, tq), lambda b, i: (b, 0, i)),
                  pl.BlockSpec((None, A_HEADS, s_pad, HEAD_DIM), lambda b, i: (b, 0, 0, 0)),
                  pl.BlockSpec((None, A_HEADS, n_tiles, HEAD_DIM, kt), lambda b, i: (b, 0, 0, 0, 0)),
                  pl.BlockSpec((None, s_pad, IDX_DIM), lambda b, i: (b, 0, 0)),
                  _const_spec((n_tbl, A_HEADS, kt, tq), True)],
        out_specs=pl.BlockSpec((None, A_HEADS, HEAD_DIM, tq), lambda b, i: (b, 0, 0, i)),
        out_shape=jax.ShapeDtypeStruct((bsz, A_HEADS, HEAD_DIM, t), F32),
        scratch_shapes=[pltpu.VMEM((n_tiles, kt, tq), jnp.int32), pltpu.VMEM((n_tiles, kt, tq), jnp.int16),
                        pltpu.VMEM((IDX_HEADS, kt, tq), F32), pltpu.VMEM((IDX_HEADS, kt, tq), F32),
                        pltpu.VMEM((A_HEADS, HEAD_DIM, tq), F32),
                        pltpu.VMEM((A_HEADS, kt, tq), F32), pltpu.VMEM((A_HEADS, kt, tq), F32),
                        pltpu.VMEM((A_HEADS, kt, tq), _MXU_DTYPE), pltpu.VMEM((A_HEADS, kt, tq), _MXU_DTYPE)],
        compiler_params=pltpu.CompilerParams(dimension_semantics=("parallel", "parallel"),
                                             vmem_limit_bytes=_VMEM_LIMIT),
        name="attn",
    )(q_t, qi_t, wi_t, k_hm, vt_hm, ki_all, tbl)


def _attn(q, qi, wi, k_all, v_all, ki_all, tbl, tbl_off, past):
    bsz, t, _ = q.shape
    n_keys = k_all.shape[1]
    _, kt, _, n_tiles = _attn_tiling(t, n_keys)
    s_pad = n_tiles * kt
    if s_pad != n_keys:
        k_all, v_all, ki_all = (jnp.pad(a, ((0, 0), (0, s_pad - n_keys), (0, 0))) for a in (k_all, v_all, ki_all))
    q_t = q.reshape(bsz, t, A_HEADS, HEAD_DIM).transpose(0, 2, 3, 1)
    qi_t = qi.reshape(bsz, t, IDX_HEADS, IDX_DIM).transpose(0, 2, 3, 1)
    k_hm = k_all.reshape(bsz, s_pad, A_HEADS, HEAD_DIM).transpose(0, 2, 1, 3)
    vt_hm = v_all.reshape(bsz, n_tiles, kt, A_HEADS, HEAD_DIM).transpose(0, 3, 1, 4, 2)
    o_t = _attn_head_major(q_t, qi_t, wi.transpose(0, 2, 1), k_hm, vt_hm, ki_all, tbl, tbl_off, past, n_keys)
    return o_t.transpose(0, 3, 1, 2).reshape(bsz, t, A_WIDTH)


def _gla_chunk(zb, misc, wg, bg, og, st):
    c = zb.shape[0]
    dkw = B_HEADS * B_DK
    q = zb[:, 0:dkw] * (B_DK ** -0.5)
    k = zb[:, dkw:2 * dkw]
    v3 = _heads3(zb[:, 2 * dkw:2 * dkw + B_WIDTH], B_HEADS, B_DV)
    gate = zb[:, 2 * dkw + B_WIDTH:]
    log_a = jax.nn.log_sigmoid(_mmx(misc, wg) + bg) / GATE_TAU
    b = _cumsum_rows(log_a)
    b_last = b[c - 1:c, :]
    b_mid = b[c // 2 - 1:c // 2, :]
    att = _bmm_nt(_heads3(q * jnp.exp(b - b_mid), B_HEADS, B_DK), _heads3(k * jnp.exp(b_mid - b), B_HEADS, B_DK))
    ri = lax.broadcasted_iota(jnp.int32, att.shape, 1)
    cj = lax.broadcasted_iota(jnp.int32, att.shape, 2)
    att = jnp.where(ri >= cj, att, 0.0)
    o3 = _bmm_nt(_heads3(q * jnp.exp(b), B_HEADS, B_DK), st) + _bmm(att, v3)
    upd = _bmm_tn(v3, _heads3(k * jnp.exp(b_last - b), B_HEADS, B_DK))
    st_new = st * _heads3(jnp.exp(b_last), B_HEADS, B_DK) + upd
    o = _lanes2(o3 * lax.rsqrt(jnp.mean(o3 * o3, axis=-1, keepdims=True) + EPS))
    return o * og * jax.nn.silu(gate), st_new


def _gla_body(zb_ref, misc_ref, wg_ref, bg_ref, og_ref, s0_ref, o_ref, s_out_ref, s_ref):
    @pl.when(pl.program_id(1) == 0)
    def _():
        s_ref[...] = s0_ref[...]

    for g in range(zb_ref.shape[0]):
        o, st = _gla_chunk(zb_ref[g], misc_ref[g], wg_ref[...], bg_ref[...], og_ref[...], s_ref[g])
        o_ref[g] = o
        s_ref[g] = st
        s_out_ref[g] = st


def _gla(zb, misc, wg_emb, bg, og, s0_t, c):
    bsz, t, _ = zb.shape
    dkw = B_HEADS * B_DK
    g = math.gcd(_SEQS_PER_STEP, bsz)
    blk = lambda w: pl.BlockSpec((g, c, w), lambda b, i: (b, i, 0))
    st = pl.BlockSpec((g, B_HEADS, B_DV, B_DK), lambda b, i: (b, 0, 0, 0))
    return pl.pallas_call(
        _gla_body,
        grid=(bsz // g, t // c),
        in_specs=[blk(_W_B), blk(_LANES), _const_spec((_LANES, dkw)), _const_spec((1, dkw)),
                  _const_spec((1, B_WIDTH)), st],
        out_specs=[blk(B_WIDTH), st],
        out_shape=(jax.ShapeDtypeStruct((bsz, t, B_WIDTH), F32),
                   jax.ShapeDtypeStruct((bsz, B_HEADS, B_DV, B_DK), F32)),
        scratch_shapes=[pltpu.VMEM((g, B_HEADS, B_DV, B_DK), F32)],
        compiler_params=pltpu.CompilerParams(dimension_semantics=("parallel", "arbitrary"),
                                             vmem_limit_bytes=_VMEM_LIMIT),
        name="gla",
    )(zb, misc, wg_emb, bg, og, s0_t)


_PAIRS = C_HEADS // 2


def _pairs3(x):
    return jnp.stack([x[:, p * _LANES:(p + 1) * _LANES] for p in range(_PAIRS)], axis=0)


def _pair_diag(x3):
    left = lax.broadcasted_iota(jnp.int32, x3.shape, 2) < x3.shape[2] // 2
    zero = jnp.zeros((), x3.dtype)
    return jnp.concatenate([jnp.where(left, x3, zero), jnp.where(left, zero, x3)], axis=1)


def _pair_diag_wide(x3):
    left = lax.broadcasted_iota(jnp.int32, x3.shape, 2) % _LANES < _LANES // 2
    zero = jnp.zeros((), x3.dtype)
    return jnp.concatenate([jnp.where(left, x3, zero), jnp.where(left, zero, x3)], axis=1)


def _pmm(a, b):
    return jnp.einsum("pij,pjk->pik", a, b, preferred_element_type=F32)


def _pmm3(a, b):
    (ah, al), (bh, bl) = a, b
    n = ah.shape[1]
    both = _pmm(jnp.concatenate([ah, al], axis=1), _pair_diag(bh))
    return both[:, 0:n, :] + (both[:, n:, :] + _pmm(ah, _pair_diag(bl)))


def _gdn_chunk(zc_ref, misc, cw_ref, alog, dtb, og, xpad_ref, s_bd):
    c = misc.shape[0]
    w3 = 3 * C_WIDTH
    bf = lambda x: x.astype(_MXU_DTYPE)
    xpad_ref[8:8 + c, :] = zc_ref[:, 0:w3]
    conv = xpad_ref[5:5 + c, :] * cw_ref[0:1, :]
    for i in range(1, CONV_W):
        conv = conv + xpad_ref[5 + i:5 + i + c, :] * cw_ref[i:i + 1, :]
    xpad_ref[0:8, :] = xpad_ref[c:c + 8, :]
    act = jax.nn.silu(conv)
    q2 = _group_norm(act[:, 0:C_WIDTH], C_DK, False) * (C_DK ** -0.5)
    k2 = _group_norm(act[:, C_WIDTH:2 * C_WIDTH], C_DK, False)
    v2 = act[:, 2 * C_WIDTH:w3]

    g_full = -jnp.exp(alog) * jax.nn.softplus(misc + dtb)
    beta_full = jax.nn.sigmoid(misc)
    gcum = _cumsum_rows(g_full)
    gexp2 = _expand_heads(gcum, _M_CA, C_HEADS, C_DV)
    bexp2 = _expand_heads(beta_full, _M_CB, C_HEADS, C_DV)
    g_last2 = gexp2[c - 1:c, :]
    eg2 = jnp.exp(gexp2)

    gcum_t = gcum.T
    col = lambda h: jnp.broadcast_to(gcum[:, _M_CA + h:_M_CA + h + 1], (c, c))
    row = lambda h: jnp.broadcast_to(gcum_t[_M_CA + h:_M_CA + h + 1, :], (c, c))
    diff = jnp.stack([jnp.concatenate([col(2 * p) - row(2 * p), col(2 * p + 1) - row(2 * p + 1)], axis=-1)
                      for p in range(_PAIRS)], axis=0)
    ri = lax.broadcasted_iota(jnp.int32, diff.shape, 1)
    cj = lax.broadcasted_iota(jnp.int32, diff.shape, 2) % c
    lmask = jnp.where(ri >= cj, jnp.exp(jnp.minimum(diff, 0.0)), 0.0)

    kb2 = k2 * bexp2
    k_bd = _pair_diag(bf(_pairs3(k2)))
    kq = jnp.einsum("pil,pjl->pij", bf(_pairs3(jnp.concatenate([kb2, q2], axis=0))), k_bd,
                    preferred_element_type=F32)
    yield
    m = jnp.where(ri > cj, kq[:, 0:c, :] * lmask, 0.0)
    qk = kq[:, c:, :] * lmask
    inv = jnp.where(ri == cj, 1.0, 0.0) - m
    pw = _split_bf16(m)
    for _ in range(int(math.log2(c)) - 1):
        sq = _pmm3(pw, pw)
        yield
        pw = _split_bf16(sq)
        step = _pmm3(_split_bf16(inv), pw)
        yield
        inv = inv + step
    rhs = jnp.concatenate([_pairs3(v2 * bexp2), _pairs3(kb2 * eg2)], axis=-1)
    uw = _pmm(bf(inv), _pair_diag_wide(bf(rhs)))
    yield
    u, w = uw[:, :, 0:_LANES], uw[:, :, _LANES:]
    wq = _pmm(bf(jnp.concatenate([w, _pairs3(q2 * eg2)], axis=1)), bf(s_bd))
    yield
    v_new = u - wq[:, 0:c, :]
    o3 = wq[:, c:, :] + _pmm(bf(qk), _pair_diag(bf(v_new)))
    upd = jnp.einsum("pck,pcv->pkv", bf(_pairs3(k2 * jnp.exp(g_last2 - gexp2))), bf(v_new),
                     preferred_element_type=F32)
    yield
    same_head = (lax.broadcasted_iota(jnp.int32, upd.shape, 1) // C_DK) == (lax.broadcasted_iota(jnp.int32, upd.shape, 2) // C_DV)
    s_new = s_bd * _pairs3(jnp.exp(g_last2)) + jnp.where(same_head, upd, 0.0)
    o2 = jnp.concatenate([o3[p] for p in range(_PAIRS)], axis=-1)
    return _group_norm(o2, C_DV, True) * og * jax.nn.silu(zc_ref[:, w3:]), s_new


def _gdn_body(zc_ref, misc_ref, cw_ref, alog_ref, dtb_ref, og_ref, s0_ref, hist_ref,
              o_ref, s_out_ref, xpad_ref, s_ref):
    n_seq = zc_ref.shape[0]

    @pl.when(pl.program_id(1) == 0)
    def _():
        xpad_ref[:, 0:8, :] = hist_ref[...]
        zero = jnp.zeros((C_DK, C_DV), F32)
        for g in range(n_seq):
            for p in range(_PAIRS):
                s_ref[g, p] = jnp.concatenate([jnp.concatenate([s0_ref[g, 2 * p], zero], axis=-1),
                                               jnp.concatenate([zero, s0_ref[g, 2 * p + 1]], axis=-1)], axis=0)

    chunks = [_gdn_chunk(zc_ref.at[g], misc_ref[g], cw_ref, alog_ref[...], dtb_ref[...], og_ref[...],
                         xpad_ref.at[g], s_ref[g]) for g in range(n_seq)]
    for g, (o, s_new) in enumerate(_round_robin(chunks)):
        o_ref[g] = o
        s_ref[g] = s_new

    @pl.when(pl.program_id(1) == pl.num_programs(1) - 1)
    def _():
        for g in range(n_seq):
            for p in range(_PAIRS):
                s_out_ref[g, 2 * p] = s_ref[g, p, 0:C_DK, 0:C_DV]
                s_out_ref[g, 2 * p + 1] = s_ref[g, p, C_DK:, C_DV:]


def _gdn(zc, misc, conv_w, alog_emb, dtb_emb, og, s0, hist8, c):
    bsz, t, _ = zc.shape
    w3 = 3 * C_WIDTH
    g = math.gcd(_SEQS_PER_STEP, bsz)
    blk = lambda w: pl.BlockSpec((g, c, w), lambda b, i: (b, i, 0))
    st = pl.BlockSpec((g, C_HEADS, C_DK, C_DV), lambda b, i: (b, 0, 0, 0))
    return pl.pallas_call(
        _gdn_body,
        grid=(bsz // g, t // c),
        in_specs=[blk(_W_C), blk(_LANES), _const_spec((CONV_W, w3)), _const_spec((1, _LANES)),
                  _const_spec((1, _LANES)), _const_spec((1, C_WIDTH)), st,
                  pl.BlockSpec((g, 8, w3), lambda b, i: (b, 0, 0))],
        out_specs=[blk(C_WIDTH), st],
        out_shape=(jax.ShapeDtypeStruct((bsz, t, C_WIDTH), F32),
                   jax.ShapeDtypeStruct((bsz, C_HEADS, C_DK, C_DV), F32)),
        scratch_shapes=[pltpu.VMEM((g, 8 + c, w3), F32), pltpu.VMEM((g, _PAIRS, 2 * C_DK, 2 * C_DV), F32)],
        compiler_params=pltpu.CompilerParams(dimension_semantics=("parallel", "arbitrary"),
                                             vmem_limit_bytes=_VMEM_LIMIT),
        name="gdn",
    )(zc, misc, conv_w, alog_emb, dtb_emb, og, s0, hist8)


def _mlp_body(x_ref, oa_ref, ob_ref, oc_ref, wo_ref, g2_ref, wup_ref, wdn_ref, y_ref, *, ff_tile, head_major):
    if head_major:
        oa = jnp.concatenate([oa_ref[h].T for h in range(A_HEADS)], axis=-1)
    else:
        oa = oa_ref[...]
    mixed = jnp.concatenate([oa, ob_ref[...], oc_ref[...]], axis=-1)
    y_ref[...] = x_ref[...] + _mm(mixed, wo_ref[...])
    h2 = (_rms(y_ref[...]) * g2_ref[...]).astype(_MXU_DTYPE)
    for j in range(D_FF // ff_tile):
        up = jnp.dot(h2, wup_ref[:, j * ff_tile:(j + 1) * ff_tile], preferred_element_type=F32)
        y_ref[...] += _mm(jnp.square(jnp.maximum(up, 0.0)), wdn_ref[j * ff_tile:(j + 1) * ff_tile, :])


_MLP_WEIGHT_SPECS = lambda: [_const_spec((D_MODEL, D_MODEL), True), _const_spec((1, D_MODEL)),
                             _const_spec((D_MODEL, D_FF), True), _const_spec((D_FF, D_MODEL), True)]


def _mlp(x2d, oa, ob, oc, wo, g2, wup, wdn):
    n = x2d.shape[0]
    tm = min(_ROW_TILE, n)
    assert n % tm == 0
    row = lambda w: pl.BlockSpec((tm, w), lambda i: (i, 0))
    return pl.pallas_call(
        functools.partial(_mlp_body, ff_tile=1024, head_major=False),
        grid=(n // tm,),
        in_specs=[row(D_MODEL), row(A_WIDTH), row(B_WIDTH), row(C_WIDTH)] + _MLP_WEIGHT_SPECS(),
        out_specs=row(D_MODEL),
        out_shape=jax.ShapeDtypeStruct((n, D_MODEL), F32),
        compiler_params=pltpu.CompilerParams(dimension_semantics=("parallel",), vmem_limit_bytes=_VMEM_LIMIT),
        name="mlp",
    )(x2d, oa, ob, oc, wo, g2, wup, wdn)


def _mlp_head_major(x, oa_t, ob, oc, wo, g2, wup, wdn):
    bsz, t, _ = x.shape
    tm = min(_ROW_TILE, t)
    assert t % tm == 0
    row = lambda w: pl.BlockSpec((None, tm, w), lambda b, i: (b, i, 0))
    return pl.pallas_call(
        functools.partial(_mlp_body, ff_tile=1024, head_major=True),
        grid=(bsz, t // tm),
        in_specs=[row(D_MODEL), pl.BlockSpec((None, A_HEADS, HEAD_DIM, tm), lambda b, i: (b, 0, 0, i)),
                  row(B_WIDTH), row(C_WIDTH)] + _MLP_WEIGHT_SPECS(),
        out_specs=row(D_MODEL),
        out_shape=jax.ShapeDtypeStruct((bsz, t, D_MODEL), F32),
        compiler_params=pltpu.CompilerParams(dimension_semantics=("parallel", "parallel"),
                                             vmem_limit_bytes=_VMEM_LIMIT),
        name="mlp",
    )(x, oa_t, ob, oc, wo, g2, wup, wdn)


def _prep_layer_params(ln1_g, w_in, a_qnorm_g, a_knorm_g, gla_w_gate, gla_b_gate, gla_onorm_g,
                       conv_w, gdn_a_log, gdn_dt_bias, gdn_onorm_g, w_o, ln2_g, w_up, w_down):
    cols = [jnp.zeros((D_MODEL, _PAD[name]), w_in.dtype) if name in _PAD
            else w_in[:, _SRC[name]:_SRC[name] + _SIZE[name]] for name in _NEW_ORDER]
    lane_emb = lambda vec, start: jnp.zeros((1, _LANES), F32).at[0, start:start + vec.shape[0]].set(vec.astype(F32))
    return dict(
        g1=ln1_g.reshape(1, D_MODEL).astype(F32),
        w_perm=jnp.concatenate(cols, axis=1).astype(_MXU_DTYPE),
        gq=jnp.tile(a_qnorm_g.astype(F32), A_HEADS).reshape(1, A_WIDTH),
        gk=jnp.tile(a_knorm_g.astype(F32), A_HEADS).reshape(1, A_WIDTH),
        wg_emb=jnp.zeros((_LANES, B_HEADS * B_DK), F32).at[_M_GLR:_M_GLR + GATE_RANK].set(gla_w_gate.astype(F32)),
        bg=gla_b_gate.reshape(1, -1).astype(F32),
        og_b=jnp.tile(gla_onorm_g.astype(F32), B_HEADS).reshape(1, B_WIDTH),
        conv_w=conv_w.astype(F32),
        alog_emb=lane_emb(gdn_a_log, _M_CA),
        dtb_emb=lane_emb(gdn_dt_bias, _M_CA),
        og_c=jnp.tile(gdn_onorm_g.astype(F32), C_HEADS).reshape(1, C_WIDTH),
        wo=w_o.astype(_MXU_DTYPE), g2=ln2_g.reshape(1, D_MODEL).astype(F32),
        wup=w_up.astype(_MXU_DTYPE), wdn=w_down.astype(_MXU_DTYPE),
    )


def _layer(x, cache, p, tbl, tbl_off):
    bsz, t, _ = x.shape
    n = bsz * t
    assert t >= CONV_W - 1
    c = min(CHUNK, t)
    r3 = lambda a: a.reshape(bsz, t, a.shape[-1])
    mixer_b = lambda zb, misc, s0: _gla(zb, misc, p["wg_emb"], p["bg"], p["og_b"], s0, c)
    mixer_c = lambda zc, misc, s0, hist8: _gdn(zc, misc, p["conv_w"], p["alog_emb"], p["dtb_emb"], p["og_c"], s0, hist8, c)

    if cache is None:
        _, kt, _, n_tiles = _attn_tiling(t, t)
        assert n_tiles * kt == t
        q_t, qi_t, wi_t, k_hm, vt_hm, ki16, ka, va, misc, zb, zc = _inproj_head_major(
            x, p["g1"], p["w_perm"], p["gq"], p["gk"], kt)
        o_a_t = _attn_head_major(q_t, qi_t, wi_t, k_hm, vt_hm, ki16, tbl, tbl_off, 0, t)
        o_b, s_gla_t = mixer_b(zb, misc, jnp.zeros((bsz, B_HEADS, B_DV, B_DK), F32))
        o_c, s_gdn = mixer_c(zc, misc, jnp.zeros((bsz, C_HEADS, C_DK, C_DV), F32), jnp.zeros((bsz, 8, 3 * C_WIDTH), F32))
        y = _mlp_head_major(x, o_a_t, o_b, o_c, p["wo"], p["g2"], p["wup"], p["wdn"])
    else:
        ck, cv, cki, sg, sd, cbuf = cache
        past = ck.shape[1]
        x2d = x.reshape(n, D_MODEL)
        qa, qi, ka, va, misc, zb, zc = _inproj(x2d, p["g1"], p["w_perm"], p["gq"], p["gk"])
        ka, va, misc, zb, zc = r3(ka), r3(va), r3(misc), r3(zb), r3(zc)
        k_all = jnp.concatenate([ck.reshape(bsz, past, A_WIDTH).astype(F32), ka], axis=1)
        v_all = jnp.concatenate([cv.reshape(bsz, past, A_WIDTH).astype(F32), va], axis=1)
        ki_all = jnp.concatenate([cki.astype(F32), misc[..., 0:IDX_DIM]], axis=1)
        o_a = _attn(r3(qa), r3(qi), misc[..., _M_WI:_M_WI + IDX_HEADS], k_all.astype(_MXU_DTYPE),
                    v_all.astype(_MXU_DTYPE), ki_all.astype(_MXU_DTYPE), tbl, tbl_off, past)
        o_b, s_gla_t = mixer_b(zb, misc, jnp.swapaxes(sg.astype(F32), 2, 3))
        hist8 = jnp.pad(cbuf.astype(F32), ((0, 0), (8 - (CONV_W - 1), 0), (0, 0)))
        o_c, s_gdn = mixer_c(zc, misc, sd.astype(F32), hist8)
        y = _mlp(x2d, o_a.reshape(n, A_WIDTH), o_b.reshape(n, B_WIDTH), o_c.reshape(n, C_WIDTH),
                 p["wo"], p["g2"], p["wup"], p["wdn"]).reshape(bsz, t, D_MODEL)

    s_gla = jnp.swapaxes(s_gla_t, 2, 3)
    new_conv = zc[:, t - (CONV_W - 1):, 0:3 * C_WIDTH]
    state = (ka.reshape(bsz, t, A_HEADS, HEAD_DIM), va.reshape(bsz, t, A_HEADS, HEAD_DIM), misc[..., 0:IDX_DIM],
             s_gla, s_gdn, new_conv)
    return y, state


def kernel(x_prompt, x_sample, cache_a_k, cache_a_v, cache_a_kidx, state_gla, state_gdn, state_conv, rel_bias, ln1_g, w_in, a_qnorm_g, a_knorm_g, gla_w_gate, gla_b_gate, gla_onorm_g, conv_w, gdn_a_log, gdn_dt_bias, gdn_onorm_g, w_o, ln2_g, w_up, w_down):
    depth = ln1_g.shape[0]
    yp, ys = x_prompt, x_sample
    past = cache_a_k.shape[2]
    tbl_p = _bias_tables(rel_bias, x_prompt.shape[1], x_prompt.shape[1], 0)
    tbl_s = _bias_tables(rel_bias, x_sample.shape[1], past + x_sample.shape[1], past)
    new_p, new_s = [], []
    for l in range(depth):
        p = _prep_layer_params(ln1_g[l], w_in[l], a_qnorm_g[l], a_knorm_g[l], gla_w_gate[l],
                               gla_b_gate[l], gla_onorm_g[l], conv_w[l], gdn_a_log[l], gdn_dt_bias[l],
                               gdn_onorm_g[l], w_o[l], ln2_g[l], w_up[l], w_down[l])
        yp, st_p = _layer(yp, None, p, *tbl_p)
        cache_l = (cache_a_k[l], cache_a_v[l], cache_a_kidx[l], state_gla[l], state_gdn[l], state_conv[l])
        ys, st_s = _layer(ys, cache_l, p, *tbl_s)
        new_p.append(st_p)
        new_s.append(st_s)
    stack = lambda states, i: jnp.stack([s[i] for s in states], axis=0)
    return (yp, ys) + tuple(stack(new_p, i) for i in range(6)) + tuple(stack(new_s, i) for i in range(6))
```

```python
import functools
import math

import numpy as np
import jax
import jax.numpy as jnp
from jax import lax
from jax.experimental import pallas as pl
from jax.experimental.pallas import tpu as pltpu

D_MODEL = 1024
CHUNK = 64
HEAD_DIM = 64
A_HEADS = 4
A_WIDTH = A_HEADS * HEAD_DIM
IDX_HEADS = 8
IDX_DIM = 64
TOPK_MAX = 256
N_BUCKETS = 32
MAX_DISTANCE = 128
B_HEADS = 4
B_DK = 32
B_DV = 64
B_WIDTH = B_HEADS * B_DV
GATE_RANK = 16
GATE_TAU = 16.0
C_HEADS = 8
C_DK = 64
C_DV = 64
C_WIDTH = C_HEADS * C_DV
CONV_W = 4
D_FF = 4 * D_MODEL
EPS = 1e-6

F32 = jnp.float32
BF16 = jnp.bfloat16
_MXU_DTYPE = BF16
_HI = lax.Precision.HIGHEST
_VMEM_LIMIT = 56 * 1024 * 1024
_LANES = 128
_Q_TILE = 512
_ROW_TILE = 512
_SEQS_PER_STEP = 8
_NEG = -1e30
_LOG2E = math.log2(math.e)
_INT_MIN = -2 ** 31

_SPLIT_NAMES = ("aq", "ak", "av", "aqi", "aki", "awi", "bq", "bk", "bv", "bglr", "bog",
                "cq", "ck", "cv", "ca", "cb", "cog")
_SPLIT_SIZES = (A_WIDTH, A_WIDTH, A_WIDTH, IDX_HEADS * IDX_DIM, IDX_DIM, IDX_HEADS,
                B_HEADS * B_DK, B_HEADS * B_DK, B_WIDTH, GATE_RANK, B_WIDTH,
                C_HEADS * C_DK, C_HEADS * C_DK, C_WIDTH, C_HEADS, C_HEADS, C_WIDTH)
D_IN = sum(_SPLIT_SIZES)
_SRC = dict(zip(_SPLIT_NAMES, np.concatenate([[0], np.cumsum(_SPLIT_SIZES)[:-1]]).tolist()))
_SIZE = dict(zip(_SPLIT_NAMES, _SPLIT_SIZES))

_M_WI, _M_GLR, _M_CA, _M_CB = 64, 72, 88, 96
_NEW_ORDER = ("aq", "ak", "av", "aqi", "aki", "awi", "bglr", "ca", "cb", "pad24",
              "bq", "bk", "bv", "bog", "cq", "ck", "cv", "cog")
_PAD = {"pad24": 24}


def _new_offsets():
    offs, pos = {}, 0
    for name in _NEW_ORDER:
        offs[name] = pos
        pos += _PAD.get(name, 0) or _SIZE[name]
    return offs, pos


_OFF, D_Z = _new_offsets()
_C_QA, _C_KA, _C_VA, _C_QI, _C_MISC = _OFF["aq"], _OFF["ak"], _OFF["av"], _OFF["aqi"], _OFF["aki"]
_C_B, _C_C = _OFF["bq"], _OFF["cq"]
_W_B = 2 * B_HEADS * B_DK + 2 * B_WIDTH
_W_C = 4 * C_WIDTH
assert _C_MISC % _LANES == 0 and _C_B == _C_MISC + _LANES and _C_C == _C_B + _W_B and D_Z == _C_C + _W_C
assert (_OFF["awi"], _OFF["bglr"], _OFF["ca"], _OFF["cb"]) == tuple(_C_MISC + m for m in (_M_WI, _M_GLR, _M_CA, _M_CB))


def _mm(a, b):
    return jnp.dot(a.astype(_MXU_DTYPE), b.astype(_MXU_DTYPE), preferred_element_type=F32)


def _mmx(a, b):
    return jnp.dot(a, b, preferred_element_type=F32, precision=_HI)


def _split_bf16(x):
    hi = pltpu.bitcast(pltpu.bitcast(x, jnp.int32) & jnp.int32(-65536), F32)
    return hi.astype(BF16), (x - hi).astype(BF16)


def _cumsum_rows(x):
    r = lax.broadcasted_iota(jnp.int32, x.shape, 0)
    s = 1
    while s < x.shape[0]:
        x = x + jnp.where(r >= s, pltpu.roll(x, s, 0), 0.0)
        s *= 2
    return x


def _heads3(x, n_heads, width):
    return jnp.stack([x[:, h * width:(h + 1) * width] for h in range(n_heads)], axis=0)


def _lanes2(x3):
    return jnp.concatenate([x3[h] for h in range(x3.shape[0])], axis=-1)


def _expand_heads(x, lane0, n_heads, width):
    c = x.shape[0]
    return jnp.concatenate([jnp.broadcast_to(x[:, lane0 + h:lane0 + h + 1], (c, width)) for h in range(n_heads)],
                           axis=-1)


def _bmm(a, b):
    return jnp.einsum("hij,hjk->hik", a.astype(_MXU_DTYPE), b.astype(_MXU_DTYPE), preferred_element_type=F32)


def _bmm_nt(a, b):
    return jnp.einsum("hik,hjk->hij", a.astype(_MXU_DTYPE), b.astype(_MXU_DTYPE), preferred_element_type=F32)


def _bmm_tn(a, b):
    return jnp.einsum("hck,hcv->hkv", a.astype(_MXU_DTYPE), b.astype(_MXU_DTYPE), preferred_element_type=F32)


def _rms(x):
    return x * lax.rsqrt(jnp.mean(x * x, axis=-1, keepdims=True) + EPS)


def _group_norm(x, group, mean):
    outs = []
    for g in range(x.shape[-1] // group):
        xs = x[:, g * group:(g + 1) * group]
        ss = jnp.sum(xs * xs, axis=-1, keepdims=True)
        outs.append(xs * lax.rsqrt((ss / group if mean else ss) + EPS))
    return jnp.concatenate(outs, axis=-1)


def _fold_rows(w, rows=8):
    parts = [w[i * rows:(i + 1) * rows, :] for i in range(w.shape[0] // rows)]
    while len(parts) > 1:
        parts = [parts[i] + parts[i + 1] for i in range(0, len(parts), 2)]
    return parts[0]


def _round_robin(gens):
    results = [None] * len(gens)
    live = list(range(len(gens)))
    while live:
        for i in list(live):
            try:
                next(gens[i])
            except StopIteration as stop:
                results[i] = stop.value
                live.remove(i)
    return results


def _const_spec(shape, single_buffer=False):
    zeros = (0,) * len(shape)
    if single_buffer:
        return pl.BlockSpec(shape, lambda *_: zeros, pipeline_mode=pl.Buffered(1))
    return pl.BlockSpec(shape, lambda *_: zeros)


def _inproj_body(x_ref, g1_ref, w_ref, gq_ref, gk_ref, *out_refs, head_major, kt):
    h = (_rms(x_ref[...]) * g1_ref[...]).astype(_MXU_DTYPE)

    def proj(c0, width):
        return jnp.dot(h, w_ref[:, c0:c0 + width], preferred_element_type=F32)

    qn = _group_norm(proj(_C_QA, A_WIDTH), HEAD_DIM, True) * gq_ref[...] * (HEAD_DIM ** -0.5 * _LOG2E)
    kn = _group_norm(proj(_C_KA, A_WIDTH), HEAD_DIM, True) * gk_ref[...]
    v = proj(_C_VA, A_WIDTH)
    qi = proj(_C_QI, IDX_HEADS * IDX_DIM) * (IDX_DIM ** -0.5)
    misc = proj(_C_MISC, _LANES)
    if head_major:
        qt_ref, qit_ref, wit_ref, khm_ref, vt_ref, ki_ref, ka_ref, va_ref, misc_ref, zb_ref, zc_ref = out_refs
        for hd in range(A_HEADS):
            hs = slice(hd * HEAD_DIM, (hd + 1) * HEAD_DIM)
            qt_ref[hd] = qn[:, hs].T.astype(qt_ref.dtype)
            khm_ref[hd] = kn[:, hs].astype(khm_ref.dtype)
            for j in range(v.shape[0] // kt):
                vt_ref[hd, j] = v[j * kt:(j + 1) * kt, hs].T.astype(vt_ref.dtype)
        for hd in range(IDX_HEADS):
            qit_ref[hd] = qi[:, hd * IDX_DIM:(hd + 1) * IDX_DIM].T.astype(qit_ref.dtype)
        wit_ref[...] = misc.T[_M_WI:_M_WI + IDX_HEADS, :]
        ki_ref[...] = misc[:, 0:IDX_DIM].astype(ki_ref.dtype)
    else:
        qa_ref, qi_ref, ka_ref, va_ref, misc_ref, zb_ref, zc_ref = out_refs
        qa_ref[...] = qn.astype(qa_ref.dtype)
        qi_ref[...] = qi.astype(qi_ref.dtype)
    ka_ref[...] = kn
    va_ref[...] = v
    misc_ref[...] = misc
    zb_ref[...] = proj(_C_B, _W_B)
    zc_ref[...] = proj(_C_C, _W_C)


def _inproj(x2d, g1, w_perm, gq, gk):
    n = x2d.shape[0]
    tm = min(_ROW_TILE, n)
    assert n % tm == 0
    row = lambda w: pl.BlockSpec((tm, w), lambda i: (i, 0))
    widths = (A_WIDTH, IDX_HEADS * IDX_DIM, A_WIDTH, A_WIDTH, _LANES, _W_B, _W_C)
    dtypes = (_MXU_DTYPE, _MXU_DTYPE, F32, F32, F32, F32, F32)
    return pl.pallas_call(
        functools.partial(_inproj_body, head_major=False, kt=None),
        grid=(n // tm,),
        in_specs=[row(D_MODEL), _const_spec((1, D_MODEL)), _const_spec((D_MODEL, D_Z), True),
                  _const_spec((1, A_WIDTH)), _const_spec((1, A_WIDTH))],
        out_specs=[row(w) for w in widths],
        out_shape=tuple(jax.ShapeDtypeStruct((n, w), d) for w, d in zip(widths, dtypes)),
        compiler_params=pltpu.CompilerParams(dimension_semantics=("parallel",), vmem_limit_bytes=_VMEM_LIMIT),
        name="inproj",
    )(x2d, g1, w_perm, gq, gk)


def _inproj_head_major(x, g1, w_perm, gq, gk, kt):
    bsz, t, _ = x.shape
    tm = min(_ROW_TILE, t)
    assert t % tm == 0 and tm % kt == 0
    row = lambda w: pl.BlockSpec((None, tm, w), lambda b, i: (b, i, 0))
    rows = lambda w, d: jax.ShapeDtypeStruct((bsz, t, w), d)
    lanes = lambda nh: pl.BlockSpec((None, nh, HEAD_DIM, tm), lambda b, i: (b, 0, 0, i))
    out_specs = [lanes(A_HEADS), lanes(IDX_HEADS), pl.BlockSpec((None, IDX_HEADS, tm), lambda b, i: (b, 0, i)),
                 pl.BlockSpec((None, A_HEADS, tm, HEAD_DIM), lambda b, i: (b, 0, i, 0)),
                 pl.BlockSpec((None, A_HEADS, tm // kt, HEAD_DIM, kt), lambda b, i: (b, 0, i, 0, 0)),
                 row(IDX_DIM), row(A_WIDTH), row(A_WIDTH), row(_LANES), row(_W_B), row(_W_C)]
    out_shape = (jax.ShapeDtypeStruct((bsz, A_HEADS, HEAD_DIM, t), _MXU_DTYPE),
                 jax.ShapeDtypeStruct((bsz, IDX_HEADS, IDX_DIM, t), _MXU_DTYPE),
                 jax.ShapeDtypeStruct((bsz, IDX_HEADS, t), F32),
                 jax.ShapeDtypeStruct((bsz, A_HEADS, t, HEAD_DIM), _MXU_DTYPE),
                 jax.ShapeDtypeStruct((bsz, A_HEADS, t // kt, HEAD_DIM, kt), _MXU_DTYPE),
                 rows(IDX_DIM, _MXU_DTYPE), rows(A_WIDTH, F32), rows(A_WIDTH, F32), rows(_LANES, F32),
                 rows(_W_B, F32), rows(_W_C, F32))
    return pl.pallas_call(
        functools.partial(_inproj_body, head_major=True, kt=kt),
        grid=(bsz, t // tm),
        in_specs=[row(D_MODEL), _const_spec((1, D_MODEL)), _const_spec((D_MODEL, D_Z), True),
                  _const_spec((1, A_WIDTH)), _const_spec((1, A_WIDTH))],
        out_specs=out_specs,
        out_shape=out_shape,
        compiler_params=pltpu.CompilerParams(dimension_semantics=("parallel", "parallel"),
                                             vmem_limit_bytes=_VMEM_LIMIT),
        name="inproj",
    )(x, g1, w_perm, gq, gk)


def _attn_body(q_ref, qi_ref, wi_ref, k_ref, vt_ref, ki_ref, tbl_ref, o_ref, key_ref, half_ref, sc_ref, sc2_ref, acc_ref, s_ref, s2_ref, p_ref, p2_ref, *,
               tq, kt, n_tiles, past, n_keys, topk, tbl_off, n_tbl, idx_bits):
    qb = pl.program_id(1)
    q0 = past + qb * tq
    n_kt = jnp.minimum(n_tiles, (q0 + tq + kt - 1) // kt)
    krow = lax.broadcasted_iota(jnp.int32, (kt, tq), 0)
    q_chunk = (q0 + lax.broadcasted_iota(jnp.int32, (1, tq), 1)) // CHUNK
    adm_end = jnp.minimum((q_chunk + 1) * CHUNK, n_keys)
    wi = wi_ref[...] * (IDX_HEADS ** -0.5)

    n_pair = (n_kt + 1) // 2

    def head_scores(t, dst_ref):
        ki_t = ki_ref[pl.ds(pl.multiple_of(t * kt, kt), kt), :]
        for h in range(IDX_HEADS):
            dst_ref[h] = _mm(ki_t, qi_ref[h])

    def combine(t, src_ref):
        acc = jnp.zeros((kt, tq), F32)
        for h in range(IDX_HEADS):
            acc = acc + wi[h:h + 1, :] * jnp.maximum(src_ref[h], 0.0)
        bits = pltpu.bitcast(acc, jnp.int32)
        bits = jnp.where(bits == _INT_MIN, 0, bits)
        key = jnp.where(bits >= 0, bits, bits ^ jnp.int32(0x7FFFFFFF))
        key = jnp.where(krow < adm_end - t * kt, key, _INT_MIN)
        key_ref[t] = key
        half_ref[t] = (key >> 16).astype(jnp.int16)

    head_scores(0, sc_ref)

    def score_pair(j, carry):
        t0 = 2 * j
        head_scores(t0 + 1, sc2_ref)
        combine(t0, sc_ref)
        head_scores(jnp.minimum(t0 + 2, 2 * n_pair - 2), sc_ref)
        combine(t0 + 1, sc2_ref)
        return carry

    lax.fori_loop(0, n_pair, score_pair, 0)

    def count(*pred_fns):
        def body(j, cnts):
            t = 2 * j
            k0, k1 = key_ref[t], key_ref[t + 1]
            return tuple(cnt + _fold_rows(jnp.where(fn(t, k0), 1.0, 0.0) + jnp.where(fn(t + 1, k1), 1.0, 0.0))
                         for fn, cnt in zip(pred_fns, cnts))
        cnts = lax.fori_loop(0, n_pair, body, tuple(jnp.zeros((8, tq), F32) for _ in pred_fns))
        return tuple(jnp.sum(cnt, axis=0, keepdims=True) for cnt in cnts)

    def count16(pred_fn):
        one, zero = jnp.int16(1), jnp.int16(0)
        def body(j, cnt):
            w = jnp.where(pred_fn(half_ref[2 * j]), one, zero) + jnp.where(pred_fn(half_ref[2 * j + 1]), one, zero)
            return cnt + _fold_rows(w, 16)
        cnt = lax.fori_loop(0, n_pair, body, jnp.zeros((16, tq), jnp.int16))
        return jnp.sum(cnt.astype(jnp.int32), axis=0, keepdims=True)

    def kth_largest16(kth):
        def bit(i, v):
            cand = v + lax.shift_left(jnp.int32(1), 15 - i)
            cand16 = cand.astype(jnp.int16)
            return jnp.where(count16(lambda k: k >= cand16) >= kth, cand, v)
        return lax.fori_loop(0, 16, bit, jnp.full((1, tq), -2 ** 15, jnp.int32))

    thr_hi = kth_largest16(topk)
    thr_hi16 = thr_hi.astype(jnp.int16)
    above = count16(lambda k: k > thr_hi16)

    def low_halves(t, carry):
        key = key_ref[t]
        half_ref[t] = jnp.where((key >> 16) == thr_hi, (key & 0xFFFF) - 2 ** 15, -2 ** 15).astype(jnp.int16)
        return carry

    lax.fori_loop(0, 2 * n_pair, low_halves, 0)
    thr = thr_hi * 2 ** 16 + (kth_largest16(topk - above) + 2 ** 15)
    thr = jnp.maximum(thr, _INT_MIN + 1)

    cnt_gt, cnt_ge = count(lambda t, k: k > thr, lambda t, k: k >= thr)
    need = topk - cnt_gt
    has_excess = jnp.max(jnp.where(cnt_ge > topk, 1.0, 0.0)) > 0.0

    def tie_search():
        def bit(i, last):
            cand = last + lax.shift_left(jnp.int32(1), idx_bits - 1 - i)
            below, = count(lambda t, k: (k == thr) & ((t * kt + krow) < cand))
            return jnp.where(below < need, cand, last)
        return lax.fori_loop(0, idx_bits, bit, jnp.zeros((1, tq), jnp.int32))

    last = lax.cond(has_excess, tie_search, lambda: jnp.full((1, tq), 2 ** 30, jnp.int32))

    acc_ref[...] = jnp.zeros(acc_ref.shape, F32)

    def logits(t, dst_ref):
        off = pl.multiple_of(t * kt, kt)
        for h in range(A_HEADS):
            dst_ref[h] = _mm(k_ref[h, pl.ds(off, kt), :], q_ref[h])

    def softmax_pv(t, src_ref, pr_ref, carry):
        ms, ls = carry
        off = pl.multiple_of(t * kt, kt)
        key = key_ref[t]
        sel = (key > thr) | ((key == thr) & ((off + krow) <= last))
        ti = jnp.clip(t - qb * (tq // kt) + tbl_off, 0, n_tbl - 1)
        new_ms, new_ls, alphas = [], [], []
        for h in range(A_HEADS):
            s = jnp.where(sel, src_ref[h] + tbl_ref[ti, h], _NEG)
            m_new = jnp.maximum(ms[h], jnp.max(s, axis=0, keepdims=True))
            alpha = jnp.exp2(ms[h] - m_new)
            p = jnp.exp2(s - m_new)
            new_ls.append(alpha * ls[h] + jnp.sum(p, axis=0, keepdims=True))
            new_ms.append(m_new)
            alphas.append(alpha)
            pr_ref[h] = p.astype(pr_ref.dtype)
        for h in range(A_HEADS):
            acc_ref[h] = alphas[h] * acc_ref[h] + jnp.dot(vt_ref[h, t], pr_ref[h], preferred_element_type=F32)
        return tuple(new_ms), tuple(new_ls)

    logits(0, s_ref)

    def attend_pair(j, carry):
        t0 = 2 * j
        logits(t0 + 1, s2_ref)
        carry = softmax_pv(t0, s_ref, p_ref, carry)
        logits(jnp.minimum(t0 + 2, 2 * n_pair - 2), s_ref)
        return softmax_pv(t0 + 1, s2_ref, p2_ref, carry)

    m0 = tuple(jnp.full((1, tq), _NEG, F32) for _ in range(A_HEADS))
    l0 = tuple(jnp.zeros((1, tq), F32) for _ in range(A_HEADS))
    _, ls = lax.fori_loop(0, n_pair, attend_pair, (m0, l0))
    for h in range(A_HEADS):
        o_ref[h] = acc_ref[h] / ls[h]


def _rel_bucket(rel):
    nb = N_BUCKETS // 2
    ret = jnp.where(rel > 0, nb, 0)
    n = jnp.abs(rel)
    max_exact = nb // 2
    nf = jnp.maximum(n, 1).astype(F32)
    large = max_exact + (jnp.log(nf / max_exact) / math.log(MAX_DISTANCE / max_exact)
                         * (nb - max_exact)).astype(jnp.int32)
    large = jnp.minimum(large, nb - 1)
    return ret + jnp.where(n < max_exact, n, large)


def _attn_tiling(t, n_keys):
    kt = 256
    tq = min(_Q_TILE, t)
    assert t % tq == 0
    return tq, kt, t // tq, 2 * -(-n_keys // (2 * kt))


def _bias_tables(rel_bias, t, n_keys, past):
    tq, kt, nqb, n_tiles = _attn_tiling(t, n_keys)
    j = jnp.arange(kt)[:, None]
    i = jnp.arange(tq)[None, :]
    if nqb == 1:
        rels = [tile * kt + j - (past + i) for tile in range(n_tiles)]
        off = 0
    else:
        assert past == 0 and tq % kt == 0 and kt >= MAX_DISTANCE
        rels = [j - i + d * kt for d in range(-2, tq // kt)]
        off = 2
    onehot = jax.nn.one_hot(_rel_bucket(jnp.stack(rels, axis=0)), N_BUCKETS, dtype=F32)
    tbl = jnp.einsum("nktb,bh->nhkt", onehot, rel_bias.astype(F32) * _LOG2E, precision=_HI)
    return tbl, off


def _attn_head_major(q_t, qi_t, wi_t, k_hm, vt_hm, ki_all, tbl, tbl_off, past, n_keys):
    bsz, _, _, t = q_t.shape
    topk = min(TOPK_MAX, n_keys // 4)
    tq, kt, nqb, n_tiles = _attn_tiling(t, n_keys)
    s_pad = n_tiles * kt
    assert s_pad >= topk and k_hm.shape[2] == s_pad and vt_hm.shape[2] == n_tiles and ki_all.shape[1] == s_pad
    n_tbl = tbl.shape[0]
    body = functools.partial(_attn_body, tq=tq, kt=kt, n_tiles=n_tiles, past=past, n_keys=n_keys, topk=topk,
                             tbl_off=tbl_off, n_tbl=n_tbl, idx_bits=int(s_pad).bit_length())
    return pl.pallas_call(
        body,
        grid=(bsz, nqb),
        in_specs=[pl.BlockSpec((None, A_HEADS, HEAD_DIM, tq), lambda b, i: (b, 0, 0, i)),
                  pl.BlockSpec((None, IDX_HEADS, IDX_DIM, tq), lambda b, i: (b, 0, 0, i)),
                  pl.BlockSpec((None, IDX_HEADS, tq), lambda b, i: (b, 0, i)),
                  pl.BlockSpec((None, A_HEADS, s_pad, HEAD_DIM), lambda b, i: (b, 0, 0, 0)),
                  pl.BlockSpec((None, A_HEADS, n_tiles, HEAD_DIM, kt), lambda b, i: (b, 0, 0, 0, 0)),
                  pl.BlockSpec((None, s_pad, IDX_DIM), lambda b, i: (b, 0, 0)),
                  _const_spec((n_tbl, A_HEADS, kt, tq), True)],
        out_specs=pl.BlockSpec((None, A_HEADS, HEAD_DIM, tq), lambda b, i: (b, 0, 0, i)),
        out_shape=jax.ShapeDtypeStruct((bsz, A_HEADS, HEAD_DIM, t), F32),
        scratch_shapes=[pltpu.VMEM((n_tiles, kt, tq), jnp.int32), pltpu.VMEM((n_tiles, kt, tq), jnp.int16),
                        pltpu.VMEM((IDX_HEADS, kt, tq), F32), pltpu.VMEM((IDX_HEADS, kt, tq), F32),
                        pltpu.VMEM((A_HEADS, HEAD_DIM, tq), F32),
                        pltpu.VMEM((A_HEADS, kt, tq), F32), pltpu.VMEM((A_HEADS, kt, tq), F32),
                        pltpu.VMEM((A_HEADS, kt, tq), _MXU_DTYPE), pltpu.VMEM((A_HEADS, kt, tq), _MXU_DTYPE)],
        compiler_params=pltpu.CompilerParams(dimension_semantics=("parallel", "parallel"),
                                             vmem_limit_bytes=_VMEM_LIMIT),
        name="attn",
    )(q_t, qi_t, wi_t, k_hm, vt_hm, ki_all, tbl)


def _attn(q, qi, wi, k_all, v_all, ki_all, tbl, tbl_off, past):
    bsz, t, _ = q.shape
    n_keys = k_all.shape[1]
    _, kt, _, n_tiles = _attn_tiling(t, n_keys)
    s_pad = n_tiles * kt
    if s_pad != n_keys:
        k_all, v_all, ki_all = (jnp.pad(a, ((0, 0), (0, s_pad - n_keys), (0, 0))) for a in (k_all, v_all, ki_all))
    q_t = q.reshape(bsz, t, A_HEADS, HEAD_DIM).transpose(0, 2, 3, 1)
    qi_t = qi.reshape(bsz, t, IDX_HEADS, IDX_DIM).transpose(0, 2, 3, 1)
    k_hm = k_all.reshape(bsz, s_pad, A_HEADS, HEAD_DIM).transpose(0, 2, 1, 3)
    vt_hm = v_all.reshape(bsz, n_tiles, kt, A_HEADS, HEAD_DIM).transpose(0, 3, 1, 4, 2)
    o_t = _attn_head_major(q_t, qi_t, wi.transpose(0, 2, 1), k_hm, vt_hm, ki_all, tbl, tbl_off, past, n_keys)
    return o_t.transpose(0, 3, 1, 2).reshape(bsz, t, A_WIDTH)


def _gla_chunk(zb, misc, wg, bg, og, st):
    c = zb.shape[0]
    dkw = B_HEADS * B_DK
    q = zb[:, 0:dkw] * (B_DK ** -0.5)
    k = zb[:, dkw:2 * dkw]
    v3 = _heads3(zb[:, 2 * dkw:2 * dkw + B_WIDTH], B_HEADS, B_DV)
    gate = zb[:, 2 * dkw + B_WIDTH:]
    log_a = jax.nn.log_sigmoid(_mmx(misc, wg) + bg) / GATE_TAU
    b = _cumsum_rows(log_a)
    b_last = b[c - 1:c, :]
    b_mid = b[c // 2 - 1:c // 2, :]
    att = _bmm_nt(_heads3(q * jnp.exp(b - b_mid), B_HEADS, B_DK), _heads3(k * jnp.exp(b_mid - b), B_HEADS, B_DK))
    ri = lax.broadcasted_iota(jnp.int32, att.shape, 1)
    cj = lax.broadcasted_iota(jnp.int32, att.shape, 2)
    att = jnp.where(ri >= cj, att, 0.0)
    o3 = _bmm_nt(_heads3(q * jnp.exp(b), B_HEADS, B_DK), st) + _bmm(att, v3)
    upd = _bmm_tn(v3, _heads3(k * jnp.exp(b_last - b), B_HEADS, B_DK))
    st_new = st * _heads3(jnp.exp(b_last), B_HEADS, B_DK) + upd
    o = _lanes2(o3 * lax.rsqrt(jnp.mean(o3 * o3, axis=-1, keepdims=True) + EPS))
    return o * og * jax.nn.silu(gate), st_new


def _gla_body(zb_ref, misc_ref, wg_ref, bg_ref, og_ref, s0_ref, o_ref, s_out_ref, s_ref):
    @pl.when(pl.program_id(1) == 0)
    def _():
        s_ref[...] = s0_ref[...]

    for g in range(zb_ref.shape[0]):
        o, st = _gla_chunk(zb_ref[g], misc_ref[g], wg_ref[...], bg_ref[...], og_ref[...], s_ref[g])
        o_ref[g] = o
        s_ref[g] = st
        s_out_ref[g] = st


def _gla(zb, misc, wg_emb, bg, og, s0_t, c):
    bsz, t, _ = zb.shape
    dkw = B_HEADS * B_DK
    g = math.gcd(_SEQS_PER_STEP, bsz)
    blk = lambda w: pl.BlockSpec((g, c, w), lambda b, i: (b, i, 0))
    st = pl.BlockSpec((g, B_HEADS, B_DV, B_DK), lambda b, i: (b, 0, 0, 0))
    return pl.pallas_call(
        _gla_body,
        grid=(bsz // g, t // c),
        in_specs=[blk(_W_B), blk(_LANES), _const_spec((_LANES, dkw)), _const_spec((1, dkw)),
                  _const_spec((1, B_WIDTH)), st],
        out_specs=[blk(B_WIDTH), st],
        out_shape=(jax.ShapeDtypeStruct((bsz, t, B_WIDTH), F32),
                   jax.ShapeDtypeStruct((bsz, B_HEADS, B_DV, B_DK), F32)),
        scratch_shapes=[pltpu.VMEM((g, B_HEADS, B_DV, B_DK), F32)],
        compiler_params=pltpu.CompilerParams(dimension_semantics=("parallel", "arbitrary"),
                                             vmem_limit_bytes=_VMEM_LIMIT),
        name="gla",
    )(zb, misc, wg_emb, bg, og, s0_t)


_PAIRS = C_HEADS // 2


def _pairs3(x):
    return jnp.stack([x[:, p * _LANES:(p + 1) * _LANES] for p in range(_PAIRS)], axis=0)


def _pair_diag(x3):
    left = lax.broadcasted_iota(jnp.int32, x3.shape, 2) < x3.shape[2] // 2
    zero = jnp.zeros((), x3.dtype)
    return jnp.concatenate([jnp.where(left, x3, zero), jnp.where(left, zero, x3)], axis=1)


def _pair_diag_wide(x3):
    left = lax.broadcasted_iota(jnp.int32, x3.shape, 2) % _LANES < _LANES // 2
    zero = jnp.zeros((), x3.dtype)
    return jnp.concatenate([jnp.where(left, x3, zero), jnp.where(left, zero, x3)], axis=1)


def _pmm(a, b):
    return jnp.einsum("pij,pjk->pik", a, b, preferred_element_type=F32)


def _pmm3(a, b):
    (ah, al), (bh, bl) = a, b
    n = ah.shape[1]
    both = _pmm(jnp.concatenate([ah, al], axis=1), _pair_diag(bh))
    return both[:, 0:n, :] + (both[:, n:, :] + _pmm(ah, _pair_diag(bl)))


def _gdn_chunk(zc_ref, misc, cw_ref, alog, dtb, og, xpad_ref, s_bd):
    c = misc.shape[0]
    w3 = 3 * C_WIDTH
    bf = lambda x: x.astype(_MXU_DTYPE)
    xpad_ref[8:8 + c, :] = zc_ref[:, 0:w3]
    conv = xpad_ref[5:5 + c, :] * cw_ref[0:1, :]
    for i in range(1, CONV_W):
        conv = conv + xpad_ref[5 + i:5 + i + c, :] * cw_ref[i:i + 1, :]
    xpad_ref[0:8, :] = xpad_ref[c:c + 8, :]
    act = jax.nn.silu(conv)
    q2 = _group_norm(act[:, 0:C_WIDTH], C_DK, False) * (C_DK ** -0.5)
    k2 = _group_norm(act[:, C_WIDTH:2 * C_WIDTH], C_DK, False)
    v2 = act[:, 2 * C_WIDTH:w3]

    g_full = -jnp.exp(alog) * jax.nn.softplus(misc + dtb)
    beta_full = jax.nn.sigmoid(misc)
    gcum = _cumsum_rows(g_full)
    gexp2 = _expand_heads(gcum, _M_CA, C_HEADS, C_DV)
    bexp2 = _expand_heads(beta_full, _M_CB, C_HEADS, C_DV)
    g_last2 = gexp2[c - 1:c, :]
    eg2 = jnp.exp(gexp2)

    gcum_t = gcum.T
    col = lambda h: jnp.broadcast_to(gcum[:, _M_CA + h:_M_CA + h + 1], (c, c))
    row = lambda h: jnp.broadcast_to(gcum_t[_M_CA + h:_M_CA + h + 1, :], (c, c))
    diff = jnp.stack([jnp.concatenate([col(2 * p) - row(2 * p), col(2 * p + 1) - row(2 * p + 1)], axis=-1)
                      for p in range(_PAIRS)], axis=0)
    ri = lax.broadcasted_iota(jnp.int32, diff.shape, 1)
    cj = lax.broadcasted_iota(jnp.int32, diff.shape, 2) % c
    lmask = jnp.where(ri >= cj, jnp.exp(jnp.minimum(diff, 0.0)), 0.0)

    kb2 = k2 * bexp2
    k_bd = _pair_diag(bf(_pairs3(k2)))
    kq = jnp.einsum("pil,pjl->pij", bf(_pairs3(jnp.concatenate([kb2, q2], axis=0))), k_bd,
                    preferred_element_type=F32)
    yield
    m = jnp.where(ri > cj, kq[:, 0:c, :] * lmask, 0.0)
    qk = kq[:, c:, :] * lmask
    inv = jnp.where(ri == cj, 1.0, 0.0) - m
    pw = _split_bf16(m)
    for _ in range(int(math.log2(c)) - 1):
        sq = _pmm3(pw, pw)
        yield
        pw = _split_bf16(sq)
        step = _pmm3(_split_bf16(inv), pw)
        yield
        inv = inv + step
    rhs = jnp.concatenate([_pairs3(v2 * bexp2), _pairs3(kb2 * eg2)], axis=-1)
    uw = _pmm(bf(inv), _pair_diag_wide(bf(rhs)))
    yield
    u, w = uw[:, :, 0:_LANES], uw[:, :, _LANES:]
    wq = _pmm(bf(jnp.concatenate([w, _pairs3(q2 * eg2)], axis=1)), bf(s_bd))
    yield
    v_new = u - wq[:, 0:c, :]
    o3 = wq[:, c:, :] + _pmm(bf(qk), _pair_diag(bf(v_new)))
    upd = jnp.einsum("pck,pcv->pkv", bf(_pairs3(k2 * jnp.exp(g_last2 - gexp2))), bf(v_new),
                     preferred_element_type=F32)
    yield
    same_head = (lax.broadcasted_iota(jnp.int32, upd.shape, 1) // C_DK) == (lax.broadcasted_iota(jnp.int32, upd.shape, 2) // C_DV)
    s_new = s_bd * _pairs3(jnp.exp(g_last2)) + jnp.where(same_head, upd, 0.0)
    o2 = jnp.concatenate([o3[p] for p in range(_PAIRS)], axis=-1)
    return _group_norm(o2, C_DV, True) * og * jax.nn.silu(zc_ref[:, w3:]), s_new


def _gdn_body(zc_ref, misc_ref, cw_ref, alog_ref, dtb_ref, og_ref, s0_ref, hist_ref,
              o_ref, s_out_ref, xpad_ref, s_ref):
    n_seq = zc_ref.shape[0]

    @pl.when(pl.program_id(1) == 0)
    def _():
        xpad_ref[:, 0:8, :] = hist_ref[...]
        zero = jnp.zeros((C_DK, C_DV), F32)
        for g in range(n_seq):
            for p in range(_PAIRS):
                s_ref[g, p] = jnp.concatenate([jnp.concatenate([s0_ref[g, 2 * p], zero], axis=-1),
                                               jnp.concatenate([zero, s0_ref[g, 2 * p + 1]], axis=-1)], axis=0)

    chunks = [_gdn_chunk(zc_ref.at[g], misc_ref[g], cw_ref, alog_ref[...], dtb_ref[...], og_ref[...],
                         xpad_ref.at[g], s_ref[g]) for g in range(n_seq)]
    for g, (o, s_new) in enumerate(_round_robin(chunks)):
        o_ref[g] = o
        s_ref[g] = s_new

    @pl.when(pl.program_id(1) == pl.num_programs(1) - 1)
    def _():
        for g in range(n_seq):
            for p in range(_PAIRS):
                s_out_ref[g, 2 * p] = s_ref[g, p, 0:C_DK, 0:C_DV]
                s_out_ref[g, 2 * p + 1] = s_ref[g, p, C_DK:, C_DV:]


def _gdn(zc, misc, conv_w, alog_emb, dtb_emb, og, s0, hist8, c):
    bsz, t, _ = zc.shape
    w3 = 3 * C_WIDTH
    g = math.gcd(_SEQS_PER_STEP, bsz)
    blk = lambda w: pl.BlockSpec((g, c, w), lambda b, i: (b, i, 0))
    st = pl.BlockSpec((g, C_HEADS, C_DK, C_DV), lambda b, i: (b, 0, 0, 0))
    return pl.pallas_call(
        _gdn_body,
        grid=(bsz // g, t // c),
        in_specs=[blk(_W_C), blk(_LANES), _const_spec((CONV_W, w3)), _const_spec((1, _LANES)),
                  _const_spec((1, _LANES)), _const_spec((1, C_WIDTH)), st,
                  pl.BlockSpec((g, 8, w3), lambda b, i: (b, 0, 0))],
        out_specs=[blk(C_WIDTH), st],
        out_shape=(jax.ShapeDtypeStruct((bsz, t, C_WIDTH), F32),
                   jax.ShapeDtypeStruct((bsz, C_HEADS, C_DK, C_DV), F32)),
        scratch_shapes=[pltpu.VMEM((g, 8 + c, w3), F32), pltpu.VMEM((g, _PAIRS, 2 * C_DK, 2 * C_DV), F32)],
        compiler_params=pltpu.CompilerParams(dimension_semantics=("parallel", "arbitrary"),
                                             vmem_limit_bytes=_VMEM_LIMIT),
        name="gdn",
    )(zc, misc, conv_w, alog_emb, dtb_emb, og, s0, hist8)


def _mlp_body(x_ref, oa_ref, ob_ref, oc_ref, wo_ref, g2_ref, wup_ref, wdn_ref, y_ref, *, ff_tile, head_major):
    if head_major:
        oa = jnp.concatenate([oa_ref[h].T for h in range(A_HEADS)], axis=-1)
    else:
        oa = oa_ref[...]
    mixed = jnp.concatenate([oa, ob_ref[...], oc_ref[...]], axis=-1)
    y_ref[...] = x_ref[...] + _mm(mixed, wo_ref[...])
    h2 = (_rms(y_ref[...]) * g2_ref[...]).astype(_MXU_DTYPE)
    for j in range(D_FF // ff_tile):
        up = jnp.dot(h2, wup_ref[:, j * ff_tile:(j + 1) * ff_tile], preferred_element_type=F32)
        y_ref[...] += _mm(jnp.square(jnp.maximum(up, 0.0)), wdn_ref[j * ff_tile:(j + 1) * ff_tile, :])


_MLP_WEIGHT_SPECS = lambda: [_const_spec((D_MODEL, D_MODEL), True), _const_spec((1, D_MODEL)),
                             _const_spec((D_MODEL, D_FF), True), _const_spec((D_FF, D_MODEL), True)]


def _mlp(x2d, oa, ob, oc, wo, g2, wup, wdn):
    n = x2d.shape[0]
    tm = min(_ROW_TILE, n)
    assert n % tm == 0
    row = lambda w: pl.BlockSpec((tm, w), lambda i: (i, 0))
    return pl.pallas_call(
        functools.partial(_mlp_body, ff_tile=1024, head_major=False),
        grid=(n // tm,),
        in_specs=[row(D_MODEL), row(A_WIDTH), row(B_WIDTH), row(C_WIDTH)] + _MLP_WEIGHT_SPECS(),
        out_specs=row(D_MODEL),
        out_shape=jax.ShapeDtypeStruct((n, D_MODEL), F32),
        compiler_params=pltpu.CompilerParams(dimension_semantics=("parallel",), vmem_limit_bytes=_VMEM_LIMIT),
        name="mlp",
    )(x2d, oa, ob, oc, wo, g2, wup, wdn)


def _mlp_head_major(x, oa_t, ob, oc, wo, g2, wup, wdn):
    bsz, t, _ = x.shape
    tm = min(_ROW_TILE, t)
    assert t % tm == 0
    row = lambda w: pl.BlockSpec((None, tm, w), lambda b, i: (b, i, 0))
    return pl.pallas_call(
        functools.partial(_mlp_body, ff_tile=1024, head_major=True),
        grid=(bsz, t // tm),
        in_specs=[row(D_MODEL), pl.BlockSpec((None, A_HEADS, HEAD_DIM, tm), lambda b, i: (b, 0, 0, i)),
                  row(B_WIDTH), row(C_WIDTH)] + _MLP_WEIGHT_SPECS(),
        out_specs=row(D_MODEL),
        out_shape=jax.ShapeDtypeStruct((bsz, t, D_MODEL), F32),
        compiler_params=pltpu.CompilerParams(dimension_semantics=("parallel", "parallel"),
                                             vmem_limit_bytes=_VMEM_LIMIT),
        name="mlp",
    )(x, oa_t, ob, oc, wo, g2, wup, wdn)


def _prep_layer_params(ln1_g, w_in, a_qnorm_g, a_knorm_g, gla_w_gate, gla_b_gate, gla_onorm_g,
                       conv_w, gdn_a_log, gdn_dt_bias, gdn_onorm_g, w_o, ln2_g, w_up, w_down):
    cols = [jnp.zeros((D_MODEL, _PAD[name]), w_in.dtype) if name in _PAD
            else w_in[:, _SRC[name]:_SRC[name] + _SIZE[name]] for name in _NEW_ORDER]
    lane_emb = lambda vec, start: jnp.zeros((1, _LANES), F32).at[0, start:start + vec.shape[0]].set(vec.astype(F32))
    return dict(
        g1=ln1_g.reshape(1, D_MODEL).astype(F32),
        w_perm=jnp.concatenate(cols, axis=1).astype(_MXU_DTYPE),
        gq=jnp.tile(a_qnorm_g.astype(F32), A_HEADS).reshape(1, A_WIDTH),
        gk=jnp.tile(a_knorm_g.astype(F32), A_HEADS).reshape(1, A_WIDTH),
        wg_emb=jnp.zeros((_LANES, B_HEADS * B_DK), F32).at[_M_GLR:_M_GLR + GATE_RANK].set(gla_w_gate.astype(F32)),
        bg=gla_b_gate.reshape(1, -1).astype(F32),
        og_b=jnp.tile(gla_onorm_g.astype(F32), B_HEADS).reshape(1, B_WIDTH),
        conv_w=conv_w.astype(F32),
        alog_emb=lane_emb(gdn_a_log, _M_CA),
        dtb_emb=lane_emb(gdn_dt_bias, _M_CA),
        og_c=jnp.tile(gdn_onorm_g.astype(F32), C_HEADS).reshape(1, C_WIDTH),
        wo=w_o.astype(_MXU_DTYPE), g2=ln2_g.reshape(1, D_MODEL).astype(F32),
        wup=w_up.astype(_MXU_DTYPE), wdn=w_down.astype(_MXU_DTYPE),
    )


def _layer(x, cache, p, tbl, tbl_off):
    bsz, t, _ = x.shape
    n = bsz * t
    assert t >= CONV_W - 1
    c = min(CHUNK, t)
    r3 = lambda a: a.reshape(bsz, t, a.shape[-1])
    mixer_b = lambda zb, misc, s0: _gla(zb, misc, p["wg_emb"], p["bg"], p["og_b"], s0, c)
    mixer_c = lambda zc, misc, s0, hist8: _gdn(zc, misc, p["conv_w"], p["alog_emb"], p["dtb_emb"], p["og_c"], s0, hist8, c)

    if cache is None:
        _, kt, _, n_tiles = _attn_tiling(t, t)
        assert n_tiles * kt == t
        q_t, qi_t, wi_t, k_hm, vt_hm, ki16, ka, va, misc, zb, zc = _inproj_head_major(
            x, p["g1"], p["w_perm"], p["gq"], p["gk"], kt)
        o_a_t = _attn_head_major(q_t, qi_t, wi_t, k_hm, vt_hm, ki16, tbl, tbl_off, 0, t)
        o_b, s_gla_t = mixer_b(zb, misc, jnp.zeros((bsz, B_HEADS, B_DV, B_DK), F32))
        o_c, s_gdn = mixer_c(zc, misc, jnp.zeros((bsz, C_HEADS, C_DK, C_DV), F32), jnp.zeros((bsz, 8, 3 * C_WIDTH), F32))
        y = _mlp_head_major(x, o_a_t, o_b, o_c, p["wo"], p["g2"], p["wup"], p["wdn"])
    else:
        ck, cv, cki, sg, sd, cbuf = cache
        past = ck.shape[1]
        x2d = x.reshape(n, D_MODEL)
        qa, qi, ka, va, misc, zb, zc = _inproj(x2d, p["g1"], p["w_perm"], p["gq"], p["gk"])
        ka, va, misc, zb, zc = r3(ka), r3(va), r3(misc), r3(zb), r3(zc)
        k_all = jnp.concatenate([ck.reshape(bsz, past, A_WIDTH).astype(F32), ka], axis=1)
        v_all = jnp.concatenate([cv.reshape(bsz, past, A_WIDTH).astype(F32), va], axis=1)
        ki_all = jnp.concatenate([cki.astype(F32), misc[..., 0:IDX_DIM]], axis=1)
        o_a = _attn(r3(qa), r3(qi), misc[..., _M_WI:_M_WI + IDX_HEADS], k_all.astype(_MXU_DTYPE),
                    v_all.astype(_MXU_DTYPE), ki_all.astype(_MXU_DTYPE), tbl, tbl_off, past)
        o_b, s_gla_t = mixer_b(zb, misc, jnp.swapaxes(sg.astype(F32), 2, 3))
        hist8 = jnp.pad(cbuf.astype(F32), ((0, 0), (8 - (CONV_W - 1), 0), (0, 0)))
        o_c, s_gdn = mixer_c(zc, misc, sd.astype(F32), hist8)
        y = _mlp(x2d, o_a.reshape(n, A_WIDTH), o_b.reshape(n, B_WIDTH), o_c.reshape(n, C_WIDTH),
                 p["wo"], p["g2"], p["wup"], p["wdn"]).reshape(bsz, t, D_MODEL)

    s_gla = jnp.swapaxes(s_gla_t, 2, 3)
    new_conv = zc[:, t - (CONV_W - 1):, 0:3 * C_WIDTH]
    state = (ka.reshape(bsz, t, A_HEADS, HEAD_DIM), va.reshape(bsz, t, A_HEADS, HEAD_DIM), misc[..., 0:IDX_DIM],
             s_gla, s_gdn, new_conv)
    return y, state


def kernel(x_prompt, x_sample, cache_a_k, cache_a_v, cache_a_kidx, state_gla, state_gdn, state_conv, rel_bias, ln1_g, w_in, a_qnorm_g, a_knorm_g, gla_w_gate, gla_b_gate, gla_onorm_g, conv_w, gdn_a_log, gdn_dt_bias, gdn_onorm_g, w_o, ln2_g, w_up, w_down):
    depth = ln1_g.shape[0]
    yp, ys = x_prompt, x_sample
    past = cache_a_k.shape[2]
    tbl_p = _bias_tables(rel_bias, x_prompt.shape[1], x_prompt.shape[1], 0)
    tbl_s = _bias_tables(rel_bias, x_sample.shape[1], past + x_sample.shape[1], past)
    new_p, new_s = [], []
    for l in range(depth):
        p = _prep_layer_params(ln1_g[l], w_in[l], a_qnorm_g[l], a_knorm_g[l], gla_w_gate[l],
                               gla_b_gate[l], gla_onorm_g[l], conv_w[l], gdn_a_log[l], gdn_dt_bias[l],
                               gdn_onorm_g[l], w_o[l], ln2_g[l], w_up[l], w_down[l])
        yp, st_p = _layer(yp, None, p, *tbl_p)
        cache_l = (cache_a_k[l], cache_a_v[l], cache_a_kidx[l], state_gla[l], state_gdn[l], state_conv[l])
        ys, st_s = _layer(ys, cache_l, p, *tbl_s)
        new_p.append(st_p)
        new_s.append(st_s)
    stack = lambda states, i: jnp.stack([s[i] for s in states], axis=0)
    return (yp, ys) + tuple(stack(new_p, i) for i in range(6)) + tuple(stack(new_s, i) for i in range(6))
```

```python
import functools
import math

import numpy as np
import jax
import jax.numpy as jnp
from jax import lax
from jax.experimental import pallas as pl
from jax.experimental.pallas import tpu as pltpu

D_MODEL = 1024
CHUNK = 64
HEAD_DIM = 64
A_HEADS = 4
A_WIDTH = A_HEADS * HEAD_DIM
IDX_HEADS = 8
IDX_DIM = 64
TOPK_MAX = 256
N_BUCKETS = 32
MAX_DISTANCE = 128
B_HEADS = 4
B_DK = 32
B_DV = 64
B_WIDTH = B_HEADS * B_DV
GATE_RANK = 16
GATE_TAU = 16.0
C_HEADS = 8
C_DK = 64
C_DV = 64
C_WIDTH = C_HEADS * C_DV
CONV_W = 4
D_FF = 4 * D_MODEL
EPS = 1e-6

F32 = jnp.float32
BF16 = jnp.bfloat16
_MXU_DTYPE = BF16
_HI = lax.Precision.HIGHEST
_VMEM_LIMIT = 56 * 1024 * 1024
_LANES = 128
_ROW_TILE = 512
_SEQS_PER_STEP = 8
_NEG = -1e30
_LOG2E = math.log2(math.e)
_INT_MIN = -2 ** 31

_SPLIT_NAMES = ("aq", "ak", "av", "aqi", "aki", "awi", "bq", "bk", "bv", "bglr", "bog",
                "cq", "ck", "cv", "ca", "cb", "cog")
_SPLIT_SIZES = (A_WIDTH, A_WIDTH, A_WIDTH, IDX_HEADS * IDX_DIM, IDX_DIM, IDX_HEADS,
                B_HEADS * B_DK, B_HEADS * B_DK, B_WIDTH, GATE_RANK, B_WIDTH,
                C_HEADS * C_DK, C_HEADS * C_DK, C_WIDTH, C_HEADS, C_HEADS, C_WIDTH)
D_IN = sum(_SPLIT_SIZES)
_SRC = dict(zip(_SPLIT_NAMES, np.concatenate([[0], np.cumsum(_SPLIT_SIZES)[:-1]]).tolist()))
_SIZE = dict(zip(_SPLIT_NAMES, _SPLIT_SIZES))

_M_WI, _M_GLR, _M_CA, _M_CB = 64, 72, 88, 96
_NEW_ORDER = ("aq", "ak", "av", "aqi", "aki", "awi", "bglr", "ca", "cb", "pad24",
              "bq", "bk", "bv", "bog", "cq", "ck", "cv", "cog")
_PAD = {"pad24": 24}


def _new_offsets():
    offs, pos = {}, 0
    for name in _NEW_ORDER:
        offs[name] = pos
        pos += _PAD.get(name, 0) or _SIZE[name]
    return offs, pos


_OFF, D_Z = _new_offsets()
_C_QA, _C_KA, _C_VA, _C_QI, _C_MISC = _OFF["aq"], _OFF["ak"], _OFF["av"], _OFF["aqi"], _OFF["aki"]
_C_B, _C_C = _OFF["bq"], _OFF["cq"]
_W_B = 2 * B_HEADS * B_DK + 2 * B_WIDTH
_W_C = 4 * C_WIDTH
assert _C_MISC % _LANES == 0 and _C_B == _C_MISC + _LANES and _C_C == _C_B + _W_B and D_Z == _C_C + _W_C
assert (_OFF["awi"], _OFF["bglr"], _OFF["ca"], _OFF["cb"]) == tuple(_C_MISC + m for m in (_M_WI, _M_GLR, _M_CA, _M_CB))


def _mm(a, b):
    return jnp.dot(a.astype(_MXU_DTYPE), b.astype(_MXU_DTYPE), preferred_element_type=F32)


def _mmx(a, b):
    return jnp.dot(a, b, preferred_element_type=F32, precision=_HI)


def _split_bf16(x):
    hi = pltpu.bitcast(pltpu.bitcast(x, jnp.int32) & jnp.int32(-65536), F32)
    return hi.astype(BF16), (x - hi).astype(BF16)


def _cumsum_rows(x):
    r = lax.broadcasted_iota(jnp.int32, x.shape, 0)
    s = 1
    while s < x.shape[0]:
        x = x + jnp.where(r >= s, pltpu.roll(x, s, 0), 0.0)
        s *= 2
    return x


def _heads3(x, n_heads, width):
    return jnp.stack([x[:, h * width:(h + 1) * width] for h in range(n_heads)], axis=0)


def _lanes2(x3):
    return jnp.concatenate([x3[h] for h in range(x3.shape[0])], axis=-1)


def _expand_heads(x, lane0, n_heads, width):
    c = x.shape[0]
    return jnp.concatenate([jnp.broadcast_to(x[:, lane0 + h:lane0 + h + 1], (c, width)) for h in range(n_heads)],
                           axis=-1)


def _bmm(a, b):
    return jnp.einsum("hij,hjk->hik", a.astype(_MXU_DTYPE), b.astype(_MXU_DTYPE), preferred_element_type=F32)


def _bmm_nt(a, b):
    return jnp.einsum("hik,hjk->hij", a.astype(_MXU_DTYPE), b.astype(_MXU_DTYPE), preferred_element_type=F32)


def _bmm_tn(a, b):
    return jnp.einsum("hck,hcv->hkv", a.astype(_MXU_DTYPE), b.astype(_MXU_DTYPE), preferred_element_type=F32)


def _rms(x):
    return x * lax.rsqrt(jnp.mean(x * x, axis=-1, keepdims=True) + EPS)


def _group_norm(x, group, mean):
    outs = []
    for g in range(x.shape[-1] // group):
        xs = x[:, g * group:(g + 1) * group]
        ss = jnp.sum(xs * xs, axis=-1, keepdims=True)
        outs.append(xs * lax.rsqrt((ss / group if mean else ss) + EPS))
    return jnp.concatenate(outs, axis=-1)


def _fold_rows(w, rows=8):
    parts = [w[i * rows:(i + 1) * rows, :] for i in range(w.shape[0] // rows)]
    while len(parts) > 1:
        parts = [parts[i] + parts[i + 1] for i in range(0, len(parts), 2)]
    return parts[0]


def _round_robin(gens):
    results = [None] * len(gens)
    live = list(range(len(gens)))
    while live:
        for i in list(live):
            try:
                next(gens[i])
            except StopIteration as stop:
                results[i] = stop.value
                live.remove(i)
    return results


def _const_spec(shape, single_buffer=False):
    zeros = (0,) * len(shape)
    if single_buffer:
        return pl.BlockSpec(shape, lambda *_: zeros, pipeline_mode=pl.Buffered(1))
    return pl.BlockSpec(shape, lambda *_: zeros)


def _inproj_body(x_ref, g1_ref, w_ref, gq_ref, gk_ref, *out_refs, head_major, kt):
    h = (_rms(x_ref[...]) * g1_ref[...]).astype(_MXU_DTYPE)

    def proj(c0, width):
        return jnp.dot(h, w_ref[:, c0:c0 + width], preferred_element_type=F32)

    qn = _group_norm(proj(_C_QA, A_WIDTH), HEAD_DIM, True) * gq_ref[...] * (HEAD_DIM ** -0.5 * _LOG2E)
    kn = _group_norm(proj(_C_KA, A_WIDTH), HEAD_DIM, True) * gk_ref[...]
    v = proj(_C_VA, A_WIDTH)
    qi = proj(_C_QI, IDX_HEADS * IDX_DIM) * (IDX_DIM ** -0.5)
    misc = proj(_C_MISC, _LANES)
    if head_major:
        qt_ref, qit_ref, wit_ref, khm_ref, vt_ref, ki_ref, ka_ref, va_ref, misc_ref, zb_ref, zc_ref = out_refs
        for hd in range(A_HEADS):
            hs = slice(hd * HEAD_DIM, (hd + 1) * HEAD_DIM)
            qt_ref[hd] = qn[:, hs].T.astype(qt_ref.dtype)
            khm_ref[hd] = kn[:, hs].astype(khm_ref.dtype)
            for j in range(v.shape[0] // kt):
                vt_ref[hd, j] = v[j * kt:(j + 1) * kt, hs].T.astype(vt_ref.dtype)
        for hd in range(IDX_HEADS):
            qit_ref[hd] = qi[:, hd * IDX_DIM:(hd + 1) * IDX_DIM].T.astype(qit_ref.dtype)
        wit_ref[...] = misc.T[_M_WI:_M_WI + IDX_HEADS, :]
        ki_ref[...] = misc[:, 0:IDX_DIM].astype(ki_ref.dtype)
    else:
        qa_ref, qi_ref, ka_ref, va_ref, misc_ref, zb_ref, zc_ref = out_refs
        qa_ref[...] = qn.astype(qa_ref.dtype)
        qi_ref[...] = qi.astype(qi_ref.dtype)
    ka_ref[...] = kn
    va_ref[...] = v
    misc_ref[...] = misc
    zb_ref[...] = proj(_C_B, _W_B)
    zc_ref[...] = proj(_C_C, _W_C)


def _inproj(x2d, g1, w_perm, gq, gk):
    n = x2d.shape[0]
    tm = min(_ROW_TILE, n)
    assert n % tm == 0
    row = lambda w: pl.BlockSpec((tm, w), lambda i: (i, 0))
    widths = (A_WIDTH, IDX_HEADS * IDX_DIM, A_WIDTH, A_WIDTH, _LANES, _W_B, _W_C)
    dtypes = (_MXU_DTYPE, _MXU_DTYPE, F32, F32, F32, F32, F32)
    return pl.pallas_call(
        functools.partial(_inproj_body, head_major=False, kt=None),
        grid=(n // tm,),
        in_specs=[row(D_MODEL), _const_spec((1, D_MODEL)), _const_spec((D_MODEL, D_Z), True),
                  _const_spec((1, A_WIDTH)), _const_spec((1, A_WIDTH))],
        out_specs=[row(w) for w in widths],
        out_shape=tuple(jax.ShapeDtypeStruct((n, w), d) for w, d in zip(widths, dtypes)),
        compiler_params=pltpu.CompilerParams(dimension_semantics=("parallel",), vmem_limit_bytes=_VMEM_LIMIT),
        name="inproj",
    )(x2d, g1, w_perm, gq, gk)


def _inproj_head_major(x, g1, w_perm, gq, gk, kt):
    bsz, t, _ = x.shape
    tm = min(_ROW_TILE, t)
    assert t % tm == 0 and tm % kt == 0
    row = lambda w: pl.BlockSpec((None, tm, w), lambda b, i: (b, i, 0))
    rows = lambda w, d: jax.ShapeDtypeStruct((bsz, t, w), d)
    lanes = lambda nh: pl.BlockSpec((None, nh, HEAD_DIM, tm), lambda b, i: (b, 0, 0, i))
    out_specs = [lanes(A_HEADS), lanes(IDX_HEADS), pl.BlockSpec((None, IDX_HEADS, tm), lambda b, i: (b, 0, i)),
                 pl.BlockSpec((None, A_HEADS, tm, HEAD_DIM), lambda b, i: (b, 0, i, 0)),
                 pl.BlockSpec((None, A_HEADS, tm // kt, HEAD_DIM, kt), lambda b, i: (b, 0, i, 0, 0)),
                 row(IDX_DIM), row(A_WIDTH), row(A_WIDTH), row(_LANES), row(_W_B), row(_W_C)]
    out_shape = (jax.ShapeDtypeStruct((bsz, A_HEADS, HEAD_DIM, t), _MXU_DTYPE),
                 jax.ShapeDtypeStruct((bsz, IDX_HEADS, IDX_DIM, t), _MXU_DTYPE),
                 jax.ShapeDtypeStruct((bsz, IDX_HEADS, t), F32),
                 jax.ShapeDtypeStruct((bsz, A_HEADS, t, HEAD_DIM), _MXU_DTYPE),
                 jax.ShapeDtypeStruct((bsz, A_HEADS, t // kt, HEAD_DIM, kt), _MXU_DTYPE),
                 rows(IDX_DIM, _MXU_DTYPE), rows(A_WIDTH, F32), rows(A_WIDTH, F32), rows(_LANES, F32),
                 rows(_W_B, F32), rows(_W_C, F32))
    return pl.pallas_call(
        functools.partial(_inproj_body, head_major=True, kt=kt),
        grid=(bsz, t // tm),
        in_specs=[row(D_MODEL), _const_spec((1, D_MODEL)), _const_spec((D_MODEL, D_Z), True),
                  _const_spec((1, A_WIDTH)), _const_spec((1, A_WIDTH))],
        out_specs=out_specs,
        out_shape=out_shape,
        compiler_params=pltpu.CompilerParams(dimension_semantics=("parallel", "parallel"),
                                             vmem_limit_bytes=_VMEM_LIMIT),
        name="inproj",
    )(x, g1, w_perm, gq, gk)


def _attn_body(q_ref, qi_ref, wi_ref, k_ref, vt_ref, ki_ref, tbl_ref, o_ref, key_ref, half_ref, sc_ref, sc2_ref, acc_ref, s_ref, s2_ref, p_ref, p2_ref, *,
               tq, kt, n_tiles, past, n_keys, topk, tbl_off, n_tbl, idx_bits):
    qb = pl.program_id(1)
    q0 = past + qb * tq
    n_kt = jnp.minimum(n_tiles, (q0 + tq + kt - 1) // kt)
    krow = lax.broadcasted_iota(jnp.int32, (kt, tq), 0)
    q_chunk = (q0 + lax.broadcasted_iota(jnp.int32, (1, tq), 1)) // CHUNK
    adm_end = jnp.minimum((q_chunk + 1) * CHUNK, n_keys)
    wi = wi_ref[...] * (IDX_HEADS ** -0.5)

    n_pair = (n_kt + 1) // 2

    def head_scores(t, dst_ref):
        ki_t = ki_ref[pl.ds(pl.multiple_of(t * kt, kt), kt), :]
        for h in range(IDX_HEADS):
            dst_ref[h] = _mm(ki_t, qi_ref[h])

    def combine(t, src_ref):
        acc = jnp.zeros((kt, tq), F32)
        for h in range(IDX_HEADS):
            acc = acc + wi[h:h + 1, :] * jnp.maximum(src_ref[h], 0.0)
        bits = pltpu.bitcast(acc, jnp.int32)
        bits = jnp.where(bits == _INT_MIN, 0, bits)
        key = jnp.where(bits >= 0, bits, bits ^ jnp.int32(0x7FFFFFFF))
        key = jnp.where(krow < adm_end - t * kt, key, _INT_MIN)
        key_ref[t] = key
        half_ref[t] = (key >> 16).astype(jnp.int16)

    head_scores(0, sc_ref)

    def score_pair(j, carry):
        t0 = 2 * j
        head_scores(t0 + 1, sc2_ref)
        combine(t0, sc_ref)
        head_scores(jnp.minimum(t0 + 2, 2 * n_pair - 2), sc_ref)
        combine(t0 + 1, sc2_ref)
        return carry

    lax.fori_loop(0, n_pair, score_pair, 0)

    def count(*pred_fns):
        def body(j, cnts):
            t = 2 * j
            k0, k1 = key_ref[t], key_ref[t + 1]
            return tuple(cnt + _fold_rows(jnp.where(fn(t, k0), 1.0, 0.0) + jnp.where(fn(t + 1, k1), 1.0, 0.0))
                         for fn, cnt in zip(pred_fns, cnts))
        cnts = lax.fori_loop(0, n_pair, body, tuple(jnp.zeros((8, tq), F32) for _ in pred_fns))
        return tuple(jnp.sum(cnt, axis=0, keepdims=True) for cnt in cnts)

    def count16(pred_fn):
        one, zero = jnp.int16(1), jnp.int16(0)
        def body(j, cnt):
            w = jnp.where(pred_fn(half_ref[2 * j]), one, zero) + jnp.where(pred_fn(half_ref[2 * j + 1]), one, zero)
            return cnt + _fold_rows(w, 16)
        cnt = lax.fori_loop(0, n_pair, body, jnp.zeros((16, tq), jnp.int16))
        return jnp.sum(cnt.astype(jnp.int32), axis=0, keepdims=True)

    def kth_largest16(kth):
        def bit(i, v):
            cand = v + lax.shift_left(jnp.int32(1), 15 - i)
            cand16 = cand.astype(jnp.int16)
            return jnp.where(count16(lambda k: k >= cand16) >= kth, cand, v)
        return lax.fori_loop(0, 16, bit, jnp.full((1, tq), -2 ** 15, jnp.int32))

    thr_hi = kth_largest16(topk)
    thr_hi16 = thr_hi.astype(jnp.int16)
    above = count16(lambda k: k > thr_hi16)

    def low_halves(t, carry):
        key = key_ref[t]
        half_ref[t] = jnp.where((key >> 16) == thr_hi, (key & 0xFFFF) - 2 ** 15, -2 ** 15).astype(jnp.int16)
        return carry

    lax.fori_loop(0, 2 * n_pair, low_halves, 0)
    thr = thr_hi * 2 ** 16 + (kth_largest16(topk - above) + 2 ** 15)
    thr = jnp.maximum(thr, _INT_MIN + 1)

    cnt_gt, cnt_ge = count(lambda t, k: k > thr, lambda t, k: k >= thr)
    need = topk - cnt_gt
    has_excess = jnp.max(jnp.where(cnt_ge > topk, 1.0, 0.0)) > 0.0

    def tie_search():
        def bit(i, last):
            cand = last + lax.shift_left(jnp.int32(1), idx_bits - 1 - i)
            below, = count(lambda t, k: (k == thr) & ((t * kt + krow) < cand))
            return jnp.where(below < need, cand, last)
        return lax.fori_loop(0, idx_bits, bit, jnp.zeros((1, tq), jnp.int32))

    last = lax.cond(has_excess, tie_search, lambda: jnp.full((1, tq), 2 ** 30, jnp.int32))

    acc_ref[...] = jnp.zeros(acc_ref.shape, F32)

    def logits(t, dst_ref):
        off = pl.multiple_of(t * kt, kt)
        for h in range(A_HEADS):
            dst_ref[h] = _mm(k_ref[h, pl.ds(off, kt), :], q_ref[h])

    def softmax_pv(t, src_ref, pr_ref, carry, ties):
        ms, ls = carry
        off = pl.multiple_of(t * kt, kt)
        key = key_ref[t]
        if ties:
            sel = (key > thr) | ((key == thr) & ((off + krow) <= last))
        else:
            sel = key >= thr
        ti = jnp.clip(t - qb + tbl_off, 0, n_tbl - 1)
        new_ms, new_ls, alphas = [], [], []
        for h in range(A_HEADS):
            s = jnp.where(sel, src_ref[h] + tbl_ref[ti, h], _NEG)
            m_new = jnp.maximum(ms[h], jnp.max(s, axis=0, keepdims=True))
            alpha = jnp.exp2(ms[h] - m_new)
            p = jnp.exp2(s - m_new)
            new_ls.append(alpha * ls[h] + jnp.sum(p, axis=0, keepdims=True))
            new_ms.append(m_new)
            alphas.append(alpha)
            pr_ref[h] = p.astype(pr_ref.dtype)
        for h in range(A_HEADS):
            acc_ref[h] = alphas[h] * acc_ref[h] + jnp.dot(vt_ref[h, t], pr_ref[h], preferred_element_type=F32)
        return tuple(new_ms), tuple(new_ls)

    logits(0, s_ref)

    def attend(ties):
        def attend_pair(j, carry):
            t0 = 2 * j
            logits(t0 + 1, s2_ref)
            carry = softmax_pv(t0, s_ref, p_ref, carry, ties)
            logits(jnp.minimum(t0 + 2, 2 * n_pair - 2), s_ref)
            return softmax_pv(t0 + 1, s2_ref, p2_ref, carry, ties)

        m0 = tuple(jnp.full((1, tq), _NEG, F32) for _ in range(A_HEADS))
        l0 = tuple(jnp.zeros((1, tq), F32) for _ in range(A_HEADS))
        return lambda: lax.fori_loop(0, n_pair, attend_pair, (m0, l0))

    _, ls = lax.cond(has_excess, attend(True), attend(False))
    for h in range(A_HEADS):
        o_ref[h] = acc_ref[h] / ls[h]


def _rel_bucket(rel):
    nb = N_BUCKETS // 2
    ret = jnp.where(rel > 0, nb, 0)
    n = jnp.abs(rel)
    max_exact = nb // 2
    nf = jnp.maximum(n, 1).astype(F32)
    large = max_exact + (jnp.log(nf / max_exact) / math.log(MAX_DISTANCE / max_exact)
                         * (nb - max_exact)).astype(jnp.int32)
    large = jnp.minimum(large, nb - 1)
    return ret + jnp.where(n < max_exact, n, large)


def _attn_tiling(t, n_keys):
    kt = 256
    tq = min(kt, t)
    assert t % tq == 0
    return tq, kt, t // tq, 2 * -(-n_keys // (2 * kt))


def _bias_tables(rel_bias, t, n_keys, past):
    tq, kt, nqb, n_tiles = _attn_tiling(t, n_keys)
    j = jnp.arange(kt)[:, None]
    i = jnp.arange(tq)[None, :]
    if nqb == 1:
        rels = [tile * kt + j - (past + i) for tile in range(n_tiles)]
        off = 0
    else:
        assert past == 0 and tq == kt and kt >= MAX_DISTANCE
        rels = [j - i - 2 * kt, j - i - kt, j - i]
        off = 2
    onehot = jax.nn.one_hot(_rel_bucket(jnp.stack(rels, axis=0)), N_BUCKETS, dtype=F32)
    tbl = jnp.einsum("nktb,bh->nhkt", onehot, rel_bias.astype(F32) * _LOG2E, precision=_HI)
    return tbl, off


def _attn_head_major(q_t, qi_t, wi_t, k_hm, vt_hm, ki_all, tbl, tbl_off, past, n_keys):
    bsz, _, _, t = q_t.shape
    topk = min(TOPK_MAX, n_keys // 4)
    tq, kt, nqb, n_tiles = _attn_tiling(t, n_keys)
    s_pad = n_tiles * kt
    assert s_pad >= topk and k_hm.shape[2] == s_pad and vt_hm.shape[2] == n_tiles and ki_all.shape[1] == s_pad
    n_tbl = tbl.shape[0]
    body = functools.partial(_attn_body, tq=tq, kt=kt, n_tiles=n_tiles, past=past, n_keys=n_keys, topk=topk,
                             tbl_off=tbl_off, n_tbl=n_tbl, idx_bits=int(s_pad).bit_length())
    return pl.pallas_call(
        body,
        grid=(bsz, nqb),
        in_specs=[pl.BlockSpec((None, A_HEADS, HEAD_DIM, tq), lambda b, i: (b, 0, 0, i)),
                  pl.BlockSpec((None, IDX_HEADS, IDX_DIM, tq), lambda b, i: (b, 0, 0, i)),
                  pl.BlockSpec((None, IDX_HEADS, tq), lambda b, i: (b, 0, i)),
                  pl.BlockSpec((None, A_HEADS, s_pad, HEAD_DIM), lambda b, i: (b, 0, 0, 0)),
                  pl.BlockSpec((None, A_HEADS, n_tiles, HEAD_DIM, kt), lambda b, i: (b, 0, 0, 0, 0)),
                  pl.BlockSpec((None, s_pad, IDX_DIM), lambda b, i: (b, 0, 0)),
                  _const_spec((n_tbl, A_HEADS, kt, tq), True)],
        out_specs=pl.BlockSpec((None, A_HEADS, HEAD_DIM, tq), lambda b, i: (b, 0, 0, i)),
        out_shape=jax.ShapeDtypeStruct((bsz, A_HEADS, HEAD_DIM, t), F32),
        scratch_shapes=[pltpu.VMEM((n_tiles, kt, tq), jnp.int32), pltpu.VMEM((n_tiles, kt, tq), jnp.int16),
                        pltpu.VMEM((IDX_HEADS, kt, tq), F32), pltpu.VMEM((IDX_HEADS, kt, tq), F32),
                        pltpu.VMEM((A_HEADS, HEAD_DIM, tq), F32),
                        pltpu.VMEM((A_HEADS, kt, tq), F32), pltpu.VMEM((A_HEADS, kt, tq), F32),
                        pltpu.VMEM((A_HEADS, kt, tq), _MXU_DTYPE), pltpu.VMEM((A_HEADS, kt, tq), _MXU_DTYPE)],
        compiler_params=pltpu.CompilerParams(dimension_semantics=("parallel", "parallel"),
                                             vmem_limit_bytes=_VMEM_LIMIT),
        name="attn",
    )(q_t, qi_t, wi_t, k_hm, vt_hm, ki_all, tbl)


def _attn(q, qi, wi, k_all, v_all, ki_all, tbl, tbl_off, past):
    bsz, t, _ = q.shape
    n_keys = k_all.shape[1]
    _, kt, _, n_tiles = _attn_tiling(t, n_keys)
    s_pad = n_tiles * kt
    if s_pad != n_keys:
        k_all, v_all, ki_all = (jnp.pad(a, ((0, 0), (0, s_pad - n_keys), (0, 0))) for a in (k_all, v_all, ki_all))
    q_t = q.reshape(bsz, t, A_HEADS, HEAD_DIM).transpose(0, 2, 3, 1)
    qi_t = qi.reshape(bsz, t, IDX_HEADS, IDX_DIM).transpose(0, 2, 3, 1)
    k_hm = k_all.reshape(bsz, s_pad, A_HEADS, HEAD_DIM).transpose(0, 2, 1, 3)
    vt_hm = v_all.reshape(bsz, n_tiles, kt, A_HEADS, HEAD_DIM).transpose(0, 3, 1, 4, 2)
    o_t = _attn_head_major(q_t, qi_t, wi.transpose(0, 2, 1), k_hm, vt_hm, ki_all, tbl, tbl_off, past, n_keys)
    return o_t.transpose(0, 3, 1, 2).reshape(bsz, t, A_WIDTH)


def _gla_chunk(zb, misc, wg, bg, og, st):
    c = zb.shape[0]
    dkw = B_HEADS * B_DK
    q = zb[:, 0:dkw] * (B_DK ** -0.5)
    k = zb[:, dkw:2 * dkw]
    v3 = _heads3(zb[:, 2 * dkw:2 * dkw + B_WIDTH], B_HEADS, B_DV)
    gate = zb[:, 2 * dkw + B_WIDTH:]
    log_a = jax.nn.log_sigmoid(_mmx(misc, wg) + bg) / GATE_TAU
    b = _cumsum_rows(log_a)
    b_last = b[c - 1:c, :]
    b_mid = b[c // 2 - 1:c // 2, :]
    att = _bmm_nt(_heads3(q * jnp.exp(b - b_mid), B_HEADS, B_DK), _heads3(k * jnp.exp(b_mid - b), B_HEADS, B_DK))
    ri = lax.broadcasted_iota(jnp.int32, att.shape, 1)
    cj = lax.broadcasted_iota(jnp.int32, att.shape, 2)
    att = jnp.where(ri >= cj, att, 0.0)
    o3 = _bmm_nt(_heads3(q * jnp.exp(b), B_HEADS, B_DK), st) + _bmm(att, v3)
    upd = _bmm_tn(v3, _heads3(k * jnp.exp(b_last - b), B_HEADS, B_DK))
    st_new = st * _heads3(jnp.exp(b_last), B_HEADS, B_DK) + upd
    o = _lanes2(o3 * lax.rsqrt(jnp.mean(o3 * o3, axis=-1, keepdims=True) + EPS))
    return o * og * jax.nn.silu(gate), st_new


def _gla_body(zb_ref, misc_ref, wg_ref, bg_ref, og_ref, s0_ref, o_ref, s_out_ref, s_ref):
    @pl.when(pl.program_id(1) == 0)
    def _():
        s_ref[...] = s0_ref[...]

    for g in range(zb_ref.shape[0]):
        o, st = _gla_chunk(zb_ref[g], misc_ref[g], wg_ref[...], bg_ref[...], og_ref[...], s_ref[g])
        o_ref[g] = o
        s_ref[g] = st
        s_out_ref[g] = st


def _gla(zb, misc, wg_emb, bg, og, s0_t, c):
    bsz, t, _ = zb.shape
    dkw = B_HEADS * B_DK
    g = math.gcd(_SEQS_PER_STEP, bsz)
    blk = lambda w: pl.BlockSpec((g, c, w), lambda b, i: (b, i, 0))
    st = pl.BlockSpec((g, B_HEADS, B_DV, B_DK), lambda b, i: (b, 0, 0, 0))
    return pl.pallas_call(
        _gla_body,
        grid=(bsz // g, t // c),
        in_specs=[blk(_W_B), blk(_LANES), _const_spec((_LANES, dkw)), _const_spec((1, dkw)),
                  _const_spec((1, B_WIDTH)), st],
        out_specs=[blk(B_WIDTH), st],
        out_shape=(jax.ShapeDtypeStruct((bsz, t, B_WIDTH), F32),
                   jax.ShapeDtypeStruct((bsz, B_HEADS, B_DV, B_DK), F32)),
        scratch_shapes=[pltpu.VMEM((g, B_HEADS, B_DV, B_DK), F32)],
        compiler_params=pltpu.CompilerParams(dimension_semantics=("parallel", "arbitrary"),
                                             vmem_limit_bytes=_VMEM_LIMIT),
        name="gla",
    )(zb, misc, wg_emb, bg, og, s0_t)


_PAIRS = C_HEADS // 2


def _pairs3(x):
    return jnp.stack([x[:, p * _LANES:(p + 1) * _LANES] for p in range(_PAIRS)], axis=0)


def _pair_diag(x3):
    left = lax.broadcasted_iota(jnp.int32, x3.shape, 2) < x3.shape[2] // 2
    zero = jnp.zeros((), x3.dtype)
    return jnp.concatenate([jnp.where(left, x3, zero), jnp.where(left, zero, x3)], axis=1)


def _pair_diag_wide(x3):
    left = lax.broadcasted_iota(jnp.int32, x3.shape, 2) % _LANES < _LANES // 2
    zero = jnp.zeros((), x3.dtype)
    return jnp.concatenate([jnp.where(left, x3, zero), jnp.where(left, zero, x3)], axis=1)


def _pmm(a, b):
    return jnp.einsum("pij,pjk->pik", a, b, preferred_element_type=F32)


def _pmm3(a, b):
    (ah, al), (bh, bl) = a, b
    n = ah.shape[1]
    both = _pmm(jnp.concatenate([ah, al], axis=1), _pair_diag(bh))
    return both[:, 0:n, :] + (both[:, n:, :] + _pmm(ah, _pair_diag(bl)))


def _gdn_chunk(zc_ref, misc, cw_ref, alog, dtb, og, xpad_ref, s_bd):
    c = misc.shape[0]
    w3 = 3 * C_WIDTH
    bf = lambda x: x.astype(_MXU_DTYPE)
    xpad_ref[8:8 + c, :] = zc_ref[:, 0:w3]
    conv = xpad_ref[5:5 + c, :] * cw_ref[0:1, :]
    for i in range(1, CONV_W):
        conv = conv + xpad_ref[5 + i:5 + i + c, :] * cw_ref[i:i + 1, :]
    xpad_ref[0:8, :] = xpad_ref[c:c + 8, :]
    act = jax.nn.silu(conv)
    q2 = _group_norm(act[:, 0:C_WIDTH], C_DK, False) * (C_DK ** -0.5)
    k2 = _group_norm(act[:, C_WIDTH:2 * C_WIDTH], C_DK, False)
    v2 = act[:, 2 * C_WIDTH:w3]

    g_full = -jnp.exp(alog) * jax.nn.softplus(misc + dtb)
    beta_full = jax.nn.sigmoid(misc)
    gcum = _cumsum_rows(g_full)
    gexp2 = _expand_heads(gcum, _M_CA, C_HEADS, C_DV)
    bexp2 = _expand_heads(beta_full, _M_CB, C_HEADS, C_DV)
    g_last2 = gexp2[c - 1:c, :]
    eg2 = jnp.exp(gexp2)

    gcum_t = gcum.T
    col = lambda h: jnp.broadcast_to(gcum[:, _M_CA + h:_M_CA + h + 1], (c, c))
    row = lambda h: jnp.broadcast_to(gcum_t[_M_CA + h:_M_CA + h + 1, :], (c, c))
    diff = jnp.stack([jnp.concatenate([col(2 * p) - row(2 * p), col(2 * p + 1) - row(2 * p + 1)], axis=-1)
                      for p in range(_PAIRS)], axis=0)
    ri = lax.broadcasted_iota(jnp.int32, diff.shape, 1)
    cj = lax.broadcasted_iota(jnp.int32, diff.shape, 2) % c
    lmask = jnp.where(ri >= cj, jnp.exp(jnp.minimum(diff, 0.0)), 0.0)

    kb2 = k2 * bexp2
    k_bd = _pair_diag(bf(_pairs3(k2)))
    kq = jnp.einsum("pil,pjl->pij", bf(_pairs3(jnp.concatenate([kb2, q2], axis=0))), k_bd,
                    preferred_element_type=F32)
    yield
    m = jnp.where(ri > cj, kq[:, 0:c, :] * lmask, 0.0)
    qk = kq[:, c:, :] * lmask
    inv = jnp.where(ri == cj, 1.0, 0.0) - m
    pw = _split_bf16(m)
    for _ in range(int(math.log2(c)) - 1):
        sq = _pmm3(pw, pw)
        yield
        pw = _split_bf16(sq)
        step = _pmm3(_split_bf16(inv), pw)
        yield
        inv = inv + step
    rhs = jnp.concatenate([_pairs3(v2 * bexp2), _pairs3(kb2 * eg2)], axis=-1)
    uw = _pmm(bf(inv), _pair_diag_wide(bf(rhs)))
    yield
    u, w = uw[:, :, 0:_LANES], uw[:, :, _LANES:]
    wq = _pmm(bf(jnp.concatenate([w, _pairs3(q2 * eg2)], axis=1)), bf(s_bd))
    yield
    v_new = u - wq[:, 0:c, :]
    o3 = wq[:, c:, :] + _pmm(bf(qk), _pair_diag(bf(v_new)))
    upd = jnp.einsum("pck,pcv->pkv", bf(_pairs3(k2 * jnp.exp(g_last2 - gexp2))), bf(v_new),
                     preferred_element_type=F32)
    yield
    same_head = (lax.broadcasted_iota(jnp.int32, upd.shape, 1) // C_DK) == (lax.broadcasted_iota(jnp.int32, upd.shape, 2) // C_DV)
    s_new = s_bd * _pairs3(jnp.exp(g_last2)) + jnp.where(same_head, upd, 0.0)
    o2 = jnp.concatenate([o3[p] for p in range(_PAIRS)], axis=-1)
    return _group_norm(o2, C_DV, True) * og * jax.nn.silu(zc_ref[:, w3:]), s_new


def _gdn_body(zc_ref, misc_ref, cw_ref, alog_ref, dtb_ref, og_ref, s0_ref, hist_ref,
              o_ref, s_out_ref, xpad_ref, s_ref):
    n_seq = zc_ref.shape[0]

    @pl.when(pl.program_id(1) == 0)
    def _():
        xpad_ref[:, 0:8, :] = hist_ref[...]
        zero = jnp.zeros((C_DK, C_DV), F32)
        for g in range(n_seq):
            for p in range(_PAIRS):
                s_ref[g, p] = jnp.concatenate([jnp.concatenate([s0_ref[g, 2 * p], zero], axis=-1),
                                               jnp.concatenate([zero, s0_ref[g, 2 * p + 1]], axis=-1)], axis=0)

    chunks = [_gdn_chunk(zc_ref.at[g], misc_ref[g], cw_ref, alog_ref[...], dtb_ref[...], og_ref[...],
                         xpad_ref.at[g], s_ref[g]) for g in range(n_seq)]
    for g, (o, s_new) in enumerate(_round_robin(chunks)):
        o_ref[g] = o
        s_ref[g] = s_new

    @pl.when(pl.program_id(1) == pl.num_programs(1) - 1)
    def _():
        for g in range(n_seq):
            for p in range(_PAIRS):
                s_out_ref[g, 2 * p] = s_ref[g, p, 0:C_DK, 0:C_DV]
                s_out_ref[g, 2 * p + 1] = s_ref[g, p, C_DK:, C_DV:]


def _gdn(zc, misc, conv_w, alog_emb, dtb_emb, og, s0, hist8, c):
    bsz, t, _ = zc.shape
    w3 = 3 * C_WIDTH
    g = math.gcd(_SEQS_PER_STEP, bsz)
    blk = lambda w: pl.BlockSpec((g, c, w), lambda b, i: (b, i, 0))
    st = pl.BlockSpec((g, C_HEADS, C_DK, C_DV), lambda b, i: (b, 0, 0, 0))
    return pl.pallas_call(
        _gdn_body,
        grid=(bsz // g, t // c),
        in_specs=[blk(_W_C), blk(_LANES), _const_spec((CONV_W, w3)), _const_spec((1, _LANES)),
                  _const_spec((1, _LANES)), _const_spec((1, C_WIDTH)), st,
                  pl.BlockSpec((g, 8, w3), lambda b, i: (b, 0, 0))],
        out_specs=[blk(C_WIDTH), st],
        out_shape=(jax.ShapeDtypeStruct((bsz, t, C_WIDTH), F32),
                   jax.ShapeDtypeStruct((bsz, C_HEADS, C_DK, C_DV), F32)),
        scratch_shapes=[pltpu.VMEM((g, 8 + c, w3), F32), pltpu.VMEM((g, _PAIRS, 2 * C_DK, 2 * C_DV), F32)],
        compiler_params=pltpu.CompilerParams(dimension_semantics=("parallel", "arbitrary"),
                                             vmem_limit_bytes=_VMEM_LIMIT),
        name="gdn",
    )(zc, misc, conv_w, alog_emb, dtb_emb, og, s0, hist8)


def _mlp_body(x_ref, oa_ref, ob_ref, oc_ref, wo_ref, g2_ref, wup_ref, wdn_ref, y_ref, *, ff_tile, head_major):
    if head_major:
        oa = jnp.concatenate([oa_ref[h].T for h in range(A_HEADS)], axis=-1)
    else:
        oa = oa_ref[...]
    mixed = jnp.concatenate([oa, ob_ref[...], oc_ref[...]], axis=-1)
    y_ref[...] = x_ref[...] + _mm(mixed, wo_ref[...])
    h2 = (_rms(y_ref[...]) * g2_ref[...]).astype(_MXU_DTYPE)
    for j in range(D_FF // ff_tile):
        up = jnp.dot(h2, wup_ref[:, j * ff_tile:(j + 1) * ff_tile], preferred_element_type=F32)
        y_ref[...] += _mm(jnp.square(jnp.maximum(up, 0.0)), wdn_ref[j * ff_tile:(j + 1) * ff_tile, :])


_MLP_WEIGHT_SPECS = lambda: [_const_spec((D_MODEL, D_MODEL), True), _const_spec((1, D_MODEL)),
                             _const_spec((D_MODEL, D_FF), True), _const_spec((D_FF, D_MODEL), True)]


def _mlp(x2d, oa, ob, oc, wo, g2, wup, wdn):
    n = x2d.shape[0]
    tm = min(_ROW_TILE, n)
    assert n % tm == 0
    row = lambda w: pl.BlockSpec((tm, w), lambda i: (i, 0))
    return pl.pallas_call(
        functools.partial(_mlp_body, ff_tile=1024, head_major=False),
        grid=(n // tm,),
        in_specs=[row(D_MODEL), row(A_WIDTH), row(B_WIDTH), row(C_WIDTH)] + _MLP_WEIGHT_SPECS(),
        out_specs=row(D_MODEL),
        out_shape=jax.ShapeDtypeStruct((n, D_MODEL), F32),
        compiler_params=pltpu.CompilerParams(dimension_semantics=("parallel",), vmem_limit_bytes=_VMEM_LIMIT),
        name="mlp",
    )(x2d, oa, ob, oc, wo, g2, wup, wdn)


def _mlp_head_major(x, oa_t, ob, oc, wo, g2, wup, wdn):
    bsz, t, _ = x.shape
    tm = min(_ROW_TILE, t)
    assert t % tm == 0
    row = lambda w: pl.BlockSpec((None, tm, w), lambda b, i: (b, i, 0))
    return pl.pallas_call(
        functools.partial(_mlp_body, ff_tile=1024, head_major=True),
        grid=(bsz, t // tm),
        in_specs=[row(D_MODEL), pl.BlockSpec((None, A_HEADS, HEAD_DIM, tm), lambda b, i: (b, 0, 0, i)),
                  row(B_WIDTH), row(C_WIDTH)] + _MLP_WEIGHT_SPECS(),
        out_specs=row(D_MODEL),
        out_shape=jax.ShapeDtypeStruct((bsz, t, D_MODEL), F32),
        compiler_params=pltpu.CompilerParams(dimension_semantics=("parallel", "parallel"),
                                             vmem_limit_bytes=_VMEM_LIMIT),
        name="mlp",
    )(x, oa_t, ob, oc, wo, g2, wup, wdn)


def _prep_layer_params(ln1_g, w_in, a_qnorm_g, a_knorm_g, gla_w_gate, gla_b_gate, gla_onorm_g,
                       conv_w, gdn_a_log, gdn_dt_bias, gdn_onorm_g, w_o, ln2_g, w_up, w_down):
    cols = [jnp.zeros((D_MODEL, _PAD[name]), w_in.dtype) if name in _PAD
            else w_in[:, _SRC[name]:_SRC[name] + _SIZE[name]] for name in _NEW_ORDER]
    lane_emb = lambda vec, start: jnp.zeros((1, _LANES), F32).at[0, start:start + vec.shape[0]].set(vec.astype(F32))
    return dict(
        g1=ln1_g.reshape(1, D_MODEL).astype(F32),
        w_perm=jnp.concatenate(cols, axis=1).astype(_MXU_DTYPE),
        gq=jnp.tile(a_qnorm_g.astype(F32), A_HEADS).reshape(1, A_WIDTH),
        gk=jnp.tile(a_knorm_g.astype(F32), A_HEADS).reshape(1, A_WIDTH),
        wg_emb=jnp.zeros((_LANES, B_HEADS * B_DK), F32).at[_M_GLR:_M_GLR + GATE_RANK].set(gla_w_gate.astype(F32)),
        bg=gla_b_gate.reshape(1, -1).astype(F32),
        og_b=jnp.tile(gla_onorm_g.astype(F32), B_HEADS).reshape(1, B_WIDTH),
        conv_w=conv_w.astype(F32),
        alog_emb=lane_emb(gdn_a_log, _M_CA),
        dtb_emb=lane_emb(gdn_dt_bias, _M_CA),
        og_c=jnp.tile(gdn_onorm_g.astype(F32), C_HEADS).reshape(1, C_WIDTH),
        wo=w_o.astype(_MXU_DTYPE), g2=ln2_g.reshape(1, D_MODEL).astype(F32),
        wup=w_up.astype(_MXU_DTYPE), wdn=w_down.astype(_MXU_DTYPE),
    )


def _layer(x, cache, p, tbl, tbl_off):
    bsz, t, _ = x.shape
    n = bsz * t
    assert t >= CONV_W - 1
    c = min(CHUNK, t)
    r3 = lambda a: a.reshape(bsz, t, a.shape[-1])
    mixer_b = lambda zb, misc, s0: _gla(zb, misc, p["wg_emb"], p["bg"], p["og_b"], s0, c)
    mixer_c = lambda zc, misc, s0, hist8: _gdn(zc, misc, p["conv_w"], p["alog_emb"], p["dtb_emb"], p["og_c"], s0, hist8, c)

    if cache is None:
        _, kt, _, n_tiles = _attn_tiling(t, t)
        assert n_tiles * kt == t
        q_t, qi_t, wi_t, k_hm, vt_hm, ki16, ka, va, misc, zb, zc = _inproj_head_major(
            x, p["g1"], p["w_perm"], p["gq"], p["gk"], kt)
        o_a_t = _attn_head_major(q_t, qi_t, wi_t, k_hm, vt_hm, ki16, tbl, tbl_off, 0, t)
        o_b, s_gla_t = mixer_b(zb, misc, jnp.zeros((bsz, B_HEADS, B_DV, B_DK), F32))
        o_c, s_gdn = mixer_c(zc, misc, jnp.zeros((bsz, C_HEADS, C_DK, C_DV), F32), jnp.zeros((bsz, 8, 3 * C_WIDTH), F32))
        y = _mlp_head_major(x, o_a_t, o_b, o_c, p["wo"], p["g2"], p["wup"], p["wdn"])
    else:
        ck, cv, cki, sg, sd, cbuf = cache
        past = ck.shape[1]
        x2d = x.reshape(n, D_MODEL)
        qa, qi, ka, va, misc, zb, zc = _inproj(x2d, p["g1"], p["w_perm"], p["gq"], p["gk"])
        ka, va, misc, zb, zc = r3(ka), r3(va), r3(misc), r3(zb), r3(zc)
        k_all = jnp.concatenate([ck.reshape(bsz, past, A_WIDTH).astype(F32), ka], axis=1)
        v_all = jnp.concatenate([cv.reshape(bsz, past, A_WIDTH).astype(F32), va], axis=1)
        ki_all = jnp.concatenate([cki.astype(F32), misc[..., 0:IDX_DIM]], axis=1)
        o_a = _attn(r3(qa), r3(qi), misc[..., _M_WI:_M_WI + IDX_HEADS], k_all.astype(_MXU_DTYPE),
                    v_all.astype(_MXU_DTYPE), ki_all.astype(_MXU_DTYPE), tbl, tbl_off, past)
        o_b, s_gla_t = mixer_b(zb, misc, jnp.swapaxes(sg.astype(F32), 2, 3))
        hist8 = jnp.pad(cbuf.astype(F32), ((0, 0), (8 - (CONV_W - 1), 0), (0, 0)))
        o_c, s_gdn = mixer_c(zc, misc, sd.astype(F32), hist8)
        y = _mlp(x2d, o_a.reshape(n, A_WIDTH), o_b.reshape(n, B_WIDTH), o_c.reshape(n, C_WIDTH),
                 p["wo"], p["g2"], p["wup"], p["wdn"]).reshape(bsz, t, D_MODEL)

    s_gla = jnp.swapaxes(s_gla_t, 2, 3)
    new_conv = zc[:, t - (CONV_W - 1):, 0:3 * C_WIDTH]
    state = (ka.reshape(bsz, t, A_HEADS, HEAD_DIM), va.reshape(bsz, t, A_HEADS, HEAD_DIM), misc[..., 0:IDX_DIM],
             s_gla, s_gdn, new_conv)
    return y, state


def kernel(x_prompt, x_sample, cache_a_k, cache_a_v, cache_a_kidx, state_gla, state_gdn, state_conv, rel_bias, ln1_g, w_in, a_qnorm_g, a_knorm_g, gla_w_gate, gla_b_gate, gla_onorm_g, conv_w, gdn_a_log, gdn_dt_bias, gdn_onorm_g, w_o, ln2_g, w_up, w_down):
    depth = ln1_g.shape[0]
    yp, ys = x_prompt, x_sample
    past = cache_a_k.shape[2]
    tbl_p = _bias_tables(rel_bias, x_prompt.shape[1], x_prompt.shape[1], 0)
    tbl_s = _bias_tables(rel_bias, x_sample.shape[1], past + x_sample.shape[1], past)
    new_p, new_s = [], []
    for l in range(depth):
        p = _prep_layer_params(ln1_g[l], w_in[l], a_qnorm_g[l], a_knorm_g[l], gla_w_gate[l],
                               gla_b_gate[l], gla_onorm_g[l], conv_w[l], gdn_a_log[l], gdn_dt_bias[l],
                               gdn_onorm_g[l], w_o[l], ln2_g[l], w_up[l], w_down[l])
        yp, st_p = _layer(yp, None, p, *tbl_p)
        cache_l = (cache_a_k[l], cache_a_v[l], cache_a_kidx[l], state_gla[l], state_gdn[l], state_conv[l])
        ys, st_s = _layer(ys, cache_l, p, *tbl_s)
        new_p.append(st_p)
        new_s.append(st_s)
    stack = lambda states, i: jnp.stack([s[i] for s in states], axis=0)
    return (yp, ys) + tuple(stack(new_p, i) for i in range(6)) + tuple(stack(new_s, i) for i in range(6))
```

```python
import functools
import math

import numpy as np
import jax
import jax.numpy as jnp
from jax import lax
from jax.experimental import pallas as pl
from jax.experimental.pallas import tpu as pltpu

D_MODEL = 1024
CHUNK = 64
HEAD_DIM = 64
A_HEADS = 4
A_WIDTH = A_HEADS * HEAD_DIM
IDX_HEADS = 8
IDX_DIM = 64
TOPK_MAX = 256
N_BUCKETS = 32
MAX_DISTANCE = 128
B_HEADS = 4
B_DK = 32
B_DV = 64
B_WIDTH = B_HEADS * B_DV
GATE_RANK = 16
GATE_TAU = 16.0
C_HEADS = 8
C_DK = 64
C_DV = 64
C_WIDTH = C_HEADS * C_DV
CONV_W = 4
D_FF = 4 * D_MODEL
EPS = 1e-6

F32 = jnp.float32
BF16 = jnp.bfloat16
_MXU_DTYPE = BF16
_HI = lax.Precision.HIGHEST
_VMEM_LIMIT = 56 * 1024 * 1024
_LANES = 128
_ROW_TILE = 512
_SEQS_PER_STEP = 8
_NEG = -1e30
_LOG2E = math.log2(math.e)
_INT_MIN = -2 ** 31

_SPLIT_NAMES = ("aq", "ak", "av", "aqi", "aki", "awi", "bq", "bk", "bv", "bglr", "bog",
                "cq", "ck", "cv", "ca", "cb", "cog")
_SPLIT_SIZES = (A_WIDTH, A_WIDTH, A_WIDTH, IDX_HEADS * IDX_DIM, IDX_DIM, IDX_HEADS,
                B_HEADS * B_DK, B_HEADS * B_DK, B_WIDTH, GATE_RANK, B_WIDTH,
                C_HEADS * C_DK, C_HEADS * C_DK, C_WIDTH, C_HEADS, C_HEADS, C_WIDTH)
D_IN = sum(_SPLIT_SIZES)
_SRC = dict(zip(_SPLIT_NAMES, np.concatenate([[0], np.cumsum(_SPLIT_SIZES)[:-1]]).tolist()))
_SIZE = dict(zip(_SPLIT_NAMES, _SPLIT_SIZES))

_M_WI, _M_GLR, _M_CA, _M_CB = 64, 72, 88, 96
_NEW_ORDER = ("aq", "ak", "av", "aqi", "aki", "awi", "bglr", "ca", "cb", "pad24",
              "bq", "bk", "bv", "bog", "cq", "ck", "cv", "cog")
_PAD = {"pad24": 24}


def _new_offsets():
    offs, pos = {}, 0
    for name in _NEW_ORDER:
        offs[name] = pos
        pos += _PAD.get(name, 0) or _SIZE[name]
    return offs, pos


_OFF, D_Z = _new_offsets()
_C_QA, _C_KA, _C_VA, _C_QI, _C_MISC = _OFF["aq"], _OFF["ak"], _OFF["av"], _OFF["aqi"], _OFF["aki"]
_C_B, _C_C = _OFF["bq"], _OFF["cq"]
_W_B = 2 * B_HEADS * B_DK + 2 * B_WIDTH
_W_C = 4 * C_WIDTH
assert _C_MISC % _LANES == 0 and _C_B == _C_MISC + _LANES and _C_C == _C_B + _W_B and D_Z == _C_C + _W_C
assert (_OFF["awi"], _OFF["bglr"], _OFF["ca"], _OFF["cb"]) == tuple(_C_MISC + m for m in (_M_WI, _M_GLR, _M_CA, _M_CB))


def _mm(a, b):
    return jnp.dot(a.astype(_MXU_DTYPE), b.astype(_MXU_DTYPE), preferred_element_type=F32)


def _mmx(a, b):
    return jnp.dot(a, b, preferred_element_type=F32, precision=_HI)


def _split_bf16(x):
    hi = pltpu.bitcast(pltpu.bitcast(x, jnp.int32) & jnp.int32(-65536), F32)
    return hi.astype(BF16), (x - hi).astype(BF16)


def _cumsum_rows(x):
    r = lax.broadcasted_iota(jnp.int32, x.shape, 0)
    s = 1
    while s < x.shape[0]:
        x = x + jnp.where(r >= s, pltpu.roll(x, s, 0), 0.0)
        s *= 2
    return x


def _heads3(x, n_heads, width):
    return jnp.stack([x[:, h * width:(h + 1) * width] for h in range(n_heads)], axis=0)


def _lanes2(x3):
    return jnp.concatenate([x3[h] for h in range(x3.shape[0])], axis=-1)


def _expand_heads(x, lane0, n_heads, width):
    c = x.shape[0]
    return jnp.concatenate([jnp.broadcast_to(x[:, lane0 + h:lane0 + h + 1], (c, width)) for h in range(n_heads)],
                           axis=-1)


def _bmm(a, b):
    return jnp.einsum("hij,hjk->hik", a.astype(_MXU_DTYPE), b.astype(_MXU_DTYPE), preferred_element_type=F32)


def _bmm_nt(a, b):
    return jnp.einsum("hik,hjk->hij", a.astype(_MXU_DTYPE), b.astype(_MXU_DTYPE), preferred_element_type=F32)


def _bmm_tn(a, b):
    return jnp.einsum("hck,hcv->hkv", a.astype(_MXU_DTYPE), b.astype(_MXU_DTYPE), preferred_element_type=F32)


def _rms(x):
    return x * lax.rsqrt(jnp.mean(x * x, axis=-1, keepdims=True) + EPS)


def _group_norm(x, group, mean):
    outs = []
    for g in range(x.shape[-1] // group):
        xs = x[:, g * group:(g + 1) * group]
        ss = jnp.sum(xs * xs, axis=-1, keepdims=True)
        outs.append(xs * lax.rsqrt((ss / group if mean else ss) + EPS))
    return jnp.concatenate(outs, axis=-1)


def _fold_rows(w, rows=8):
    parts = [w[i * rows:(i + 1) * rows, :] for i in range(w.shape[0] // rows)]
    while len(parts) > 1:
        parts = [parts[i] + parts[i + 1] for i in range(0, len(parts), 2)]
    return parts[0]


def _round_robin(gens):
    results = [None] * len(gens)
    live = list(range(len(gens)))
    while live:
        for i in list(live):
            try:
                next(gens[i])
            except StopIteration as stop:
                results[i] = stop.value
                live.remove(i)
    return results


def _const_spec(shape, single_buffer=False):
    zeros = (0,) * len(shape)
    if single_buffer:
        return pl.BlockSpec(shape, lambda *_: zeros, pipeline_mode=pl.Buffered(1))
    return pl.BlockSpec(shape, lambda *_: zeros)


def _inproj_body(x_ref, g1_ref, w_ref, gq_ref, gk_ref, *out_refs, head_major, kt):
    h = (_rms(x_ref[...]) * g1_ref[...]).astype(_MXU_DTYPE)

    def proj(c0, width):
        return jnp.dot(h, w_ref[:, c0:c0 + width], preferred_element_type=F32)

    qn = _group_norm(proj(_C_QA, A_WIDTH), HEAD_DIM, True) * gq_ref[...] * (HEAD_DIM ** -0.5 * _LOG2E)
    kn = _group_norm(proj(_C_KA, A_WIDTH), HEAD_DIM, True) * gk_ref[...]
    v = proj(_C_VA, A_WIDTH)
    qi = proj(_C_QI, IDX_HEADS * IDX_DIM) * (IDX_DIM ** -0.5)
    misc = proj(_C_MISC, _LANES)
    if head_major:
        qt_ref, qit_ref, wit_ref, khm_ref, vt_ref, ki_ref, ka_ref, va_ref, misc_ref, zb_ref, zc_ref = out_refs
        for hd in range(A_HEADS):
            hs = slice(hd * HEAD_DIM, (hd + 1) * HEAD_DIM)
            qt_ref[hd] = qn[:, hs].T.astype(qt_ref.dtype)
            khm_ref[hd] = kn[:, hs].astype(khm_ref.dtype)
            for j in range(v.shape[0] // kt):
                vt_ref[hd, j] = v[j * kt:(j + 1) * kt, hs].T.astype(vt_ref.dtype)
        for hd in range(IDX_HEADS):
            qit_ref[hd] = qi[:, hd * IDX_DIM:(hd + 1) * IDX_DIM].T.astype(qit_ref.dtype)
        wit_ref[...] = misc.T[_M_WI:_M_WI + IDX_HEADS, :]
        ki_ref[...] = misc[:, 0:IDX_DIM].astype(ki_ref.dtype)
    else:
        qa_ref, qi_ref, ka_ref, va_ref, misc_ref, zb_ref, zc_ref = out_refs
        qa_ref[...] = qn.astype(qa_ref.dtype)
        qi_ref[...] = qi.astype(qi_ref.dtype)
    ka_ref[...] = kn
    va_ref[...] = v
    misc_ref[...] = misc
    zb_ref[...] = proj(_C_B, _W_B)
    zc_ref[...] = proj(_C_C, _W_C)


def _inproj(x2d, g1, w_perm, gq, gk):
    n = x2d.shape[0]
    tm = min(_ROW_TILE, n)
    assert n % tm == 0
    row = lambda w: pl.BlockSpec((tm, w), lambda i: (i, 0))
    widths = (A_WIDTH, IDX_HEADS * IDX_DIM, A_WIDTH, A_WIDTH, _LANES, _W_B, _W_C)
    dtypes = (_MXU_DTYPE, _MXU_DTYPE, F32, F32, F32, F32, F32)
    return pl.pallas_call(
        functools.partial(_inproj_body, head_major=False, kt=None),
        grid=(n // tm,),
        in_specs=[row(D_MODEL), _const_spec((1, D_MODEL)), _const_spec((D_MODEL, D_Z), True),
                  _const_spec((1, A_WIDTH)), _const_spec((1, A_WIDTH))],
        out_specs=[row(w) for w in widths],
        out_shape=tuple(jax.ShapeDtypeStruct((n, w), d) for w, d in zip(widths, dtypes)),
        compiler_params=pltpu.CompilerParams(dimension_semantics=("parallel",), vmem_limit_bytes=_VMEM_LIMIT),
        name="inproj",
    )(x2d, g1, w_perm, gq, gk)


def _inproj_head_major(x, g1, w_perm, gq, gk, kt):
    bsz, t, _ = x.shape
    tm = min(_ROW_TILE, t)
    assert t % tm == 0 and tm % kt == 0
    row = lambda w: pl.BlockSpec((None, tm, w), lambda b, i: (b, i, 0))
    rows = lambda w, d: jax.ShapeDtypeStruct((bsz, t, w), d)
    lanes = lambda nh: pl.BlockSpec((None, nh, HEAD_DIM, tm), lambda b, i: (b, 0, 0, i))
    out_specs = [lanes(A_HEADS), lanes(IDX_HEADS), pl.BlockSpec((None, IDX_HEADS, tm), lambda b, i: (b, 0, i)),
                 pl.BlockSpec((None, A_HEADS, tm, HEAD_DIM), lambda b, i: (b, 0, i, 0)),
                 pl.BlockSpec((None, A_HEADS, tm // kt, HEAD_DIM, kt), lambda b, i: (b, 0, i, 0, 0)),
                 row(IDX_DIM), row(A_WIDTH), row(A_WIDTH), row(_LANES), row(_W_B), row(_W_C)]
    out_shape = (jax.ShapeDtypeStruct((bsz, A_HEADS, HEAD_DIM, t), _MXU_DTYPE),
                 jax.ShapeDtypeStruct((bsz, IDX_HEADS, IDX_DIM, t), _MXU_DTYPE),
                 jax.ShapeDtypeStruct((bsz, IDX_HEADS, t), F32),
                 jax.ShapeDtypeStruct((bsz, A_HEADS, t, HEAD_DIM), _MXU_DTYPE),
                 jax.ShapeDtypeStruct((bsz, A_HEADS, t // kt, HEAD_DIM, kt), _MXU_DTYPE),
                 rows(IDX_DIM, _MXU_DTYPE), rows(A_WIDTH, F32), rows(A_WIDTH, F32), rows(_LANES, F32),
                 rows(_W_B, F32), rows(_W_C, F32))
    return pl.pallas_call(
        functools.partial(_inproj_body, head_major=True, kt=kt),
        grid=(bsz, t // tm),
        in_specs=[row(D_MODEL), _const_spec((1, D_MODEL)), _const_spec((D_MODEL, D_Z), True),
                  _const_spec((1, A_WIDTH)), _const_spec((1, A_WIDTH))],
        out_specs=out_specs,
        out_shape=out_shape,
        compiler_params=pltpu.CompilerParams(dimension_semantics=("parallel", "parallel"),
                                             vmem_limit_bytes=_VMEM_LIMIT),
        name="inproj",
    )(x, g1, w_perm, gq, gk)


def _attn_body(q_ref, qi_ref, wi_ref, k_ref, vt_ref, ki_ref, tbl_ref, o_ref, key_ref, half_ref, sc_ref, sc2_ref, acc_ref, s_ref, s2_ref, p_ref, p2_ref, *,
               tq, kt, n_tiles, past, n_keys, topk, tbl_off, n_tbl, idx_bits):
    qb = pl.program_id(1)
    q0 = past + qb * tq
    n_kt = jnp.minimum(n_tiles, (q0 + tq + kt - 1) // kt)
    krow = lax.broadcasted_iota(jnp.int32, (kt, tq), 0)
    q_chunk = (q0 + lax.broadcasted_iota(jnp.int32, (1, tq), 1)) // CHUNK
    adm_end = jnp.minimum((q_chunk + 1) * CHUNK, n_keys)
    wi = wi_ref[...] * (IDX_HEADS ** -0.5)

    n_pair = (n_kt + 1) // 2

    def head_scores(t, dst_ref):
        ki_t = ki_ref[pl.ds(pl.multiple_of(t * kt, kt), kt), :]
        for h in range(IDX_HEADS):
            dst_ref[h] = _mm(ki_t, qi_ref[h])

    def combine(t, src_ref):
        acc = jnp.zeros((kt, tq), F32)
        for h in range(IDX_HEADS):
            acc = acc + wi[h:h + 1, :] * jnp.maximum(src_ref[h], 0.0)
        bits = pltpu.bitcast(acc, jnp.int32)
        bits = jnp.where(bits == _INT_MIN, 0, bits)
        key = jnp.where(bits >= 0, bits, bits ^ jnp.int32(0x7FFFFFFF))
        key = jnp.where(krow < adm_end - t * kt, key, _INT_MIN)
        key_ref[t] = key
        half_ref[t] = (key >> 16).astype(jnp.int16)

    head_scores(0, sc_ref)

    def score_pair(j, carry):
        t0 = 2 * j
        head_scores(t0 + 1, sc2_ref)
        combine(t0, sc_ref)
        head_scores(jnp.minimum(t0 + 2, 2 * n_pair - 2), sc_ref)
        combine(t0 + 1, sc2_ref)
        return carry

    lax.fori_loop(0, n_pair, score_pair, 0)

    def count(*pred_fns):
        def body(j, cnts):
            t = 2 * j
            k0, k1 = key_ref[t], key_ref[t + 1]
            return tuple(cnt + _fold_rows(jnp.where(fn(t, k0), 1.0, 0.0) + jnp.where(fn(t + 1, k1), 1.0, 0.0))
                         for fn, cnt in zip(pred_fns, cnts))
        cnts = lax.fori_loop(0, n_pair, body, tuple(jnp.zeros((8, tq), F32) for _ in pred_fns))
        return tuple(jnp.sum(cnt, axis=0, keepdims=True) for cnt in cnts)

    def count16(pred_fn):
        one, zero = jnp.int16(1), jnp.int16(0)
        def body(j, cnt):
            w = jnp.where(pred_fn(half_ref[2 * j]), one, zero) + jnp.where(pred_fn(half_ref[2 * j + 1]), one, zero)
            return cnt + _fold_rows(w, 16)
        cnt = lax.fori_loop(0, n_pair, body, jnp.zeros((16, tq), jnp.int16))
        return jnp.sum(cnt.astype(jnp.int32), axis=0, keepdims=True)

    def kth_largest16(kth):
        def bit(i, v):
            cand = v + lax.shift_left(jnp.int32(1), 15 - i)
            cand16 = cand.astype(jnp.int16)
            return jnp.where(count16(lambda k: k >= cand16) >= kth, cand, v)
        return lax.fori_loop(0, 16, bit, jnp.full((1, tq), -2 ** 15, jnp.int32))

    thr_hi = kth_largest16(topk)
    thr_hi16 = thr_hi.astype(jnp.int16)
    above = count16(lambda k: k > thr_hi16)

    def low_halves(t, carry):
        key = key_ref[t]
        half_ref[t] = jnp.where((key >> 16) == thr_hi, (key & 0xFFFF) - 2 ** 15, -2 ** 15).astype(jnp.int16)
        return carry

    lax.fori_loop(0, 2 * n_pair, low_halves, 0)
    thr = thr_hi * 2 ** 16 + (kth_largest16(topk - above) + 2 ** 15)
    thr = jnp.maximum(thr, _INT_MIN + 1)

    cnt_gt, cnt_ge = count(lambda t, k: k > thr, lambda t, k: k >= thr)
    need = topk - cnt_gt
    has_excess = jnp.max(jnp.where(cnt_ge > topk, 1.0, 0.0)) > 0.0

    def tie_search():
        def bit(i, last):
            cand = last + lax.shift_left(jnp.int32(1), idx_bits - 1 - i)
            below, = count(lambda t, k: (k == thr) & ((t * kt + krow) < cand))
            return jnp.where(below < need, cand, last)
        return lax.fori_loop(0, idx_bits, bit, jnp.zeros((1, tq), jnp.int32))

    last = lax.cond(has_excess, tie_search, lambda: jnp.full((1, tq), 2 ** 30, jnp.int32))

    acc_ref[...] = jnp.zeros(acc_ref.shape, F32)

    def logits(t, dst_ref):
        off = pl.multiple_of(t * kt, kt)
        for h in range(A_HEADS):
            dst_ref[h] = _mm(k_ref[h, pl.ds(off, kt), :], q_ref[h])

    def softmax_pv(t, src_ref, pr_ref, carry, ties):
        ms, ls = carry
        off = pl.multiple_of(t * kt, kt)
        key = key_ref[t]
        if ties:
            sel = (key > thr) | ((key == thr) & ((off + krow) <= last))
        else:
            sel = key >= thr
        ti = jnp.clip(t - qb + tbl_off, 0, n_tbl - 1)
        new_ms, new_ls, alphas = [], [], []
        for h in range(A_HEADS):
            s = jnp.where(sel, src_ref[h] + tbl_ref[ti, h], _NEG)
            m_new = jnp.maximum(ms[h], jnp.max(s, axis=0, keepdims=True))
            alpha = jnp.exp2(ms[h] - m_new)
            p = jnp.exp2(s - m_new)
            new_ls.append(alpha * ls[h] + jnp.sum(p, axis=0, keepdims=True))
            new_ms.append(m_new)
            alphas.append(alpha)
            pr_ref[h] = p.astype(pr_ref.dtype)
        for h in range(A_HEADS):
            acc_ref[h] = alphas[h] * acc_ref[h] + jnp.dot(vt_ref[h, t], pr_ref[h], preferred_element_type=F32)
        return tuple(new_ms), tuple(new_ls)

    logits(0, s_ref)

    def attend(ties):
        def attend_pair(j, carry):
            t0 = 2 * j
            logits(t0 + 1, s2_ref)
            carry = softmax_pv(t0, s_ref, p_ref, carry, ties)
            logits(jnp.minimum(t0 + 2, 2 * n_pair - 2), s_ref)
            return softmax_pv(t0 + 1, s2_ref, p2_ref, carry, ties)

        m0 = tuple(jnp.full((1, tq), _NEG, F32) for _ in range(A_HEADS))
        l0 = tuple(jnp.zeros((1, tq), F32) for _ in range(A_HEADS))
        return lambda: lax.fori_loop(0, n_pair, attend_pair, (m0, l0))

    _, ls = lax.cond(has_excess, attend(True), attend(False))
    for h in range(A_HEADS):
        o_ref[h] = acc_ref[h] / ls[h]


def _rel_bucket(rel):
    nb = N_BUCKETS // 2
    ret = jnp.where(rel > 0, nb, 0)
    n = jnp.abs(rel)
    max_exact = nb // 2
    nf = jnp.maximum(n, 1).astype(F32)
    large = max_exact + (jnp.log(nf / max_exact) / math.log(MAX_DISTANCE / max_exact)
                         * (nb - max_exact)).astype(jnp.int32)
    large = jnp.minimum(large, nb - 1)
    return ret + jnp.where(n < max_exact, n, large)


def _attn_tiling(t, n_keys):
    kt = 256
    tq = min(kt, t)
    assert t % tq == 0
    return tq, kt, t // tq, 2 * -(-n_keys // (2 * kt))


def _bias_tables(rel_bias, t, n_keys, past):
    tq, kt, nqb, n_tiles = _attn_tiling(t, n_keys)
    j = jnp.arange(kt)[:, None]
    i = jnp.arange(tq)[None, :]
    if nqb == 1:
        rels = [tile * kt + j - (past + i) for tile in range(n_tiles)]
        off = 0
    else:
        assert past == 0 and tq == kt and kt >= MAX_DISTANCE
        rels = [j - i - 2 * kt, j - i - kt, j - i]
        off = 2
    onehot = jax.nn.one_hot(_rel_bucket(jnp.stack(rels, axis=0)), N_BUCKETS, dtype=F32)
    tbl = jnp.einsum("nktb,bh->nhkt", onehot, rel_bias.astype(F32) * _LOG2E, precision=_HI)
    return tbl, off


def _attn_head_major(q_t, qi_t, wi_t, k_hm, vt_hm, ki_all, tbl, tbl_off, past, n_keys):
    bsz, _, _, t = q_t.shape
    topk = min(TOPK_MAX, n_keys // 4)
    tq, kt, nqb, n_tiles = _attn_tiling(t, n_keys)
    s_pad = n_tiles * kt
    assert s_pad >= topk and k_hm.shape[2] == s_pad and vt_hm.shape[2] == n_tiles and ki_all.shape[1] == s_pad
    n_tbl = tbl.shape[0]
    body = functools.partial(_attn_body, tq=tq, kt=kt, n_tiles=n_tiles, past=past, n_keys=n_keys, topk=topk,
                             tbl_off=tbl_off, n_tbl=n_tbl, idx_bits=int(s_pad).bit_length())
    return pl.pallas_call(
        body,
        grid=(bsz, nqb),
        in_specs=[pl.BlockSpec((None, A_HEADS, HEAD_DIM, tq), lambda b, i: (b, 0, 0, i)),
                  pl.BlockSpec((None, IDX_HEADS, IDX_DIM, tq), lambda b, i: (b, 0, 0, i)),
                  pl.BlockSpec((None, IDX_HEADS, tq), lambda b, i: (b, 0, i)),
                  pl.BlockSpec((None, A_HEADS, s_pad, HEAD_DIM), lambda b, i: (b, 0, 0, 0)),
                  pl.BlockSpec((None, A_HEADS, n_tiles, HEAD_DIM, kt), lambda b, i: (b, 0, 0, 0, 0)),
                  pl.BlockSpec((None, s_pad, IDX_DIM), lambda b, i: (b, 0, 0)),
                  _const_spec((n_tbl, A_HEADS, kt, tq), True)],
        out_specs=pl.BlockSpec((None, A_HEADS, HEAD_DIM, tq), lambda b, i: (b, 0, 0, i)),
        out_shape=jax.ShapeDtypeStruct((bsz, A_HEADS, HEAD_DIM, t), F32),
        scratch_shapes=[pltpu.VMEM((n_tiles, kt, tq), jnp.int32), pltpu.VMEM((n_tiles, kt, tq), jnp.int16),
                        pltpu.VMEM((IDX_HEADS, kt, tq), F32), pltpu.VMEM((IDX_HEADS, kt, tq), F32),
                        pltpu.VMEM((A_HEADS, HEAD_DIM, tq), F32),
                        pltpu.VMEM((A_HEADS, kt, tq), F32), pltpu.VMEM((A_HEADS, kt, tq), F32),
                        pltpu.VMEM((A_HEADS, kt, tq), _MXU_DTYPE), pltpu.VMEM((A_HEADS, kt, tq), _MXU_DTYPE)],
        compiler_params=pltpu.CompilerParams(dimension_semantics=("parallel", "parallel"),
                                             vmem_limit_bytes=_VMEM_LIMIT),
        name="attn",
    )(q_t, qi_t, wi_t, k_hm, vt_hm, ki_all, tbl)


def _attn(q, qi, wi, k_all, v_all, ki_all, tbl, tbl_off, past):
    bsz, t, _ = q.shape
    n_keys = k_all.shape[1]
    _, kt, _, n_tiles = _attn_tiling(t, n_keys)
    s_pad = n_tiles * kt
    if s_pad != n_keys:
        k_all, v_all, ki_all = (jnp.pad(a, ((0, 0), (0, s_pad - n_keys), (0, 0))) for a in (k_all, v_all, ki_all))
    q_t = q.reshape(bsz, t, A_HEADS, HEAD_DIM).transpose(0, 2, 3, 1)
    qi_t = qi.reshape(bsz, t, IDX_HEADS, IDX_DIM).transpose(0, 2, 3, 1)
    k_hm = k_all.reshape(bsz, s_pad, A_HEADS, HEAD_DIM).transpose(0, 2, 1, 3)
    vt_hm = v_all.reshape(bsz, n_tiles, kt, A_HEADS, HEAD_DIM).transpose(0, 3, 1, 4, 2)
    o_t = _attn_head_major(q_t, qi_t, wi.transpose(0, 2, 1), k_hm, vt_hm, ki_all, tbl, tbl_off, past, n_keys)
    return o_t.transpose(0, 3, 1, 2).reshape(bsz, t, A_WIDTH)


def _gla_chunk(zb, misc, wg, bg, og, st):
    c = zb.shape[0]
    dkw = B_HEADS * B_DK
    q = zb[:, 0:dkw] * (B_DK ** -0.5)
    k = zb[:, dkw:2 * dkw]
    v3 = _heads3(zb[:, 2 * dkw:2 * dkw + B_WIDTH], B_HEADS, B_DV)
    gate = zb[:, 2 * dkw + B_WIDTH:]
    log_a = jax.nn.log_sigmoid(_mmx(misc, wg) + bg) / GATE_TAU
    b = _cumsum_rows(log_a)
    b_last = b[c - 1:c, :]
    b_mid = b[c // 2 - 1:c // 2, :]
    att = _bmm_nt(_heads3(q * jnp.exp(b - b_mid), B_HEADS, B_DK), _heads3(k * jnp.exp(b_mid - b), B_HEADS, B_DK))
    inter = _bmm_nt(_heads3(q * jnp.exp(b), B_HEADS, B_DK), st)
    upd = _bmm_tn(v3, _heads3(k * jnp.exp(b_last - b), B_HEADS, B_DK))
    yield
    ri = lax.broadcasted_iota(jnp.int32, att.shape, 1)
    cj = lax.broadcasted_iota(jnp.int32, att.shape, 2)
    att = jnp.where(ri >= cj, att, 0.0)
    o3 = inter + _bmm(att, v3)
    yield
    st_new = st * _heads3(jnp.exp(b_last), B_HEADS, B_DK) + upd
    o = _lanes2(o3 * lax.rsqrt(jnp.mean(o3 * o3, axis=-1, keepdims=True) + EPS))
    return o * og * jax.nn.silu(gate), st_new


def _gla_body(zb_ref, misc_ref, wg_ref, bg_ref, og_ref, s0_ref, o_ref, s_out_ref, s_ref):
    @pl.when(pl.program_id(1) == 0)
    def _():
        s_ref[...] = s0_ref[...]

    chunks = [_gla_chunk(zb_ref[g], misc_ref[g], wg_ref[...], bg_ref[...], og_ref[...], s_ref[g])
              for g in range(zb_ref.shape[0])]
    for g, (o, st) in enumerate(_round_robin(chunks)):
        o_ref[g] = o
        s_ref[g] = st
        s_out_ref[g] = st


def _gla(zb, misc, wg_emb, bg, og, s0_t, c):
    bsz, t, _ = zb.shape
    dkw = B_HEADS * B_DK
    g = math.gcd(_SEQS_PER_STEP, bsz)
    blk = lambda w: pl.BlockSpec((g, c, w), lambda b, i: (b, i, 0))
    st = pl.BlockSpec((g, B_HEADS, B_DV, B_DK), lambda b, i: (b, 0, 0, 0))
    return pl.pallas_call(
        _gla_body,
        grid=(bsz // g, t // c),
        in_specs=[blk(_W_B), blk(_LANES), _const_spec((_LANES, dkw)), _const_spec((1, dkw)),
                  _const_spec((1, B_WIDTH)), st],
        out_specs=[blk(B_WIDTH), st],
        out_shape=(jax.ShapeDtypeStruct((bsz, t, B_WIDTH), F32),
                   jax.ShapeDtypeStruct((bsz, B_HEADS, B_DV, B_DK), F32)),
        scratch_shapes=[pltpu.VMEM((g, B_HEADS, B_DV, B_DK), F32)],
        compiler_params=pltpu.CompilerParams(dimension_semantics=("parallel", "arbitrary"),
                                             vmem_limit_bytes=_VMEM_LIMIT),
        name="gla",
    )(zb, misc, wg_emb, bg, og, s0_t)


_PAIRS = C_HEADS // 2


def _pairs3(x):
    return jnp.stack([x[:, p * _LANES:(p + 1) * _LANES] for p in range(_PAIRS)], axis=0)


def _pair_diag(x3):
    left = lax.broadcasted_iota(jnp.int32, x3.shape, 2) < x3.shape[2] // 2
    zero = jnp.zeros((), x3.dtype)
    return jnp.concatenate([jnp.where(left, x3, zero), jnp.where(left, zero, x3)], axis=1)


def _pair_diag_wide(x3):
    left = lax.broadcasted_iota(jnp.int32, x3.shape, 2) % _LANES < _LANES // 2
    zero = jnp.zeros((), x3.dtype)
    return jnp.concatenate([jnp.where(left, x3, zero), jnp.where(left, zero, x3)], axis=1)


def _pmm(a, b):
    return jnp.einsum("pij,pjk->pik", a, b, preferred_element_type=F32)


def _pmm3(a, b):
    (ah, al), (bh, bl) = a, b
    n = ah.shape[1]
    both = _pmm(jnp.concatenate([ah, al], axis=1), _pair_diag(bh))
    return both[:, 0:n, :] + (both[:, n:, :] + _pmm(ah, _pair_diag(bl)))


def _gdn_chunk(zc_ref, misc, cw_ref, alog, dtb, og, xpad_ref, s_bd):
    c = misc.shape[0]
    w3 = 3 * C_WIDTH
    bf = lambda x: x.astype(_MXU_DTYPE)
    xpad_ref[8:8 + c, :] = zc_ref[:, 0:w3]
    conv = xpad_ref[5:5 + c, :] * cw_ref[0:1, :]
    for i in range(1, CONV_W):
        conv = conv + xpad_ref[5 + i:5 + i + c, :] * cw_ref[i:i + 1, :]
    xpad_ref[0:8, :] = xpad_ref[c:c + 8, :]
    act = jax.nn.silu(conv)
    q2 = _group_norm(act[:, 0:C_WIDTH], C_DK, False) * (C_DK ** -0.5)
    k2 = _group_norm(act[:, C_WIDTH:2 * C_WIDTH], C_DK, False)
    v2 = act[:, 2 * C_WIDTH:w3]

    g_full = -jnp.exp(alog) * jax.nn.softplus(misc + dtb)
    beta_full = jax.nn.sigmoid(misc)
    gcum = _cumsum_rows(g_full)
    gexp2 = _expand_heads(gcum, _M_CA, C_HEADS, C_DV)
    bexp2 = _expand_heads(beta_full, _M_CB, C_HEADS, C_DV)
    g_last2 = gexp2[c - 1:c, :]
    eg2 = jnp.exp(gexp2)

    gcum_t = gcum.T
    col = lambda h: jnp.broadcast_to(gcum[:, _M_CA + h:_M_CA + h + 1], (c, c))
    row = lambda h: jnp.broadcast_to(gcum_t[_M_CA + h:_M_CA + h + 1, :], (c, c))
    diff = jnp.stack([jnp.concatenate([col(2 * p) - row(2 * p), col(2 * p + 1) - row(2 * p + 1)], axis=-1)
                      for p in range(_PAIRS)], axis=0)
    ri = lax.broadcasted_iota(jnp.int32, diff.shape, 1)
    cj = lax.broadcasted_iota(jnp.int32, diff.shape, 2) % c
    lmask = jnp.where(ri >= cj, jnp.exp(jnp.minimum(diff, 0.0)), 0.0)

    kb2 = k2 * bexp2
    k_bd = _pair_diag(bf(_pairs3(k2)))
    kq = jnp.einsum("pil,pjl->pij", bf(_pairs3(jnp.concatenate([kb2, q2], axis=0))), k_bd,
                    preferred_element_type=F32)
    yield
    m = jnp.where(ri > cj, kq[:, 0:c, :] * lmask, 0.0)
    qk = kq[:, c:, :] * lmask
    inv = jnp.where(ri == cj, 1.0, 0.0) - m
    pw = _split_bf16(m)
    for _ in range(int(math.log2(c)) - 1):
        sq = _pmm3(pw, pw)
        yield
        pw = _split_bf16(sq)
        step = _pmm3(_split_bf16(inv), pw)
        yield
        inv = inv + step
    rhs = jnp.concatenate([_pairs3(v2 * bexp2), _pairs3(kb2 * eg2)], axis=-1)
    uw = _pmm(bf(inv), _pair_diag_wide(bf(rhs)))
    yield
    u, w = uw[:, :, 0:_LANES], uw[:, :, _LANES:]
    wq = _pmm(bf(jnp.concatenate([w, _pairs3(q2 * eg2)], axis=1)), bf(s_bd))
    yield
    v_new = u - wq[:, 0:c, :]
    o3 = wq[:, c:, :] + _pmm(bf(qk), _pair_diag(bf(v_new)))
    upd = jnp.einsum("pck,pcv->pkv", bf(_pairs3(k2 * jnp.exp(g_last2 - gexp2))), bf(v_new),
                     preferred_element_type=F32)
    yield
    same_head = (lax.broadcasted_iota(jnp.int32, upd.shape, 1) // C_DK) == (lax.broadcasted_iota(jnp.int32, upd.shape, 2) // C_DV)
    s_new = s_bd * _pairs3(jnp.exp(g_last2)) + jnp.where(same_head, upd, 0.0)
    o2 = jnp.concatenate([o3[p] for p in range(_PAIRS)], axis=-1)
    return _group_norm(o2, C_DV, True) * og * jax.nn.silu(zc_ref[:, w3:]), s_new


def _gdn_body(zc_ref, misc_ref, cw_ref, alog_ref, dtb_ref, og_ref, s0_ref, hist_ref,
              o_ref, s_out_ref, xpad_ref, s_ref):
    n_seq = zc_ref.shape[0]

    @pl.when(pl.program_id(1) == 0)
    def _():
        xpad_ref[:, 0:8, :] = hist_ref[...]
        zero = jnp.zeros((C_DK, C_DV), F32)
        for g in range(n_seq):
            for p in range(_PAIRS):
                s_ref[g, p] = jnp.concatenate([jnp.concatenate([s0_ref[g, 2 * p], zero], axis=-1),
                                               jnp.concatenate([zero, s0_ref[g, 2 * p + 1]], axis=-1)], axis=0)

    chunks = [_gdn_chunk(zc_ref.at[g], misc_ref[g], cw_ref, alog_ref[...], dtb_ref[...], og_ref[...],
                         xpad_ref.at[g], s_ref[g]) for g in range(n_seq)]
    for g, (o, s_new) in enumerate(_round_robin(chunks)):
        o_ref[g] = o
        s_ref[g] = s_new

    @pl.when(pl.program_id(1) == pl.num_programs(1) - 1)
    def _():
        for g in range(n_seq):
            for p in range(_PAIRS):
                s_out_ref[g, 2 * p] = s_ref[g, p, 0:C_DK, 0:C_DV]
                s_out_ref[g, 2 * p + 1] = s_ref[g, p, C_DK:, C_DV:]


def _gdn(zc, misc, conv_w, alog_emb, dtb_emb, og, s0, hist8, c):
    bsz, t, _ = zc.shape
    w3 = 3 * C_WIDTH
    g = math.gcd(_SEQS_PER_STEP, bsz)
    blk = lambda w: pl.BlockSpec((g, c, w), lambda b, i: (b, i, 0))
    st = pl.BlockSpec((g, C_HEADS, C_DK, C_DV), lambda b, i: (b, 0, 0, 0))
    return pl.pallas_call(
        _gdn_body,
        grid=(bsz // g, t // c),
        in_specs=[blk(_W_C), blk(_LANES), _const_spec((CONV_W, w3)), _const_spec((1, _LANES)),
                  _const_spec((1, _LANES)), _const_spec((1, C_WIDTH)), st,
                  pl.BlockSpec((g, 8, w3), lambda b, i: (b, 0, 0))],
        out_specs=[blk(C_WIDTH), st],
        out_shape=(jax.ShapeDtypeStruct((bsz, t, C_WIDTH), F32),
                   jax.ShapeDtypeStruct((bsz, C_HEADS, C_DK, C_DV), F32)),
        scratch_shapes=[pltpu.VMEM((g, 8 + c, w3), F32), pltpu.VMEM((g, _PAIRS, 2 * C_DK, 2 * C_DV), F32)],
        compiler_params=pltpu.CompilerParams(dimension_semantics=("parallel", "arbitrary"),
                                             vmem_limit_bytes=_VMEM_LIMIT),
        name="gdn",
    )(zc, misc, conv_w, alog_emb, dtb_emb, og, s0, hist8)


def _mlp_body(x_ref, oa_ref, ob_ref, oc_ref, wo_ref, g2_ref, wup_ref, wdn_ref, y_ref, *, ff_tile, head_major):
    if head_major:
        oa = jnp.concatenate([oa_ref[h].T for h in range(A_HEADS)], axis=-1)
    else:
        oa = oa_ref[...]
    mixed = jnp.concatenate([oa, ob_ref[...], oc_ref[...]], axis=-1)
    y_ref[...] = x_ref[...] + _mm(mixed, wo_ref[...])
    h2 = (_rms(y_ref[...]) * g2_ref[...]).astype(_MXU_DTYPE)
    for j in range(D_FF // ff_tile):
        up = jnp.dot(h2, wup_ref[:, j * ff_tile:(j + 1) * ff_tile], preferred_element_type=F32)
        y_ref[...] += _mm(jnp.square(jnp.maximum(up, 0.0)), wdn_ref[j * ff_tile:(j + 1) * ff_tile, :])


_MLP_WEIGHT_SPECS = lambda: [_const_spec((D_MODEL, D_MODEL), True), _const_spec((1, D_MODEL)),
                             _const_spec((D_MODEL, D_FF), True), _const_spec((D_FF, D_MODEL), True)]


def _mlp(x2d, oa, ob, oc, wo, g2, wup, wdn):
    n = x2d.shape[0]
    tm = min(_ROW_TILE, n)
    assert n % tm == 0
    row = lambda w: pl.BlockSpec((tm, w), lambda i: (i, 0))
    return pl.pallas_call(
        functools.partial(_mlp_body, ff_tile=1024, head_major=False),
        grid=(n // tm,),
        in_specs=[row(D_MODEL), row(A_WIDTH), row(B_WIDTH), row(C_WIDTH)] + _MLP_WEIGHT_SPECS(),
        out_specs=row(D_MODEL),
        out_shape=jax.ShapeDtypeStruct((n, D_MODEL), F32),
        compiler_params=pltpu.CompilerParams(dimension_semantics=("parallel",), vmem_limit_bytes=_VMEM_LIMIT),
        name="mlp",
    )(x2d, oa, ob, oc, wo, g2, wup, wdn)


def _mlp_head_major(x, oa_t, ob, oc, wo, g2, wup, wdn):
    bsz, t, _ = x.shape
    tm = min(_ROW_TILE, t)
    assert t % tm == 0
    row = lambda w: pl.BlockSpec((None, tm, w), lambda b, i: (b, i, 0))
    return pl.pallas_call(
        functools.partial(_mlp_body, ff_tile=1024, head_major=True),
        grid=(bsz, t // tm),
        in_specs=[row(D_MODEL), pl.BlockSpec((None, A_HEADS, HEAD_DIM, tm), lambda b, i: (b, 0, 0, i)),
                  row(B_WIDTH), row(C_WIDTH)] + _MLP_WEIGHT_SPECS(),
        out_specs=row(D_MODEL),
        out_shape=jax.ShapeDtypeStruct((bsz, t, D_MODEL), F32),
        compiler_params=pltpu.CompilerParams(dimension_semantics=("parallel", "parallel"),
                                             vmem_limit_bytes=_VMEM_LIMIT),
        name="mlp",
    )(x, oa_t, ob, oc, wo, g2, wup, wdn)


def _prep_layer_params(ln1_g, w_in, a_qnorm_g, a_knorm_g, gla_w_gate, gla_b_gate, gla_onorm_g,
                       conv_w, gdn_a_log, gdn_dt_bias, gdn_onorm_g, w_o, ln2_g, w_up, w_down):
    cols = [jnp.zeros((D_MODEL, _PAD[name]), w_in.dtype) if name in _PAD
            else w_in[:, _SRC[name]:_SRC[name] + _SIZE[name]] for name in _NEW_ORDER]
    lane_emb = lambda vec, start: jnp.zeros((1, _LANES), F32).at[0, start:start + vec.shape[0]].set(vec.astype(F32))
    return dict(
        g1=ln1_g.reshape(1, D_MODEL).astype(F32),
        w_perm=jnp.concatenate(cols, axis=1).astype(_MXU_DTYPE),
        gq=jnp.tile(a_qnorm_g.astype(F32), A_HEADS).reshape(1, A_WIDTH),
        gk=jnp.tile(a_knorm_g.astype(F32), A_HEADS).reshape(1, A_WIDTH),
        wg_emb=jnp.zeros((_LANES, B_HEADS * B_DK), F32).at[_M_GLR:_M_GLR + GATE_RANK].set(gla_w_gate.astype(F32)),
        bg=gla_b_gate.reshape(1, -1).astype(F32),
        og_b=jnp.tile(gla_onorm_g.astype(F32), B_HEADS).reshape(1, B_WIDTH),
        conv_w=conv_w.astype(F32),
        alog_emb=lane_emb(gdn_a_log, _M_CA),
        dtb_emb=lane_emb(gdn_dt_bias, _M_CA),
        og_c=jnp.tile(gdn_onorm_g.astype(F32), C_HEADS).reshape(1, C_WIDTH),
        wo=w_o.astype(_MXU_DTYPE), g2=ln2_g.reshape(1, D_MODEL).astype(F32),
        wup=w_up.astype(_MXU_DTYPE), wdn=w_down.astype(_MXU_DTYPE),
    )


def _layer(x, cache, p, tbl, tbl_off):
    bsz, t, _ = x.shape
    n = bsz * t
    assert t >= CONV_W - 1
    c = min(CHUNK, t)
    r3 = lambda a: a.reshape(bsz, t, a.shape[-1])
    mixer_b = lambda zb, misc, s0: _gla(zb, misc, p["wg_emb"], p["bg"], p["og_b"], s0, c)
    mixer_c = lambda zc, misc, s0, hist8: _gdn(zc, misc, p["conv_w"], p["alog_emb"], p["dtb_emb"], p["og_c"], s0, hist8, c)

    if cache is None:
        _, kt, _, n_tiles = _attn_tiling(t, t)
        assert n_tiles * kt == t
        q_t, qi_t, wi_t, k_hm, vt_hm, ki16, ka, va, misc, zb, zc = _inproj_head_major(
            x, p["g1"], p["w_perm"], p["gq"], p["gk"], kt)
        o_a_t = _attn_head_major(q_t, qi_t, wi_t, k_hm, vt_hm, ki16, tbl, tbl_off, 0, t)
        o_b, s_gla_t = mixer_b(zb, misc, jnp.zeros((bsz, B_HEADS, B_DV, B_DK), F32))
        o_c, s_gdn = mixer_c(zc, misc, jnp.zeros((bsz, C_HEADS, C_DK, C_DV), F32), jnp.zeros((bsz, 8, 3 * C_WIDTH), F32))
        y = _mlp_head_major(x, o_a_t, o_b, o_c, p["wo"], p["g2"], p["wup"], p["wdn"])
    else:
        ck, cv, cki, sg, sd, cbuf = cache
        past = ck.shape[1]
        x2d = x.reshape(n, D_MODEL)
        qa, qi, ka, va, misc, zb, zc = _inproj(x2d, p["g1"], p["w_perm"], p["gq"], p["gk"])
        ka, va, misc, zb, zc = r3(ka), r3(va), r3(misc), r3(zb), r3(zc)
        k_all = jnp.concatenate([ck.reshape(bsz, past, A_WIDTH).astype(F32), ka], axis=1)
        v_all = jnp.concatenate([cv.reshape(bsz, past, A_WIDTH).astype(F32), va], axis=1)
        ki_all = jnp.concatenate([cki.astype(F32), misc[..., 0:IDX_DIM]], axis=1)
        o_a = _attn(r3(qa), r3(qi), misc[..., _M_WI:_M_WI + IDX_HEADS], k_all.astype(_MXU_DTYPE),
                    v_all.astype(_MXU_DTYPE), ki_all.astype(_MXU_DTYPE), tbl, tbl_off, past)
        o_b, s_gla_t = mixer_b(zb, misc, jnp.swapaxes(sg.astype(F32), 2, 3))
        hist8 = jnp.pad(cbuf.astype(F32), ((0, 0), (8 - (CONV_W - 1), 0), (0, 0)))
        o_c, s_gdn = mixer_c(zc, misc, sd.astype(F32), hist8)
        y = _mlp(x2d, o_a.reshape(n, A_WIDTH), o_b.reshape(n, B_WIDTH), o_c.reshape(n, C_WIDTH),
                 p["wo"], p["g2"], p["wup"], p["wdn"]).reshape(bsz, t, D_MODEL)

    s_gla = jnp.swapaxes(s_gla_t, 2, 3)
    new_conv = zc[:, t - (CONV_W - 1):, 0:3 * C_WIDTH]
    state = (ka.reshape(bsz, t, A_HEADS, HEAD_DIM), va.reshape(bsz, t, A_HEADS, HEAD_DIM), misc[..., 0:IDX_DIM],
             s_gla, s_gdn, new_conv)
    return y, state


def kernel(x_prompt, x_sample, cache_a_k, cache_a_v, cache_a_kidx, state_gla, state_gdn, state_conv, rel_bias, ln1_g, w_in, a_qnorm_g, a_knorm_g, gla_w_gate, gla_b_gate, gla_onorm_g, conv_w, gdn_a_log, gdn_dt_bias, gdn_onorm_g, w_o, ln2_g, w_up, w_down):
    depth = ln1_g.shape[0]
    yp, ys = x_prompt, x_sample
    past = cache_a_k.shape[2]
    tbl_p = _bias_tables(rel_bias, x_prompt.shape[1], x_prompt.shape[1], 0)
    tbl_s = _bias_tables(rel_bias, x_sample.shape[1], past + x_sample.shape[1], past)
    new_p, new_s = [], []
    for l in range(depth):
        p = _prep_layer_params(ln1_g[l], w_in[l], a_qnorm_g[l], a_knorm_g[l], gla_w_gate[l],
                               gla_b_gate[l], gla_onorm_g[l], conv_w[l], gdn_a_log[l], gdn_dt_bias[l],
                               gdn_onorm_g[l], w_o[l], ln2_g[l], w_up[l], w_down[l])
        yp, st_p = _layer(yp, None, p, *tbl_p)
        cache_l = (cache_a_k[l], cache_a_v[l], cache_a_kidx[l], state_gla[l], state_gdn[l], state_conv[l])
        ys, st_s = _layer(ys, cache_l, p, *tbl_s)
        new_p.append(st_p)
        new_s.append(st_s)
    stack = lambda states, i: jnp.stack([s[i] for s in states], axis=0)
    return (yp, ys) + tuple(stack(new_p, i) for i in range(6)) + tuple(stack(new_s, i) for i in range(6))
```
